```python
import jax, jax.numpy as jnp
from jax import lax
import numpy as np

D_MODEL = 4096
BATCH = 4
SEQ = 2048
DEPTH = 1

GRID_W = 64
CTX_LEN = 256
M_HEADS = 8
M_DQK = 256
M_DV = 512
M_CHUNK = 64
A_HEADS = 32
A_KV_HEADS = 8
A_HEAD_DIM = 128
WINDOW = 128
A_BLOCK = 128
ROPE_BASE = 10000.0
D_FF = 4 * D_MODEL
EPS = 1e-6

M_QK_W = M_HEADS * M_DQK
M_V_W = M_HEADS * M_DV
A_Q_W = A_HEADS * A_HEAD_DIM
A_KV_W = A_KV_HEADS * A_HEAD_DIM
SPLITS = (M_QK_W, M_QK_W, M_V_W, M_V_W, 4 * M_HEADS, A_Q_W, A_KV_W, A_KV_W, D_MODEL, D_MODEL)
PROJ_W = sum(SPLITS)

kernel_name = "hybrid_mlstm_swa_dit_block"


def rmsnorm(x, g):
    xf = x.astype(jnp.float32)
    y = xf * lax.rsqrt(jnp.mean(xf * xf, axis=-1, keepdims=True) + EPS)
    return (y * g.astype(jnp.float32)).astype(x.dtype)


def modulate(h, shift, scale):
    return h * (1.0 + scale) + shift


def split_cols(a):
    idx = [int(i) for i in np.cumsum(SPLITS)[:-1]]
    return jnp.split(a, idx, axis=-1)


def _rotate(x, ang):
    x1, x2 = jnp.split(x, 2, axis=-1)
    cos = jnp.cos(ang)[:, None, :].astype(x.dtype)
    sin = jnp.sin(ang)[:, None, :].astype(x.dtype)
    return jnp.concatenate([x1 * cos - x2 * sin, x1 * sin + x2 * cos], axis=-1)


def rope_2d(x, row, col):
    half = x.shape[-1] // 2
    nf = half // 2
    freqs = ROPE_BASE ** (-jnp.arange(nf, dtype=jnp.float32) / nf)
    ang_r = row.astype(jnp.float32)[:, None] * freqs
    ang_c = col.astype(jnp.float32)[:, None] * freqs
    return jnp.concatenate([_rotate(x[..., :half], ang_r), _rotate(x[..., half:], ang_c)], axis=-1)


def mlstm_scan(q, k, v, ig, lf, state):
    B, S, H, _ = q.shape
    nc = S // M_CHUNK
    L = M_CHUNK
    tril = jnp.tril(jnp.ones((L, L), dtype=bool))

    def chunks(a):
        return jnp.moveaxis(a.reshape((B, nc, L) + a.shape[2:]), 1, 0)

    def step(carry, xs):
        C, n, m = carry
        qc, kc, vc, igc, lfc = xs
        b = jnp.cumsum(lfc, axis=1)
        a = b + m[:, None, :]
        Dm = b[:, :, None, :] - b[:, None, :, :] + igc[:, None, :, :]
        Dm = jnp.where(tril[None, :, :, None], Dm, -jnp.inf)
        m_t = jnp.maximum(a, jnp.max(Dm, axis=2))
        s = jnp.einsum('bthd,bshd->btsh', qc, kc) * jnp.exp(Dm - m_t[:, :, None, :])
        inter = jnp.exp(a - m_t)
        num = inter[..., None] * jnp.einsum('bthd,bhvd->bthv', qc, C) + jnp.einsum('btsh,bshv->bthv', s, vc)
        den = inter * jnp.einsum('bthd,bhd->bth', qc, n) + jnp.sum(s, axis=2)
        h = num / jnp.maximum(jnp.abs(den), jnp.exp(-m_t))[..., None]
        bL = b[:, -1]
        w = bL[:, None, :] - b + igc
        m_new = jnp.maximum(bL + m, jnp.max(w, axis=1))
        decay = jnp.exp(bL + m - m_new)
        wexp = jnp.exp(w - m_new[:, None, :])
        C_new = decay[:, :, None, None] * C + jnp.einsum('bsh,bshv,bshd->bhvd', wexp, vc, kc)
        n_new = decay[:, :, None] * n + jnp.einsum('bsh,bshd->bhd', wexp, kc)
        return (C_new, n_new, m_new), h

    state, hs = lax.scan(step, state, (chunks(q), chunks(k), chunks(v), chunks(ig), chunks(lf)))
    h = jnp.moveaxis(hs, 0, 1).reshape(B, S, H, v.shape[-1])
    return h, state


def mlstm_bidir(qm, km, vm, gm, qmc, kmc, vmc, gmc, gate_b):
    f32 = jnp.float32
    B, S = qm.shape[:2]
    Cn = qmc.shape[1]

    def prep(q, k, v, g, n_tok):
        q = q.astype(f32).reshape(B, n_tok, M_HEADS, M_DQK) * (M_DQK ** -0.5)
        k = k.astype(f32).reshape(B, n_tok, M_HEADS, M_DQK)
        v = v.astype(f32).reshape(B, n_tok, M_HEADS, M_DV)
        g = g.astype(f32).reshape(B, n_tok, 4, M_HEADS) + gate_b.astype(f32)
        return q, k, v, g[:, :, 0], jax.nn.log_sigmoid(g[:, :, 1]), g[:, :, 2], jax.nn.log_sigmoid(g[:, :, 3])

    q, k, v, ig_f, lf_f, ig_b, lf_b = prep(qm, km, vm, gm, S)
    qc, kc, vc, igc_f, lfc_f, igc_b, lfc_b = prep(qmc, kmc, vmc, gmc, Cn)
    zero = (jnp.zeros((B, M_HEADS, M_DV, M_DQK), f32), jnp.zeros((B, M_HEADS, M_DQK), f32),
            jnp.zeros((B, M_HEADS), f32))
    rev = lambda a: jnp.flip(a, axis=1)
    hc_f, st_f = mlstm_scan(qc, kc, vc, igc_f, lfc_f, zero)
    h_f, _ = mlstm_scan(q, k, v, ig_f, lf_f, st_f)
    hc_b, st_b = mlstm_scan(rev(qc), rev(kc), rev(vc), rev(igc_b), rev(lfc_b), zero)
    h_b, _ = mlstm_scan(rev(q), rev(k), rev(v), rev(ig_b), rev(lf_b), st_b)
    return h_f + rev(h_b), hc_f + rev(hc_b)


def mlstm_readout(h, o, norm_g):
    B, N = h.shape[:2]
    h = h * lax.rsqrt(jnp.mean(h * h, axis=-1, keepdims=True) + EPS)
    h = h.reshape(B, N, M_V_W) * norm_g.astype(jnp.float32) * jax.nn.sigmoid(o.astype(jnp.float32))
    return h.astype(o.dtype)


def window_attention(q, k, v, kc, vc, sink):
    B, S, Hq, d = q.shape
    Hkv = k.shape[2]
    G = Hq // Hkv
    Cn = kc.shape[1]
    nb = S // A_BLOCK
    scale = d ** -0.5
    pad = ((0, 0), (A_BLOCK, A_BLOCK), (0, 0), (0, 0))
    kp = jnp.pad(k, pad)
    vp = jnp.pad(v, pad)
    qg = q.reshape(B, S, Hkv, G, d)
    sink_l = sink.astype(jnp.float32).reshape(Hkv, G)
    offs_q = jnp.arange(A_BLOCK)
    offs_k = jnp.arange(3 * A_BLOCK) - A_BLOCK

    def block(n):
        start = n * A_BLOCK
        qb = lax.dynamic_slice_in_dim(qg, start, A_BLOCK, axis=1)
        kb = lax.dynamic_slice_in_dim(kp, start, 3 * A_BLOCK, axis=1)
        vb = lax.dynamic_slice_in_dim(vp, start, 3 * A_BLOCK, axis=1)
        qpos = start + offs_q
        kpos = start + offs_k
        valid = (jnp.abs(qpos[:, None] - kpos[None, :]) <= WINDOW) & (kpos >= 0)[None, :] & (kpos < S)[None, :]
        s_w = jnp.einsum('bqhgd,bkhd->bhgqk', qb, kb).astype(jnp.float32) * scale
        s_w = jnp.where(valid, s_w, -jnp.inf)
        s_c = jnp.einsum('bqhgd,bchd->bhgqc', qb, kc).astype(jnp.float32) * scale
        s_s = jnp.broadcast_to(sink_l[None, :, :, None, None], (B, Hkv, G, A_BLOCK, 1))
        p = jax.nn.softmax(jnp.concatenate([s_w, s_c, s_s], axis=-1), axis=-1)
        pw = p[..., :3 * A_BLOCK].astype(v.dtype)
        pc = p[..., 3 * A_BLOCK:3 * A_BLOCK + Cn].astype(v.dtype)
        o = jnp.einsum('bhgqk,bkhd->bqhgd', pw, vb) + jnp.einsum('bhgqc,bchd->bqhgd', pc, vc)
        return o.reshape(B, A_BLOCK, Hq * d)

    out = lax.map(block, jnp.arange(nb))
    return jnp.moveaxis(out, 0, 1).reshape(B, S, Hq * d)


def context_attention(qc, kc, vc, sink):
    B, Cn, Hq, d = qc.shape
    Hkv = kc.shape[2]
    G = Hq // Hkv
    qg = qc.reshape(B, Cn, Hkv, G, d)
    s = jnp.einsum('bqhgd,bkhd->bhgqk', qg, kc).astype(jnp.float32) * (d ** -0.5)
    s_s = jnp.broadcast_to(sink.astype(jnp.float32).reshape(Hkv, G)[None, :, :, None, None], (B, Hkv, G, Cn, 1))
    p = jax.nn.softmax(jnp.concatenate([s, s_s], axis=-1), axis=-1)[..., :Cn]
    o = jnp.einsum('bhgqk,bkhd->bqhgd', p.astype(vc.dtype), vc)
    return o.reshape(B, Cn, Hq * d)


def token_mixers(u, uc, row, col, w_in, m_gate_b, m_norm_g, sink, w_out_m, w_out_a, w_o, ctx_out):
    B, S, _ = u.shape
    Cn = uc.shape[1]
    qm, km, vm, om, gm, qa, ka, va, bg_m, bg_a = split_cols(u @ w_in)
    qmc, kmc, vmc, omc, gmc, qac, kac, vac, bgc_m, bgc_a = split_cols(uc @ w_in)
    hm, hm_c = mlstm_bidir(qm, km, vm, gm, qmc, kmc, vmc, gmc, m_gate_b)
    y_m = mlstm_readout(hm, om, m_norm_g) @ w_out_m
    q = rope_2d(qa.reshape(B, S, A_HEADS, A_HEAD_DIM), row, col)
    k = rope_2d(ka.reshape(B, S, A_KV_HEADS, A_HEAD_DIM), row, col)
    v = va.reshape(B, S, A_KV_HEADS, A_HEAD_DIM)
    kc = kac.reshape(B, Cn, A_KV_HEADS, A_HEAD_DIM)
    vc = vac.reshape(B, Cn, A_KV_HEADS, A_HEAD_DIM)
    y_a = window_attention(q, k, v, kc, vc, sink) @ w_out_a
    mix = (jax.nn.sigmoid(bg_m) * y_m + jax.nn.sigmoid(bg_a) * y_a) @ w_o
    mix_c = None
    if ctx_out:
        y_m_c = mlstm_readout(hm_c, omc, m_norm_g) @ w_out_m
        y_a_c = context_attention(qac.reshape(B, Cn, A_HEADS, A_HEAD_DIM), kc, vc, sink) @ w_out_a
        mix_c = (jax.nn.sigmoid(bgc_m) * y_m_c + jax.nn.sigmoid(bgc_a) * y_a_c) @ w_o
    return mix, mix_c


def sq_relu_mlp(u, w1, w2):
    return jnp.square(jax.nn.relu(u @ w1)) @ w2


def setup_inputs(seed: int = 0) -> dict:
    key = jax.random.key(seed)
    ks = jax.random.split(key, 16)
    D = D_MODEL

    def nrm(k, shape, scale):
        return jax.random.normal(k, shape, jnp.float32) * scale

    gate_base = jnp.array([0.0, 3.0, 0.0, 3.0], jnp.float32)[None, :, None]
    return {
        "x": nrm(ks[0], (BATCH, SEQ, D), 1.0),
        "c": nrm(ks[1], (BATCH, D), 1.0),
        "ctx": nrm(ks[2], (BATCH, CTX_LEN, D), 1.0),
        "c_ctx": nrm(ks[3], (D,), 1.0),
        "w_mod": nrm(ks[4], (DEPTH, D, 6 * D), 0.5 * D ** -0.5),
        "b_mod": nrm(ks[5], (DEPTH, 6 * D), 0.02),
        "norm_g": 1.0 + nrm(ks[6], (DEPTH, 4, D), 0.02),
        "w_in": nrm(ks[7], (DEPTH, D, PROJ_W), D ** -0.5),
        "m_gate_b": gate_base + nrm(ks[8], (DEPTH, 4, M_HEADS), 0.1),
        "m_norm_g": 1.0 + nrm(ks[9], (DEPTH, M_V_W), 0.02),
        "attn_sink": nrm(ks[10], (DEPTH, A_HEADS), 0.5),
        "w_out_m": nrm(ks[11], (DEPTH, M_V_W, D), M_V_W ** -0.5),
        "w_out_a": nrm(ks[12], (DEPTH, A_Q_W, D), A_Q_W ** -0.5),
        "w_o": nrm(ks[13], (DEPTH, D, D), D ** -0.5),
        "w_ff1": nrm(ks[14], (DEPTH, D, D_FF), D ** -0.5),
        "w_ff2": nrm(ks[15], (DEPTH, D_FF, D), D_FF ** -0.5),
    }


def reference(x, c, ctx, c_ctx, w_mod, b_mod, norm_g, w_in, m_gate_b, m_norm_g, attn_sink,
              w_out_m, w_out_a, w_o, w_ff1, w_ff2):
    B, S, _ = x.shape
    ROWS = S // GRID_W
    row, col = jnp.meshgrid(jnp.arange(ROWS), jnp.arange(GRID_W), indexing='ij')
    row = row.reshape(-1)
    col = col.reshape(-1)
    sc = jax.nn.silu(c)
    scc = jax.nn.silu(c_ctx)
    h, hc = x, ctx
    for l in range(DEPTH):
        ctx_out = l < DEPTH - 1
        mod = (sc @ w_mod[l] + b_mod[l])[:, None, :]
        mod_c = scc @ w_mod[l] + b_mod[l]
        sh1, s1, g1, sh2, s2, g2 = jnp.split(mod, 6, axis=-1)
        csh1, cs1, cg1, csh2, cs2, cg2 = jnp.split(mod_c, 6, axis=-1)
        u = modulate(rmsnorm(h, norm_g[l, 0]), sh1, s1)
        uc = modulate(rmsnorm(hc, norm_g[l, 0]), csh1, cs1)
        mix, mix_c = token_mixers(u, uc, row, col, w_in[l], m_gate_b[l], m_norm_g[l], attn_sink[l],
                                  w_out_m[l], w_out_a[l], w_o[l], ctx_out)
        h = h + g1 * rmsnorm(mix, norm_g[l, 1])
        u2 = modulate(rmsnorm(h, norm_g[l, 2]), sh2, s2)
        h = h + g2 * rmsnorm(sq_relu_mlp(u2, w_ff1[l], w_ff2[l]), norm_g[l, 3])
        if ctx_out:
            hc = hc + cg1 * rmsnorm(mix_c, norm_g[l, 1])
            uc2 = modulate(rmsnorm(hc, norm_g[l, 2]), csh2, cs2)
            hc = hc + cg2 * rmsnorm(sq_relu_mlp(uc2, w_ff1[l], w_ff2[l]), norm_g[l, 3])
    return h
```

```python
import functools

import jax
import jax.numpy as jnp
from jax import lax
from jax.experimental import pallas as pl
from jax.experimental.pallas import tpu as pltpu

F32 = jnp.float32
BF16 = jnp.bfloat16

D_MODEL = 4096
GRID_W = 64
CTX_LEN = 256
M_HEADS = 8
M_DQK = 256
M_DV = 512
A_HEADS = 32
A_KV_HEADS = 8
A_HEAD_DIM = 128
A_GROUP = A_HEADS // A_KV_HEADS
WINDOW = 128
A_BLOCK = 128
ROPE_BASE = 10000.0
D_FF = 4 * D_MODEL
EPS = 1e-6

M_QK_W = M_HEADS * M_DQK
M_V_W = M_HEADS * M_DV
A_Q_W = A_HEADS * A_HEAD_DIM
A_KV_W = A_KV_HEADS * A_HEAD_DIM
N_GATES = 4 * M_HEADS

W_IN_GATE_LO = 2 * M_QK_W + 2 * M_V_W
W_IN_GATE_HI = W_IN_GATE_LO + N_GATES
OFF_QM = 0
OFF_KM = OFF_QM + M_QK_W
OFF_VM = OFF_KM + M_QK_W
OFF_OM = OFF_VM + M_V_W
OFF_QA = OFF_OM + M_V_W
OFF_KA = OFF_QA + A_Q_W
OFF_VA = OFF_KA + A_KV_W
OFF_BGM = OFF_VA + A_KV_W
OFF_BGA = OFF_BGM + D_MODEL
PROJ_ALIGNED_W = OFF_BGA + D_MODEL
CTX_OFF_KM = 0
CTX_OFF_VM = M_QK_W
CTX_OFF_KA = M_QK_W + M_V_W
CTX_OFF_VA = CTX_OFF_KA + A_KV_W
CTX_PROJ_W = CTX_OFF_VA + A_KV_W

M_CHUNK = 256
Q_SCALE = M_DQK ** -0.5
LANES = 128
GATE_PAD = LANES
MOD_ROWS = 8
VMEM_LIMIT = 56 * 1024 * 1024


def _params(n_axes):
    return pltpu.CompilerParams(dimension_semantics=("arbitrary",) * n_axes, vmem_limit_bytes=VMEM_LIMIT)


def _sigmoid(x):
    return 1.0 / (1.0 + jnp.exp(-x))


def _log_sigmoid(x):
    return jnp.minimum(x, 0.0) - jnp.log1p(jnp.exp(-jnp.abs(x)))


def _rms(x):
    return x * lax.rsqrt(jnp.mean(x * x, axis=-1, keepdims=True) + EPS)


def _mod_kernel(c_ref, w_ref, b_ref, o_ref):
    c = c_ref[...]
    sc = (c * _sigmoid(c)).astype(BF16)
    o_ref[...] = jnp.dot(sc, w_ref[...].astype(BF16), preferred_element_type=F32) + b_ref[...]


def _modulation(c_all, w_mod, b_mod, tn=1024):
    d, n = w_mod.shape
    return pl.pallas_call(
        _mod_kernel,
        grid=(n // tn,),
        in_specs=[pl.BlockSpec((MOD_ROWS, d), lambda j: (0, 0)),
                  pl.BlockSpec((d, tn), lambda j: (0, j)),
                  pl.BlockSpec((1, tn), lambda j: (0, j))],
        out_specs=pl.BlockSpec((MOD_ROWS, tn), lambda j: (0, j)),
        out_shape=jax.ShapeDtypeStruct((MOD_ROWS, n), F32),
        compiler_params=_params(1),
        name="modulation",
    )(c_all, w_mod, b_mod.reshape(1, n))


def _prenorm_kernel(x_ref, g_ref, sh_ref, sc_ref, o_ref):
    y = _rms(x_ref[...]) * g_ref[...]
    o_ref[...] = (y * (1.0 + sc_ref[...]) + sh_ref[...]).astype(o_ref.dtype)


def _mod_spec(slot, rows_per_mod_row, tm, fixed_row=None):
    if fixed_row is None:
        return pl.BlockSpec((None, None, 1, D_MODEL), lambda i: ((i * tm) // rows_per_mod_row, slot, 0, 0))
    return pl.BlockSpec((None, None, 1, D_MODEL), lambda i: (fixed_row, slot, 0, 0))


def _norm_spec(slot):
    return pl.BlockSpec((None, 1, D_MODEL), lambda i: (slot, 0, 0))


def _prenorm(x2, norm_g, mod, rows_per_mod_row, fixed_row=None, tm=256):
    m = x2.shape[0]
    return pl.pallas_call(
        _prenorm_kernel,
        grid=(m // tm,),
        in_specs=[pl.BlockSpec((tm, D_MODEL), lambda i: (i, 0)),
                  _norm_spec(0),
                  _mod_spec(0, rows_per_mod_row, tm, fixed_row),
                  _mod_spec(1, rows_per_mod_row, tm, fixed_row)],
        out_specs=pl.BlockSpec((tm, D_MODEL), lambda i: (i, 0)),
        out_shape=jax.ShapeDtypeStruct((m, D_MODEL), BF16),
        compiler_params=_params(1),
        name="prenorm",
    )(x2, norm_g, mod, mod)


def _post_mix_kernel(x_ref, mix_ref, gn1_ref, gn2_ref, g1_ref, sh2_ref, s2_ref, h1_ref, u2_ref):
    h1 = x_ref[...] + g1_ref[...] * (_rms(mix_ref[...]) * gn1_ref[...])
    h1_ref[...] = h1
    y2 = _rms(h1) * gn2_ref[...]
    u2_ref[...] = (y2 * (1.0 + s2_ref[...]) + sh2_ref[...]).astype(u2_ref.dtype)


def _post_mix(x2, mix, norm_g, mod, rows_per_mod_row, tm=256):
    m = x2.shape[0]
    row = pl.BlockSpec((tm, D_MODEL), lambda i: (i, 0))
    return pl.pallas_call(
        _post_mix_kernel,
        grid=(m // tm,),
        in_specs=[row, row, _norm_spec(1), _norm_spec(2),
                  _mod_spec(2, rows_per_mod_row, tm), _mod_spec(3, rows_per_mod_row, tm),
                  _mod_spec(4, rows_per_mod_row, tm)],
        out_specs=[row, row],
        out_shape=[jax.ShapeDtypeStruct((m, D_MODEL), F32), jax.ShapeDtypeStruct((m, D_MODEL), BF16)],
        compiler_params=_params(1),
        name="post_mix",
    )(x2, mix, norm_g, norm_g, mod, mod, mod)


def _post_mlp_kernel(h1_ref, y_ref, gn_ref, g2_ref, o_ref):
    o_ref[...] = h1_ref[...] + g2_ref[...] * (_rms(y_ref[...]) * gn_ref[...])


def _post_mlp(h1, y, norm_g, mod, rows_per_mod_row, tm=256):
    m = h1.shape[0]
    row = pl.BlockSpec((tm, D_MODEL), lambda i: (i, 0))
    return pl.pallas_call(
        _post_mlp_kernel,
        grid=(m // tm,),
        in_specs=[row, row, _norm_spec(3), _mod_spec(5, rows_per_mod_row, tm)],
        out_specs=row,
        out_shape=jax.ShapeDtypeStruct((m, D_MODEL), F32),
        compiler_params=_params(1),
        name="post_mlp",
    )(h1, y, norm_g, mod)


def _matmul_kernel(a_ref, w_ref, *rest, act, has_bias):
    o_ref = rest[-1]
    acc = jnp.dot(a_ref[...], w_ref[...], preferred_element_type=F32)
    if has_bias:
        acc = acc + rest[0][...]
    if act == "relu2":
        acc = jnp.square(jnp.maximum(acc, 0.0))
    o_ref[...] = acc.astype(o_ref.dtype)


def _matmul(a, w, *, out_dtype, tm, tn, n_out=None, w_tile_of=None, act=None, bias=None, name="matmul"):
    m, k = a.shape
    n = w.shape[1] if n_out is None else n_out
    tile_of = (lambda j: j) if w_tile_of is None else w_tile_of
    in_specs = [pl.BlockSpec((tm, k), lambda j, i: (i, 0)),
                pl.BlockSpec((k, tn), lambda j, i: (0, tile_of(j)))]
    args = [a, w]
    if bias is not None:
        in_specs.append(pl.BlockSpec((1, tn), lambda j, i: (0, tile_of(j))))
        args.append(bias)
    return pl.pallas_call(
        functools.partial(_matmul_kernel, act=act, has_bias=bias is not None),
        grid=(n // tn, m // tm),
        in_specs=in_specs,
        out_specs=pl.BlockSpec((tm, tn), lambda j, i: (i, j)),
        out_shape=jax.ShapeDtypeStruct((m, n), out_dtype),
        compiler_params=_params(2),
        name=name,
    )(*args)


def _matmul_ksplit_kernel(a_ref, w_ref, o_ref):
    @pl.when(pl.program_id(2) == 0)
    def _():
        o_ref[...] = jnp.zeros_like(o_ref)

    o_ref[...] += jnp.dot(a_ref[...], w_ref[...], preferred_element_type=F32)


def _matmul_ksplit(a, w, *, tm, tn, tk, name):
    m, k = a.shape
    n = w.shape[1]
    return pl.pallas_call(
        _matmul_ksplit_kernel,
        grid=(n // tn, m // tm, k // tk),
        in_specs=[pl.BlockSpec((tm, tk), lambda j, i, kk: (i, kk)),
                  pl.BlockSpec((tk, tn), lambda j, i, kk: (kk, j))],
        out_specs=pl.BlockSpec((tm, tn), lambda j, i, kk: (i, j)),
        out_shape=jax.ShapeDtypeStruct((m, n), F32),
        compiler_params=_params(3),
        name=name,
    )(a, w)


def _branch_merge_kernel(hm_ref, at_ref, wm_ref, wa_ref, gm_ref, ga_ref, o_ref):
    ym = jnp.dot(hm_ref[...], wm_ref[...], preferred_element_type=F32)
    ya = jnp.dot(at_ref[...], wa_ref[...], preferred_element_type=F32)
    z = _sigmoid(gm_ref[...].astype(F32)) * ym + _sigmoid(ga_ref[...].astype(F32)) * ya
    o_ref[...] = z.astype(o_ref.dtype)


def _branch_merge(hm, at, w_out_m, w_out_a, proj, tm=512, tn=512):
    m, k = hm.shape
    n = w_out_m.shape[1]
    act = pl.BlockSpec((tm, k), lambda j, i: (i, 0))
    wgt = pl.BlockSpec((k, tn), lambda j, i: (0, j))
    return pl.pallas_call(
        _branch_merge_kernel,
        grid=(n // tn, m // tm),
        in_specs=[act, act, wgt, wgt,
                  pl.BlockSpec((tm, tn), lambda j, i: (i, OFF_BGM // tn + j)),
                  pl.BlockSpec((tm, tn), lambda j, i: (i, OFF_BGA // tn + j))],
        out_specs=pl.BlockSpec((tm, tn), lambda j, i: (i, j)),
        out_shape=jax.ShapeDtypeStruct((m, n), BF16),
        compiler_params=_params(2),
        name="branch_merge",
    )(hm, at, w_out_m, w_out_a, proj, proj)


def _mlstm_chunk(q_ref, k_ref, v_ref, gc_ref, gr_ref, ct_ref, n_ref, m_ref, h_ref, *, reverse):
    length = k_ref.shape[0]
    ji, jf = (2, 3) if reverse else (0, 1)
    gc = gc_ref[...]
    gr = gr_ref[...]
    ig_c = gc[:, ji:ji + 1]
    lf_c = _log_sigmoid(gc[:, jf:jf + 1])
    ig_r = gr[ji:ji + 1, :]
    lf_r = _log_sigmoid(gr[jf:jf + 1, :])
    t_idx = lax.broadcasted_iota(jnp.int32, (length, length), 0)
    s_idx = lax.broadcasted_iota(jnp.int32, (length, length), 1)
    lower = s_idx <= t_idx
    upper = s_idx >= t_idx
    seen, seen_t = (upper, lower) if reverse else (lower, upper)
    b_c = jnp.sum(jnp.where(seen, lf_r, 0.0), axis=1, keepdims=True)
    b_r = jnp.sum(jnp.where(seen_t, lf_c, 0.0), axis=0, keepdims=True)
    b_tot = jnp.sum(lf_r, axis=1, keepdims=True)
    m_prev = m_ref[...]
    k = k_ref[...]
    v = v_ref[...]

    if h_ref is not None:
        q = q_ref[...]
        a_c = b_c + m_prev
        dm = jnp.where(seen, b_c - b_r + ig_r, -jnp.inf)
        m_t = jnp.maximum(a_c, jnp.max(dm, axis=1, keepdims=True))
        qk = lax.dot_general(q, k, (((1,), (1,)), ((), ())), preferred_element_type=F32)
        s = qk * Q_SCALE * jnp.exp(dm - m_t)
        inter = jnp.exp(a_c - m_t) * Q_SCALE
        qc = jnp.dot(q, ct_ref[...].astype(BF16), preferred_element_type=F32)
        num = inter * qc + jnp.dot(s.astype(BF16), v, preferred_element_type=F32)
        qn = jnp.sum(q.astype(F32) * n_ref[...], axis=1, keepdims=True)
        den = inter * qn + jnp.sum(s, axis=1, keepdims=True)
        h_ref[...] = num * (1.0 / jnp.maximum(jnp.abs(den), jnp.exp(-m_t)))

    w_c = b_tot - b_c + ig_c
    m_new = jnp.maximum(b_tot + m_prev, jnp.max(w_c, axis=0, keepdims=True))
    decay = jnp.exp(b_tot + m_prev - m_new)
    kw = k.astype(F32) * jnp.exp(w_c - m_new)
    ct_ref[...] = decay * ct_ref[...] + lax.dot_general(
        kw.astype(BF16), v, (((0,), (0,)), ((), ())), preferred_element_type=F32)
    n_ref[...] = decay * n_ref[...] + jnp.sum(kw, axis=0, keepdims=True)
    m_ref[...] = m_new


def _mlstm_kernel(qf, kf, vf, gcf, grf, qb, kb, vb, gcb, grb, kc, vc, gcc, grc,
                  hf, hb, ctf, nf, mf, ctb, nb, mb):
    c = pl.program_id(2)

    @pl.when(c == 0)
    def _():
        for ct, n, m in ((ctf, nf, mf), (ctb, nb, mb)):
            ct[...] = jnp.zeros_like(ct)
            n[...] = jnp.zeros_like(n)
            m[...] = jnp.zeros_like(m)
        _mlstm_chunk(None, kc, vc, gcc, grc, ctf, nf, mf, None, reverse=False)
        _mlstm_chunk(None, kc, vc, gcc, grc, ctb, nb, mb, None, reverse=True)

    @pl.when(c > 0)
    def _():
        _mlstm_chunk(qf, kf, vf, gcf, grf, ctf, nf, mf, hf, reverse=False)
        _mlstm_chunk(qb, kb, vb, gcb, grb, ctb, nb, mb, hb, reverse=True)


def _mlstm(proj, proj_ctx, gc, gr, gc_ctx, gr_ctx, batch):
    length = M_CHUNK
    assert CTX_LEN == length
    seq = proj.shape[0] // batch
    nc = seq // length

    def fwd(c):
        return jnp.maximum(c - 1, 0)

    def bwd(c):
        return nc - jnp.maximum(c, 1)

    def lat_specs(chunk_of):
        return [
            pl.BlockSpec((length, M_DQK), lambda b, h, c: (b * nc + chunk_of(c), OFF_QM // M_DQK + h)),
            pl.BlockSpec((length, M_DQK), lambda b, h, c: (b * nc + chunk_of(c), OFF_KM // M_DQK + h)),
            pl.BlockSpec((length, M_DV), lambda b, h, c: (b * nc + chunk_of(c), OFF_VM // M_DV + h)),
            pl.BlockSpec((None, None, length, 4), lambda b, h, c: (b, h, chunk_of(c), 0)),
            pl.BlockSpec((None, None, 4, length), lambda b, h, c: (b, h, 0, chunk_of(c))),
        ]

    ctx_specs = [
        pl.BlockSpec((length, M_DQK), lambda b, h, c: (b, CTX_OFF_KM // M_DQK + h)),
        pl.BlockSpec((length, M_DV), lambda b, h, c: (b, CTX_OFF_VM // M_DV + h)),
        pl.BlockSpec((None, None, length, 4), lambda b, h, c: (b, h, 0, 0)),
        pl.BlockSpec((None, None, 4, length), lambda b, h, c: (b, h, 0, 0)),
    ]
    out_shape = jax.ShapeDtypeStruct((batch * seq, M_V_W), F32)
    state = [pltpu.VMEM((M_DQK, M_DV), F32), pltpu.VMEM((1, M_DQK), F32), pltpu.VMEM((1, 1), F32)]
    return pl.pallas_call(
        _mlstm_kernel,
        grid=(batch, M_HEADS, nc + 1),
        in_specs=lat_specs(fwd) + lat_specs(bwd) + ctx_specs,
        out_specs=[pl.BlockSpec((length, M_DV), lambda b, h, c: (b * nc + fwd(c), h)),
                   pl.BlockSpec((length, M_DV), lambda b, h, c: (b * nc + bwd(c), h))],
        out_shape=[out_shape, out_shape],
        scratch_shapes=state + state,
        compiler_params=_params(3),
        name="mlstm",
    )(proj, proj, proj, gc, gr, proj, proj, proj, gc, gr, proj_ctx, proj_ctx, gc_ctx, gr_ctx)


def _readout_kernel(hf_ref, hb_ref, om_ref, g_ref, o_ref):
    h = hf_ref[...] + hb_ref[...]
    y = _rms(h) * g_ref[...] * _sigmoid(om_ref[...].astype(F32))
    o_ref[...] = y.astype(o_ref.dtype)


def _readout(hf, hb, proj, m_norm_g, tm=512):
    m = hf.shape[0]
    blk = pl.BlockSpec((tm, M_DV), lambda i, h: (i, h))
    return pl.pallas_call(
        _readout_kernel,
        grid=(m // tm, M_HEADS),
        in_specs=[blk, blk,
                  pl.BlockSpec((tm, M_DV), lambda i, h: (i, OFF_OM // M_DV + h)),
                  pl.BlockSpec((1, M_DV), lambda i, h: (0, h))],
        out_specs=blk,
        out_shape=jax.ShapeDtypeStruct((m, M_V_W), BF16),
        compiler_params=_params(2),
        name="mlstm_readout",
    )(hf, hb, proj, m_norm_g.reshape(1, M_V_W))


def _rope(x, cos, sin_signed):
    lane = lax.broadcasted_iota(jnp.int32, x.shape, 1)
    partner = jnp.where((lane % 64) < 32, pltpu.roll(x, LANES - 32, 1), pltpu.roll(x, 32, 1))
    return x * cos + partner * sin_signed


def _attn_kernel(sink_ref, q_ref, k_ref, v_ref, kc_ref, vc_ref, cos_ref, sin_ref, o_ref, kpad, vpad):
    seq = k_ref.shape[0]
    kvh = pl.program_id(1)
    scale = A_HEAD_DIM ** -0.5
    zeros = jnp.zeros((A_BLOCK, A_HEAD_DIM), BF16)
    kpad[0:A_BLOCK, :] = zeros
    kpad[A_BLOCK + seq:, :] = zeros
    vpad[0:A_BLOCK, :] = zeros
    vpad[A_BLOCK + seq:, :] = zeros
    kpad[A_BLOCK:A_BLOCK + seq, :] = _rope(k_ref[...].astype(F32), cos_ref[...], sin_ref[...]).astype(BF16)
    vpad[A_BLOCK:A_BLOCK + seq, :] = v_ref[...]
    kc = kc_ref[...]
    vc = vc_ref[...]
    qi = lax.broadcasted_iota(jnp.int32, (A_BLOCK, 3 * A_BLOCK), 0)
    kj = lax.broadcasted_iota(jnp.int32, (A_BLOCK, 3 * A_BLOCK), 1)
    band = (kj >= qi) & (kj <= qi + 2 * WINDOW)
    nt = (((1,), (1,)), ((), ()))

    def body(n, carry):
        r0 = pl.multiple_of(n * A_BLOCK, A_BLOCK)
        cos = cos_ref[pl.ds(r0, A_BLOCK), :]
        sin = sin_ref[pl.ds(r0, A_BLOCK), :]
        kw = kpad[pl.ds(r0, 3 * A_BLOCK), :]
        vw = vpad[pl.ds(r0, 3 * A_BLOCK), :]
        kpos = r0 - A_BLOCK + kj
        valid = band & (kpos >= 0) & (kpos < seq)
        for g in range(A_GROUP):
            cols = slice(g * A_HEAD_DIM, (g + 1) * A_HEAD_DIM)
            qg = _rope(q_ref[pl.ds(r0, A_BLOCK), cols].astype(F32), cos, sin).astype(BF16)
            sw = lax.dot_general(qg, kw, nt, preferred_element_type=F32) * scale
            sw = jnp.where(valid, sw, -jnp.inf)
            sc = lax.dot_general(qg, kc, nt, preferred_element_type=F32) * scale
            sink = sink_ref[kvh * A_GROUP + g]
            mx = jnp.maximum(jnp.maximum(jnp.max(sw, axis=1, keepdims=True),
                                         jnp.max(sc, axis=1, keepdims=True)), sink)
            pw = jnp.exp(sw - mx)
            pc = jnp.exp(sc - mx)
            den = (jnp.sum(pw, axis=1, keepdims=True) + jnp.sum(pc, axis=1, keepdims=True)
                   + jnp.exp(sink - mx))
            o = (jnp.dot(pw.astype(BF16), vw, preferred_element_type=F32)
                 + jnp.dot(pc.astype(BF16), vc, preferred_element_type=F32))
            o_ref[pl.ds(r0, A_BLOCK), cols] = (o * (1.0 / den)).astype(o_ref.dtype)
        return carry

    lax.fori_loop(0, seq // A_BLOCK, body, 0)


def _attention(proj, proj_ctx, sink, cos, sin, batch):
    seq = proj.shape[0] // batch
    qw = A_GROUP * A_HEAD_DIM
    tab = pl.BlockSpec((seq, A_HEAD_DIM), lambda b, h: (0, 0))
    return pl.pallas_call(
        _attn_kernel,
        grid=(batch, A_KV_HEADS),
        in_specs=[pl.BlockSpec(memory_space=pltpu.SMEM),
                  pl.BlockSpec((seq, qw), lambda b, h: (b, OFF_QA // qw + h)),
                  pl.BlockSpec((seq, A_HEAD_DIM), lambda b, h: (b, OFF_KA // A_HEAD_DIM + h)),
                  pl.BlockSpec((seq, A_HEAD_DIM), lambda b, h: (b, OFF_VA // A_HEAD_DIM + h)),
                  pl.BlockSpec((CTX_LEN, A_HEAD_DIM), lambda b, h: (b, CTX_OFF_KA // A_HEAD_DIM + h)),
                  pl.BlockSpec((CTX_LEN, A_HEAD_DIM), lambda b, h: (b, CTX_OFF_VA // A_HEAD_DIM + h)),
                  tab, tab],
        out_specs=pl.BlockSpec((seq, qw), lambda b, h: (b, h)),
        out_shape=jax.ShapeDtypeStruct((batch * seq, A_Q_W), BF16),
        scratch_shapes=[pltpu.VMEM((seq + 2 * A_BLOCK, A_HEAD_DIM), BF16),
                        pltpu.VMEM((seq + 2 * A_BLOCK, A_HEAD_DIM), BF16)],
        compiler_params=_params(2),
        name="window_attention",
    )(sink, proj, proj, proj, proj_ctx, proj_ctx, cos, sin)


def _rope_tables(seq):
    t = jnp.arange(seq)
    nf = A_HEAD_DIM // 4
    freqs = ROPE_BASE ** (-jnp.arange(nf, dtype=F32) / nf)
    ang_r = (t // GRID_W).astype(F32)[:, None] * freqs
    ang_c = (t % GRID_W).astype(F32)[:, None] * freqs
    cos = jnp.concatenate([jnp.cos(ang_r), jnp.cos(ang_r), jnp.cos(ang_c), jnp.cos(ang_c)], axis=-1)
    sin = jnp.concatenate([-jnp.sin(ang_r), jnp.sin(ang_r), -jnp.sin(ang_c), jnp.sin(ang_c)], axis=-1)
    return cos, sin


def _gate_layouts(g, batch):
    g4 = g[:, :N_GATES].reshape(batch, -1, 4, M_HEADS)
    return g4.transpose(0, 3, 1, 2), g4.transpose(0, 3, 2, 1)


def kernel(x, c, ctx, c_ctx, w_mod, b_mod, norm_g, w_in, m_gate_b, m_norm_g, attn_sink,
           w_out_m, w_out_a, w_o, w_ff1, w_ff2):
    batch, seq, d = x.shape
    n_ctx = ctx.shape[1]
    x2 = x.reshape(batch * seq, d)
    ctx2 = ctx.reshape(batch * n_ctx, d)

    w_in0 = w_in[0]
    w_al = jnp.concatenate([w_in0[:, :W_IN_GATE_LO], w_in0[:, W_IN_GATE_HI:]], axis=1).astype(BF16)
    w_gate = jnp.pad(w_in0[:, W_IN_GATE_LO:W_IN_GATE_HI], ((0, 0), (0, GATE_PAD - N_GATES))).astype(BF16)
    gate_bias = jnp.pad(m_gate_b[0].reshape(1, N_GATES), ((0, 0), (0, GATE_PAD - N_GATES)))

    c_all = jnp.zeros((MOD_ROWS, d), F32).at[:batch].set(c).at[batch].set(c_ctx)
    mod = _modulation(c_all, w_mod[0], b_mod[0]).reshape(MOD_ROWS, 6, 1, d)
    gn = norm_g[0].reshape(4, 1, d)

    u = _prenorm(x2, gn, mod, seq)
    uc = _prenorm(ctx2, gn, mod, n_ctx, fixed_row=batch)

    tile = 1024
    proj = _matmul(u, w_al, out_dtype=BF16, tm=tile, tn=tile, name="in_proj")
    g_lat = _matmul(u, w_gate, out_dtype=F32, tm=tile, tn=GATE_PAD, bias=gate_bias, name="gate_proj")
    ctx_tiles = (OFF_VA + A_KV_W - OFF_KA) // tile
    lat_tiles = (OFF_OM - OFF_KM) // tile
    proj_ctx = _matmul(
        uc, w_al, out_dtype=BF16, tm=tile, tn=tile, n_out=CTX_PROJ_W,
        w_tile_of=lambda j: jnp.where(j < lat_tiles, j + OFF_KM // tile, j - lat_tiles + OFF_KA // tile),
        name="in_proj_ctx")
    assert lat_tiles + ctx_tiles == CTX_PROJ_W // tile
    g_ctx = _matmul(uc, w_gate, out_dtype=F32, tm=tile, tn=GATE_PAD, bias=gate_bias, name="gate_proj_ctx")

    gc, gr = _gate_layouts(g_lat, batch)
    gc_ctx, gr_ctx = _gate_layouts(g_ctx, batch)
    hf, hb = _mlstm(proj, proj_ctx, gc, gr, gc_ctx, gr_ctx, batch)
    hm = _readout(hf, hb, proj, m_norm_g[0])

    cos, sin = _rope_tables(seq)
    at = _attention(proj, proj_ctx, attn_sink[0], cos, sin, batch)

    z = _branch_merge(hm, at, w_out_m[0].astype(BF16), w_out_a[0].astype(BF16), proj)
    mix = _matmul(z, w_o[0].astype(BF16), out_dtype=F32, tm=tile, tn=tile, name="out_proj")
    h1, u2 = _post_mix(x2, mix, gn, mod, seq)

    hid = _matmul(u2, w_ff1[0].astype(BF16), out_dtype=BF16, tm=tile, tn=tile, act="relu2", name="ff1")
    y = _matmul_ksplit(hid, w_ff2[0].astype(BF16), tm=tile, tn=tile, tk=4096, name="ff2")
    out = _post_mlp(h1, y, gn, mod, seq)
    return out.reshape(batch, seq, d)
```

```python
import functools

import jax
import jax.numpy as jnp
from jax import lax
from jax.experimental import pallas as pl
from jax.experimental.pallas import tpu as pltpu

F32 = jnp.float32
BF16 = jnp.bfloat16

D_MODEL = 4096
GRID_W = 64
CTX_LEN = 256
M_HEADS = 8
M_DQK = 256
M_DV = 512
A_HEADS = 32
A_KV_HEADS = 8
A_HEAD_DIM = 128
A_GROUP = A_HEADS // A_KV_HEADS
WINDOW = 128
A_BLOCK = 128
ROPE_BASE = 10000.0
D_FF = 4 * D_MODEL
EPS = 1e-6

M_QK_W = M_HEADS * M_DQK
M_V_W = M_HEADS * M_DV
A_Q_W = A_HEADS * A_HEAD_DIM
A_KV_W = A_KV_HEADS * A_HEAD_DIM
N_GATES = 4 * M_HEADS

W_IN_GATE_LO = 2 * M_QK_W + 2 * M_V_W
W_IN_GATE_HI = W_IN_GATE_LO + N_GATES
OFF_QM = 0
OFF_KM = OFF_QM + M_QK_W
OFF_VM = OFF_KM + M_QK_W
OFF_OM = OFF_VM + M_V_W
OFF_QA = OFF_OM + M_V_W
OFF_KA = OFF_QA + A_Q_W
OFF_VA = OFF_KA + A_KV_W
OFF_BGM = OFF_VA + A_KV_W
OFF_BGA = OFF_BGM + D_MODEL
PROJ_ALIGNED_W = OFF_BGA + D_MODEL
CTX_OFF_KM = 0
CTX_OFF_VM = M_QK_W
CTX_OFF_KA = M_QK_W + M_V_W
CTX_OFF_VA = CTX_OFF_KA + A_KV_W
CTX_PROJ_W = CTX_OFF_VA + A_KV_W

M_CHUNK = 256
Q_SCALE = M_DQK ** -0.5
LANES = 128
GATE_PAD = LANES
MOD_ROWS = 8
VMEM_LIMIT = 56 * 1024 * 1024
CAST_ROWS = 512


def _params(n_axes):
    return pltpu.CompilerParams(dimension_semantics=("arbitrary",) * n_axes, vmem_limit_bytes=VMEM_LIMIT)


def _sigmoid(x):
    return 1.0 / (1.0 + jnp.exp(-x))


def _log_sigmoid(x):
    return jnp.minimum(x, 0.0) - jnp.log1p(jnp.exp(-jnp.abs(x)))


def _rms(x):
    return x * lax.rsqrt(jnp.mean(x * x, axis=-1, keepdims=True) + EPS)


def _mod_kernel(c_ref, w_ref, b_ref, o_ref):
    c = c_ref[...]
    sc = (c * _sigmoid(c)).astype(BF16)
    o_ref[...] = jnp.dot(sc, w_ref[...].astype(BF16), preferred_element_type=F32) + b_ref[...]


def _modulation(c_all, w_mod, b_mod, tn=1024):
    d, n = w_mod.shape
    return pl.pallas_call(
        _mod_kernel,
        grid=(n // tn,),
        in_specs=[pl.BlockSpec((MOD_ROWS, d), lambda j: (0, 0)),
                  pl.BlockSpec((d, tn), lambda j: (0, j)),
                  pl.BlockSpec((1, tn), lambda j: (0, j))],
        out_specs=pl.BlockSpec((MOD_ROWS, tn), lambda j: (0, j)),
        out_shape=jax.ShapeDtypeStruct((MOD_ROWS, n), F32),
        compiler_params=_params(1),
        name="modulation",
    )(c_all, w_mod, b_mod.reshape(1, n))


def _prenorm_kernel(x_ref, g_ref, sh_ref, sc_ref, o_ref):
    y = _rms(x_ref[...]) * g_ref[...]
    o_ref[...] = (y * (1.0 + sc_ref[...]) + sh_ref[...]).astype(o_ref.dtype)


def _mod_spec(slot, rows_per_mod_row, tm, fixed_row=None):
    if fixed_row is None:
        return pl.BlockSpec((None, None, 1, D_MODEL), lambda i: ((i * tm) // rows_per_mod_row, slot, 0, 0))
    return pl.BlockSpec((None, None, 1, D_MODEL), lambda i: (fixed_row, slot, 0, 0))


def _norm_spec(slot):
    return pl.BlockSpec((None, 1, D_MODEL), lambda i: (slot, 0, 0))


def _prenorm(x2, norm_g, mod, rows_per_mod_row, fixed_row=None, tm=256):
    m = x2.shape[0]
    return pl.pallas_call(
        _prenorm_kernel,
        grid=(m // tm,),
        in_specs=[pl.BlockSpec((tm, D_MODEL), lambda i: (i, 0)),
                  _norm_spec(0),
                  _mod_spec(0, rows_per_mod_row, tm, fixed_row),
                  _mod_spec(1, rows_per_mod_row, tm, fixed_row)],
        out_specs=pl.BlockSpec((tm, D_MODEL), lambda i: (i, 0)),
        out_shape=jax.ShapeDtypeStruct((m, D_MODEL), BF16),
        compiler_params=_params(1),
        name="prenorm",
    )(x2, norm_g, mod, mod)


def _post_mix_kernel(x_ref, mix_ref, gn1_ref, gn2_ref, g1_ref, sh2_ref, s2_ref, h1_ref, u2_ref):
    h1 = x_ref[...] + g1_ref[...] * (_rms(mix_ref[...]) * gn1_ref[...])
    h1_ref[...] = h1
    y2 = _rms(h1) * gn2_ref[...]
    u2_ref[...] = (y2 * (1.0 + s2_ref[...]) + sh2_ref[...]).astype(u2_ref.dtype)


def _post_mix(x2, mix, norm_g, mod, rows_per_mod_row, tm=256):
    m = x2.shape[0]
    row = pl.BlockSpec((tm, D_MODEL), lambda i: (i, 0))
    return pl.pallas_call(
        _post_mix_kernel,
        grid=(m // tm,),
        in_specs=[row, row, _norm_spec(1), _norm_spec(2),
                  _mod_spec(2, rows_per_mod_row, tm), _mod_spec(3, rows_per_mod_row, tm),
                  _mod_spec(4, rows_per_mod_row, tm)],
        out_specs=[row, row],
        out_shape=[jax.ShapeDtypeStruct((m, D_MODEL), F32), jax.ShapeDtypeStruct((m, D_MODEL), BF16)],
        compiler_params=_params(1),
        name="post_mix",
    )(x2, mix, norm_g, norm_g, mod, mod, mod)


def _post_mlp_kernel(h1_ref, y_ref, gn_ref, g2_ref, o_ref):
    o_ref[...] = h1_ref[...] + g2_ref[...] * (_rms(y_ref[...]) * gn_ref[...])


def _post_mlp(h1, y, norm_g, mod, rows_per_mod_row, tm=256):
    m = h1.shape[0]
    row = pl.BlockSpec((tm, D_MODEL), lambda i: (i, 0))
    return pl.pallas_call(
        _post_mlp_kernel,
        grid=(m // tm,),
        in_specs=[row, row, _norm_spec(3), _mod_spec(5, rows_per_mod_row, tm)],
        out_specs=row,
        out_shape=jax.ShapeDtypeStruct((m, D_MODEL), F32),
        compiler_params=_params(1),
        name="post_mlp",
    )(h1, y, norm_g, mod)


def _cast_rows(dst_ref, load_rows, n_rows):
    def step(r, carry):
        rows = pl.ds(pl.multiple_of(r * CAST_ROWS, CAST_ROWS), CAST_ROWS)
        dst_ref[rows, :] = load_rows(rows).astype(dst_ref.dtype)
        return carry

    lax.fori_loop(0, n_rows // CAST_ROWS, step, 0)


def _resident_dot(a_ref, w_ref, w_scr):
    @pl.when(pl.program_id(1) == 0)
    def _():
        _cast_rows(w_scr, lambda rows: w_ref[rows, :], w_ref.shape[0])

    return jnp.dot(a_ref[...], w_scr[...], preferred_element_type=F32)


def _in_proj_kernel(a_ref, w_ref, wn_ref, o_ref, w_scr, *, tile_of):
    tn = w_scr.shape[1]

    @pl.when(pl.program_id(1) == 0)
    def _():
        shifted = tile_of(pl.program_id(0)) * tn >= W_IN_GATE_LO

        @pl.when(jnp.logical_not(shifted))
        def _():
            _cast_rows(w_scr, lambda rows: w_ref[rows, :], w_ref.shape[0])

        @pl.when(shifted)
        def _():
            def load(rows):
                both = jnp.concatenate([w_ref[rows, :], wn_ref[rows, :]], axis=1)
                return both[:, N_GATES:N_GATES + tn]

            _cast_rows(w_scr, load, w_ref.shape[0])

    o_ref[...] = jnp.dot(a_ref[...], w_scr[...], preferred_element_type=F32).astype(o_ref.dtype)


def _in_proj(a, w_in3, *, n_out, tile_of=None, tm=1024, tn=512, name="in_proj"):
    m, k = a.shape
    tile_of = (lambda j: j) if tile_of is None else tile_of
    per = tn // LANES
    return pl.pallas_call(
        functools.partial(_in_proj_kernel, tile_of=tile_of),
        grid=(n_out // tn, m // tm),
        in_specs=[pl.BlockSpec((tm, k), lambda j, i: (i, 0)),
                  pl.BlockSpec((None, k, tn), lambda j, i: (0, 0, tile_of(j))),
                  pl.BlockSpec((None, k, LANES), lambda j, i: (0, 0, (tile_of(j) + 1) * per))],
        out_specs=pl.BlockSpec((tm, tn), lambda j, i: (i, j)),
        out_shape=jax.ShapeDtypeStruct((m, n_out), BF16),
        scratch_shapes=[pltpu.VMEM((k, tn), BF16)],
        compiler_params=_params(2),
        name=name,
    )(a, w_in3, w_in3)


def _gate_proj_kernel(a_ref, w_ref, b_ref, o_ref):
    o_ref[...] = jnp.dot(a_ref[...], w_ref[...], preferred_element_type=F32) + b_ref[...]


def _gate_proj(a, w_gate, bias, tm=1024, name="gate_proj"):
    m, k = a.shape
    n = w_gate.shape[1]
    return pl.pallas_call(
        _gate_proj_kernel,
        grid=(m // tm,),
        in_specs=[pl.BlockSpec((tm, k), lambda i: (i, 0)),
                  pl.BlockSpec((k, n), lambda i: (0, 0)),
                  pl.BlockSpec((1, n), lambda i: (0, 0))],
        out_specs=pl.BlockSpec((tm, n), lambda i: (i, 0)),
        out_shape=jax.ShapeDtypeStruct((m, n), F32),
        compiler_params=_params(1),
        name=name,
    )(a, w_gate, bias)


def _gated_kernel(a_ref, w_ref, g_ref, o_ref, w_scr):
    acc = _resident_dot(a_ref, w_ref, w_scr)
    o_ref[...] = (_sigmoid(g_ref[...].astype(F32)) * acc).astype(o_ref.dtype)


def _gated_add_kernel(a_ref, w_ref, g_ref, prev_ref, o_ref, w_scr):
    acc = _resident_dot(a_ref, w_ref, w_scr)
    o_ref[...] = (prev_ref[...].astype(F32) + _sigmoid(g_ref[...].astype(F32)) * acc).astype(o_ref.dtype)


def _plain_kernel(a_ref, w_ref, o_ref, w_scr):
    o_ref[...] = _resident_dot(a_ref, w_ref, w_scr).astype(o_ref.dtype)


def _ff1_kernel(a_ref, w_ref, side_ref, o_ref, side_out_ref, w_scr):
    acc = _resident_dot(a_ref, w_ref, w_scr)
    o_ref[...] = jnp.square(jnp.maximum(acc, 0.0)).astype(o_ref.dtype)
    side_out_ref[...] = side_ref[...].astype(side_out_ref.dtype)


def _resident_matmul(kernel_fn, a, w3, extra, *, out_dtype, gate_off=None, tm=1024, tn=512, name):
    m, k = a.shape
    n = w3.shape[2]
    in_specs = [pl.BlockSpec((tm, k), lambda j, i: (i, 0)),
                pl.BlockSpec((None, k, tn), lambda j, i: (0, 0, j))]
    for idx, arr in enumerate(extra):
        off = gate_off // tn if (idx == 0 and gate_off is not None) else 0
        in_specs.append(pl.BlockSpec((tm, tn), lambda j, i, off=off: (i, off + j)))
    return pl.pallas_call(
        kernel_fn,
        grid=(n // tn, m // tm),
        in_specs=in_specs,
        out_specs=pl.BlockSpec((tm, tn), lambda j, i: (i, j)),
        out_shape=jax.ShapeDtypeStruct((m, n), out_dtype),
        scratch_shapes=[pltpu.VMEM((k, tn), BF16)],
        compiler_params=_params(2),
        name=name,
    )(a, w3, *extra)


def _ff1(a, w3, w_next3, tm=1024, tn=512):
    m, k = a.shape
    n = w3.shape[2]
    rows, cols = w_next3.shape[1:]
    n_i = m // tm
    slab = rows // ((n // tn) * n_i)
    side = pl.BlockSpec((None, slab, cols), lambda j, i: (0, j * n_i + i, 0))
    return pl.pallas_call(
        _ff1_kernel,
        grid=(n // tn, n_i),
        in_specs=[pl.BlockSpec((tm, k), lambda j, i: (i, 0)),
                  pl.BlockSpec((None, k, tn), lambda j, i: (0, 0, j)),
                  side],
        out_specs=[pl.BlockSpec((tm, tn), lambda j, i: (i, j)),
                   pl.BlockSpec((slab, cols), lambda j, i: (j * n_i + i, 0))],
        out_shape=[jax.ShapeDtypeStruct((m, n), BF16), jax.ShapeDtypeStruct((rows, cols), BF16)],
        scratch_shapes=[pltpu.VMEM((k, tn), BF16)],
        compiler_params=_params(2),
        name="ff1",
    )(a, w3, w_next3)


def _matmul_ksplit_kernel(a_ref, w_ref, o_ref):
    @pl.when(pl.program_id(2) == 0)
    def _():
        o_ref[...] = jnp.zeros_like(o_ref)

    o_ref[...] += jnp.dot(a_ref[...], w_ref[...], preferred_element_type=F32)


def _matmul_ksplit(a, w, *, tm, tn, tk, name):
    m, k = a.shape
    n = w.shape[1]
    return pl.pallas_call(
        _matmul_ksplit_kernel,
        grid=(n // tn, m // tm, k // tk),
        in_specs=[pl.BlockSpec((tm, tk), lambda j, i, kk: (i, kk)),
                  pl.BlockSpec((tk, tn), lambda j, i, kk: (kk, j))],
        out_specs=pl.BlockSpec((tm, tn), lambda j, i, kk: (i, j)),
        out_shape=jax.ShapeDtypeStruct((m, n), F32),
        compiler_params=_params(3),
        name=name,
    )(a, w)


def _mlstm_chunk(q_ref, k_ref, v_ref, gc_ref, gr_ref, ct_ref, n_ref, m_ref, h_ref, *, reverse):
    length = k_ref.shape[0]
    ji, jf = (2, 3) if reverse else (0, 1)
    gc = gc_ref[...]
    gr = gr_ref[...]
    ig_c = gc[:, ji:ji + 1]
    lf_c = _log_sigmoid(gc[:, jf:jf + 1])
    ig_r = gr[ji:ji + 1, :]
    lf_r = _log_sigmoid(gr[jf:jf + 1, :])
    t_idx = lax.broadcasted_iota(jnp.int32, (length, length), 0)
    s_idx = lax.broadcasted_iota(jnp.int32, (length, length), 1)
    lower = s_idx <= t_idx
    upper = s_idx >= t_idx
    seen, seen_t = (upper, lower) if reverse else (lower, upper)
    b_c = jnp.sum(jnp.where(seen, lf_r, 0.0), axis=1, keepdims=True)
    b_r = jnp.sum(jnp.where(seen_t, lf_c, 0.0), axis=0, keepdims=True)
    b_tot = jnp.sum(lf_r, axis=1, keepdims=True)
    m_prev = m_ref[...]
    k = k_ref[...]
    v = v_ref[...]

    if h_ref is not None:
        q = q_ref[...]
        a_c = b_c + m_prev
        dm = jnp.where(seen, b_c - b_r + ig_r, -jnp.inf)
        m_t = jnp.maximum(a_c, jnp.max(dm, axis=1, keepdims=True))
        qk = lax.dot_general(q, k, (((1,), (1,)), ((), ())), preferred_element_type=F32)
        s = qk * Q_SCALE * jnp.exp(dm - m_t)
        inter = jnp.exp(a_c - m_t) * Q_SCALE
        qc = jnp.dot(q, ct_ref[...].astype(BF16), preferred_element_type=F32)
        num = inter * qc + jnp.dot(s.astype(BF16), v, preferred_element_type=F32)
        qn = jnp.sum(q.astype(F32) * n_ref[...], axis=1, keepdims=True)
        den = inter * qn + jnp.sum(s, axis=1, keepdims=True)
        h_ref[...] = (num * (1.0 / jnp.maximum(jnp.abs(den), jnp.exp(-m_t)))).astype(h_ref.dtype)

    w_c = b_tot - b_c + ig_c
    m_new = jnp.maximum(b_tot + m_prev, jnp.max(w_c, axis=0, keepdims=True))
    decay = jnp.exp(b_tot + m_prev - m_new)
    kw = k.astype(F32) * jnp.exp(w_c - m_new)
    ct_ref[...] = decay * ct_ref[...] + lax.dot_general(
        kw.astype(BF16), v, (((0,), (0,)), ((), ())), preferred_element_type=F32)
    n_ref[...] = decay * n_ref[...] + jnp.sum(kw, axis=0, keepdims=True)
    m_ref[...] = m_new


def _mlstm_kernel(qf, kf, vf, gcf, grf, qb, kb, vb, gcb, grb, kc, vc, gcc, grc,
                  hf, hb, ctf, nf, mf, ctb, nb, mb):
    c = pl.program_id(2)

    @pl.when(c == 0)
    def _():
        for ct, n, m in ((ctf, nf, mf), (ctb, nb, mb)):
            ct[...] = jnp.zeros_like(ct)
            n[...] = jnp.zeros_like(n)
            m[...] = jnp.zeros_like(m)
        _mlstm_chunk(None, kc, vc, gcc, grc, ctf, nf, mf, None, reverse=False)
        _mlstm_chunk(None, kc, vc, gcc, grc, ctb, nb, mb, None, reverse=True)

    @pl.when(c > 0)
    def _():
        _mlstm_chunk(qf, kf, vf, gcf, grf, ctf, nf, mf, hf, reverse=False)
        _mlstm_chunk(qb, kb, vb, gcb, grb, ctb, nb, mb, hb, reverse=True)


def _mlstm(proj, proj_ctx, gc, gr, gc_ctx, gr_ctx, batch):
    length = M_CHUNK
    assert CTX_LEN == length
    seq = proj.shape[0] // batch
    nc = seq // length

    def fwd(c):
        return jnp.maximum(c - 1, 0)

    def bwd(c):
        return nc - jnp.maximum(c, 1)

    def lat_specs(chunk_of):
        return [
            pl.BlockSpec((length, M_DQK), lambda b, h, c: (b * nc + chunk_of(c), OFF_QM // M_DQK + h)),
            pl.BlockSpec((length, M_DQK), lambda b, h, c: (b * nc + chunk_of(c), OFF_KM // M_DQK + h)),
            pl.BlockSpec((length, M_DV), lambda b, h, c: (b * nc + chunk_of(c), OFF_VM // M_DV + h)),
            pl.BlockSpec((None, None, length, 4), lambda b, h, c: (b, h, chunk_of(c), 0)),
            pl.BlockSpec((None, None, 4, length), lambda b, h, c: (b, h, 0, chunk_of(c))),
        ]

    ctx_specs = [
        pl.BlockSpec((length, M_DQK), lambda b, h, c: (b, CTX_OFF_KM // M_DQK + h)),
        pl.BlockSpec((length, M_DV), lambda b, h, c: (b, CTX_OFF_VM // M_DV + h)),
        pl.BlockSpec((None, None, length, 4), lambda b, h, c: (b, h, 0, 0)),
        pl.BlockSpec((None, None, 4, length), lambda b, h, c: (b, h, 0, 0)),
    ]
    out_shape = jax.ShapeDtypeStruct((batch * seq, M_V_W), BF16)
    state = [pltpu.VMEM((M_DQK, M_DV), F32), pltpu.VMEM((1, M_DQK), F32), pltpu.VMEM((1, 1), F32)]
    return pl.pallas_call(
        _mlstm_kernel,
        grid=(batch, M_HEADS, nc + 1),
        in_specs=lat_specs(fwd) + lat_specs(bwd) + ctx_specs,
        out_specs=[pl.BlockSpec((length, M_DV), lambda b, h, c: (b * nc + fwd(c), h)),
                   pl.BlockSpec((length, M_DV), lambda b, h, c: (b * nc + bwd(c), h))],
        out_shape=[out_shape, out_shape],
        scratch_shapes=state + state,
        compiler_params=_params(3),
        name="mlstm",
    )(proj, proj, proj, gc, gr, proj, proj, proj, gc, gr, proj_ctx, proj_ctx, gc_ctx, gr_ctx)


def _readout_kernel(hf_ref, hb_ref, om_ref, g_ref, o_ref):
    h = hf_ref[...].astype(F32) + hb_ref[...].astype(F32)
    y = _rms(h) * g_ref[...] * _sigmoid(om_ref[...].astype(F32))
    o_ref[...] = y.astype(o_ref.dtype)


def _readout(hf, hb, proj, m_norm_g, tm=512):
    m = hf.shape[0]
    blk = pl.BlockSpec((tm, M_DV), lambda i, h: (i, h))
    return pl.pallas_call(
        _readout_kernel,
        grid=(m // tm, M_HEADS),
        in_specs=[blk, blk,
                  pl.BlockSpec((tm, M_DV), lambda i, h: (i, OFF_OM // M_DV + h)),
                  pl.BlockSpec((1, M_DV), lambda i, h: (0, h))],
        out_specs=blk,
        out_shape=jax.ShapeDtypeStruct((m, M_V_W), BF16),
        compiler_params=_params(2),
        name="mlstm_readout",
    )(hf, hb, proj, m_norm_g.reshape(1, M_V_W))


def _rope(x, cos, sin_signed):
    lane = lax.broadcasted_iota(jnp.int32, x.shape, 1)
    partner = jnp.where((lane % 64) < 32, pltpu.roll(x, LANES - 32, 1), pltpu.roll(x, 32, 1))
    return x * cos + partner * sin_signed


def _attn_kernel(sink_ref, q_ref, k_ref, v_ref, kc_ref, vc_ref, cos_ref, sin_ref, o_ref,
                 kpad, vpad, s_buf, p_buf, den_buf):
    seq = k_ref.shape[0]
    nb = seq // A_BLOCK
    kvh = pl.program_id(1)
    scale = A_HEAD_DIM ** -0.5
    rows = A_GROUP * A_BLOCK
    n_win = 3 * A_BLOCK
    zeros = jnp.zeros((A_BLOCK, A_HEAD_DIM), BF16)
    kpad[0:A_BLOCK, :] = zeros
    kpad[A_BLOCK + seq:, :] = zeros
    vpad[0:A_BLOCK, :] = zeros
    vpad[A_BLOCK + seq:, :] = zeros
    kpad[A_BLOCK:A_BLOCK + seq, :] = _rope(k_ref[...].astype(F32), cos_ref[...], sin_ref[...]).astype(BF16)
    vpad[A_BLOCK:A_BLOCK + seq, :] = v_ref[...]
    qi = lax.broadcasted_iota(jnp.int32, (rows, n_win), 0) & (A_BLOCK - 1)
    kj = lax.broadcasted_iota(jnp.int32, (rows, n_win), 1)
    band = (kj >= qi) & (kj <= qi + 2 * WINDOW)
    kj_row = lax.broadcasted_iota(jnp.int32, (1, n_win), 1)
    sink = jnp.concatenate(
        [jnp.full((A_BLOCK, 1), sink_ref[kvh * A_GROUP + g], F32) for g in range(A_GROUP)], axis=0)
    nt = (((1,), (1,)), ((), ()))

    def scores(n):
        r0 = pl.multiple_of(n * A_BLOCK, A_BLOCK)
        cos = cos_ref[pl.ds(r0, A_BLOCK), :]
        sin = sin_ref[pl.ds(r0, A_BLOCK), :]
        q = jnp.concatenate(
            [_rope(q_ref[pl.ds(r0, A_BLOCK), g * A_HEAD_DIM:(g + 1) * A_HEAD_DIM].astype(F32), cos, sin)
             for g in range(A_GROUP)], axis=0).astype(BF16)
        kpos = r0 - A_BLOCK + kj_row
        valid = band & ((kpos >= 0) & (kpos < seq))
        sw = lax.dot_general(q, kpad[pl.ds(r0, n_win), :], nt, preferred_element_type=F32) * scale
        s_buf[:, 0:n_win] = jnp.where(valid, sw, -jnp.inf)
        s_buf[:, n_win:] = lax.dot_general(q, kc_ref[...], nt, preferred_element_type=F32) * scale

    def softmax():
        s = s_buf[...]
        mx = jnp.maximum(jnp.max(s, axis=1, keepdims=True), sink)
        p = jnp.exp(s - mx)
        den_buf[...] = jnp.sum(p, axis=1, keepdims=True) + jnp.exp(sink - mx)
        p_buf[...] = p.astype(p_buf.dtype)

    def output(n):
        r0 = pl.multiple_of(n * A_BLOCK, A_BLOCK)
        o = (jnp.dot(p_buf[:, 0:n_win], vpad[pl.ds(r0, n_win), :], preferred_element_type=F32)
             + jnp.dot(p_buf[:, n_win:], vc_ref[...], preferred_element_type=F32)) * (1.0 / den_buf[...])
        for g in range(A_GROUP):
            o_ref[pl.ds(r0, A_BLOCK), g * A_HEAD_DIM:(g + 1) * A_HEAD_DIM] = (
                o[g * A_BLOCK:(g + 1) * A_BLOCK, :].astype(o_ref.dtype))

    scores(0)
    softmax()
    scores(1)

    def body(t, carry):
        output(t - 1)
        softmax()
        scores(t + 1)
        return carry

    lax.fori_loop(1, nb - 1, body, 0)
    output(nb - 2)
    softmax()
    output(nb - 1)


def _attention(proj, proj_ctx, sink, cos, sin, batch):
    seq = proj.shape[0] // batch
    qw = A_GROUP * A_HEAD_DIM
    tab = pl.BlockSpec((seq, A_HEAD_DIM), lambda b, h: (0, 0))
    return pl.pallas_call(
        _attn_kernel,
        grid=(batch, A_KV_HEADS),
        in_specs=[pl.BlockSpec(memory_space=pltpu.SMEM),
                  pl.BlockSpec((seq, qw), lambda b, h: (b, OFF_QA // qw + h)),
                  pl.BlockSpec((seq, A_HEAD_DIM), lambda b, h: (b, OFF_KA // A_HEAD_DIM + h)),
                  pl.BlockSpec((seq, A_HEAD_DIM), lambda b, h: (b, OFF_VA // A_HEAD_DIM + h)),
                  pl.BlockSpec((CTX_LEN, A_HEAD_DIM), lambda b, h: (b, CTX_OFF_KA // A_HEAD_DIM + h)),
                  pl.BlockSpec((CTX_LEN, A_HEAD_DIM), lambda b, h: (b, CTX_OFF_VA // A_HEAD_DIM + h)),
                  tab, tab],
        out_specs=pl.BlockSpec((seq, qw), lambda b, h: (b, h)),
        out_shape=jax.ShapeDtypeStruct((batch * seq, A_Q_W), BF16),
        scratch_shapes=[pltpu.VMEM((seq + 2 * A_BLOCK, A_HEAD_DIM), BF16),
                        pltpu.VMEM((seq + 2 * A_BLOCK, A_HEAD_DIM), BF16),
                        pltpu.VMEM((A_GROUP * A_BLOCK, 3 * A_BLOCK + CTX_LEN), F32),
                        pltpu.VMEM((A_GROUP * A_BLOCK, 3 * A_BLOCK + CTX_LEN), BF16),
                        pltpu.VMEM((A_GROUP * A_BLOCK, 1), F32)],
        compiler_params=_params(2),
        name="window_attention",
    )(sink, proj, proj, proj, proj_ctx, proj_ctx, cos, sin)


def _rope_tables(seq):
    t = jnp.arange(seq)
    nf = A_HEAD_DIM // 4
    freqs = ROPE_BASE ** (-jnp.arange(nf, dtype=F32) / nf)
    ang_r = (t // GRID_W).astype(F32)[:, None] * freqs
    ang_c = (t % GRID_W).astype(F32)[:, None] * freqs
    cos = jnp.concatenate([jnp.cos(ang_r), jnp.cos(ang_r), jnp.cos(ang_c), jnp.cos(ang_c)], axis=-1)
    sin = jnp.concatenate([-jnp.sin(ang_r), jnp.sin(ang_r), -jnp.sin(ang_c), jnp.sin(ang_c)], axis=-1)
    return cos, sin


def _gate_layouts(g, batch):
    g4 = g[:, :N_GATES].reshape(batch, -1, 4, M_HEADS)
    return g4.transpose(0, 3, 1, 2), g4.transpose(0, 3, 2, 1)


def kernel(x, c, ctx, c_ctx, w_mod, b_mod, norm_g, w_in, m_gate_b, m_norm_g, attn_sink,
           w_out_m, w_out_a, w_o, w_ff1, w_ff2):
    batch, seq, d = x.shape
    n_ctx = ctx.shape[1]
    x2 = x.reshape(batch * seq, d)
    ctx2 = ctx.reshape(batch * n_ctx, d)

    w_gate = jnp.pad(w_in[0][:, W_IN_GATE_LO:W_IN_GATE_HI], ((0, 0), (0, GATE_PAD - N_GATES))).astype(BF16)
    gate_bias = jnp.pad(m_gate_b[0].reshape(1, N_GATES), ((0, 0), (0, GATE_PAD - N_GATES)))

    c_all = jnp.zeros((MOD_ROWS, d), F32).at[:batch].set(c).at[batch].set(c_ctx)
    mod = _modulation(c_all, w_mod[0], b_mod[0]).reshape(MOD_ROWS, 6, 1, d)
    gn = norm_g[0].reshape(4, 1, d)

    u = _prenorm(x2, gn, mod, seq)
    uc = _prenorm(ctx2, gn, mod, n_ctx, fixed_row=batch)

    tn = 512
    proj = _in_proj(u, w_in, n_out=PROJ_ALIGNED_W, tn=tn)
    g_lat = _gate_proj(u, w_gate, gate_bias)
    lat_tiles = (OFF_OM - OFF_KM) // tn
    proj_ctx = _in_proj(
        uc, w_in, n_out=CTX_PROJ_W, tn=tn, name="in_proj_ctx",
        tile_of=lambda j: jnp.where(j < lat_tiles, j + OFF_KM // tn, j - lat_tiles + OFF_KA // tn))
    g_ctx = _gate_proj(uc, w_gate, gate_bias, name="gate_proj_ctx")

    gc, gr = _gate_layouts(g_lat, batch)
    gc_ctx, gr_ctx = _gate_layouts(g_ctx, batch)
    hf, hb = _mlstm(proj, proj_ctx, gc, gr, gc_ctx, gr_ctx, batch)
    hm = _readout(hf, hb, proj, m_norm_g[0])

    cos, sin = _rope_tables(seq)
    at = _attention(proj, proj_ctx, attn_sink[0], cos, sin, batch)

    zm = _resident_matmul(_gated_kernel, hm, w_out_m, [proj], out_dtype=BF16, gate_off=OFF_BGM, name="out_m")
    z = _resident_matmul(_gated_add_kernel, at, w_out_a, [proj, zm], out_dtype=BF16, gate_off=OFF_BGA,
                         name="out_a_merge")
    mix = _resident_matmul(_plain_kernel, z, w_o, [], out_dtype=F32, name="out_proj")
    h1, u2 = _post_mix(x2, mix, gn, mod, seq)

    hid, w_ff2_bf16 = _ff1(u2, w_ff1, w_ff2)
    y = _matmul_ksplit(hid, w_ff2_bf16, tm=1024, tn=1024, tk=4096, name="ff2")
    out = _post_mlp(h1, y, gn, mod, seq)
    return out.reshape(batch, seq, d)
```

```python
import functools

import jax
import jax.numpy as jnp
from jax import lax
from jax.experimental import pallas as pl
from jax.experimental.pallas import tpu as pltpu

F32 = jnp.float32
BF16 = jnp.bfloat16

D_MODEL = 4096
GRID_W = 64
CTX_LEN = 256
M_HEADS = 8
M_DQK = 256
M_DV = 512
A_HEADS = 32
A_KV_HEADS = 8
A_HEAD_DIM = 128
A_GROUP = A_HEADS // A_KV_HEADS
WINDOW = 128
A_BLOCK = 128
ROPE_BASE = 10000.0
D_FF = 4 * D_MODEL
EPS = 1e-6

M_QK_W = M_HEADS * M_DQK
M_V_W = M_HEADS * M_DV
A_Q_W = A_HEADS * A_HEAD_DIM
A_KV_W = A_KV_HEADS * A_HEAD_DIM
N_GATES = 4 * M_HEADS

W_IN_GATE_LO = 2 * M_QK_W + 2 * M_V_W
W_IN_GATE_HI = W_IN_GATE_LO + N_GATES
OFF_QM = 0
OFF_KM = OFF_QM + M_QK_W
OFF_VM = OFF_KM + M_QK_W
OFF_OM = OFF_VM + M_V_W
OFF_QA = OFF_OM + M_V_W
OFF_KA = OFF_QA + A_Q_W
OFF_VA = OFF_KA + A_KV_W
OFF_BGM = OFF_VA + A_KV_W
OFF_BGA = OFF_BGM + D_MODEL
PROJ_ALIGNED_W = OFF_BGA + D_MODEL
CTX_OFF_KM = 0
CTX_OFF_VM = M_QK_W
CTX_OFF_KA = M_QK_W + M_V_W
CTX_OFF_VA = CTX_OFF_KA + A_KV_W
CTX_PROJ_W = CTX_OFF_VA + A_KV_W

M_CHUNK = 256
Q_SCALE = M_DQK ** -0.5
LANES = 128
GATE_PAD = LANES
MOD_ROWS = 8
VMEM_LIMIT = 60000 * 1024


def _params(n_axes):
    return pltpu.CompilerParams(dimension_semantics=("arbitrary",) * n_axes, vmem_limit_bytes=VMEM_LIMIT)


def _sigmoid(x):
    return 1.0 / (1.0 + jnp.exp(-x))


def _log_sigmoid(x):
    return jnp.minimum(x, 0.0) - jnp.log1p(jnp.exp(-jnp.abs(x)))


def _rms(x):
    return x * lax.rsqrt(jnp.mean(x * x, axis=-1, keepdims=True) + EPS)


def _mod_kernel(c_ref, w_ref, b_ref, o_ref):
    c = c_ref[...]
    sc = (c * _sigmoid(c)).astype(BF16)
    o_ref[...] = jnp.dot(sc, w_ref[...].astype(BF16), preferred_element_type=F32) + b_ref[...]


def _modulation(c_all, w_mod, b_mod, tn=1024):
    d, n = w_mod.shape
    return pl.pallas_call(
        _mod_kernel,
        grid=(n // tn,),
        in_specs=[pl.BlockSpec((MOD_ROWS, d), lambda j: (0, 0)),
                  pl.BlockSpec((d, tn), lambda j: (0, j)),
                  pl.BlockSpec((1, tn), lambda j: (0, j))],
        out_specs=pl.BlockSpec((MOD_ROWS, tn), lambda j: (0, j)),
        out_shape=jax.ShapeDtypeStruct((MOD_ROWS, n), F32),
        compiler_params=_params(1),
        name="modulation",
    )(c_all, w_mod, b_mod.reshape(1, n))


def _prenorm_kernel(x_ref, g_ref, sh_ref, sc_ref, o_ref):
    y = _rms(x_ref[...]) * g_ref[...]
    o_ref[...] = (y * (1.0 + sc_ref[...]) + sh_ref[...]).astype(o_ref.dtype)


def _mod_spec(slot, rows_per_mod_row, tm, fixed_row=None):
    if fixed_row is None:
        return pl.BlockSpec((None, None, 1, D_MODEL), lambda i: ((i * tm) // rows_per_mod_row, slot, 0, 0))
    return pl.BlockSpec((None, None, 1, D_MODEL), lambda i: (fixed_row, slot, 0, 0))


def _norm_spec(slot):
    return pl.BlockSpec((None, 1, D_MODEL), lambda i: (slot, 0, 0))


def _prenorm(x2, norm_g, mod, rows_per_mod_row, fixed_row=None, tm=256):
    m = x2.shape[0]
    return pl.pallas_call(
        _prenorm_kernel,
        grid=(m // tm,),
        in_specs=[pl.BlockSpec((tm, D_MODEL), lambda i: (i, 0)),
                  _norm_spec(0),
                  _mod_spec(0, rows_per_mod_row, tm, fixed_row),
                  _mod_spec(1, rows_per_mod_row, tm, fixed_row)],
        out_specs=pl.BlockSpec((tm, D_MODEL), lambda i: (i, 0)),
        out_shape=jax.ShapeDtypeStruct((m, D_MODEL), BF16),
        compiler_params=_params(1),
        name="prenorm",
    )(x2, norm_g, mod, mod)


def _post_mix_kernel(x_ref, mix_ref, gn1_ref, gn2_ref, g1_ref, sh2_ref, s2_ref, h1_ref, u2_ref):
    h1 = x_ref[...] + g1_ref[...] * (_rms(mix_ref[...]) * gn1_ref[...])
    h1_ref[...] = h1
    y2 = _rms(h1) * gn2_ref[...]
    u2_ref[...] = (y2 * (1.0 + s2_ref[...]) + sh2_ref[...]).astype(u2_ref.dtype)


def _post_mix(x2, mix, norm_g, mod, rows_per_mod_row, tm=256):
    m = x2.shape[0]
    row = pl.BlockSpec((tm, D_MODEL), lambda i: (i, 0))
    return pl.pallas_call(
        _post_mix_kernel,
        grid=(m // tm,),
        in_specs=[row, row, _norm_spec(1), _norm_spec(2),
                  _mod_spec(2, rows_per_mod_row, tm), _mod_spec(3, rows_per_mod_row, tm),
                  _mod_spec(4, rows_per_mod_row, tm)],
        out_specs=[row, row],
        out_shape=[jax.ShapeDtypeStruct((m, D_MODEL), F32), jax.ShapeDtypeStruct((m, D_MODEL), BF16)],
        compiler_params=_params(1),
        name="post_mix",
    )(x2, mix, norm_g, norm_g, mod, mod, mod)


def _post_mlp_kernel(h1_ref, y_ref, gn_ref, g2_ref, o_ref):
    o_ref[...] = h1_ref[...] + g2_ref[...] * (_rms(y_ref[...]) * gn_ref[...])


def _post_mlp(h1, y, norm_g, mod, rows_per_mod_row, tm=256):
    m = h1.shape[0]
    row = pl.BlockSpec((tm, D_MODEL), lambda i: (i, 0))
    return pl.pallas_call(
        _post_mlp_kernel,
        grid=(m // tm,),
        in_specs=[row, row, _norm_spec(3), _mod_spec(5, rows_per_mod_row, tm)],
        out_specs=row,
        out_shape=jax.ShapeDtypeStruct((m, D_MODEL), F32),
        compiler_params=_params(1),
        name="post_mlp",
    )(h1, y, norm_g, mod)


_NT = (((1,), (1,)), ((), ()))


def _stream_kernel(a_ref, slab_ref, *rest, nj, transposed, tile_of, epilogue, n_extra, has_next, has_side):
    j = pl.program_id(0)
    i = pl.program_id(1)
    rest = list(rest)
    next_ref = rest.pop(0) if has_next else None
    extra = [rest.pop(0) for _ in range(n_extra)]
    side_ref = rest.pop(0) if has_side else None
    o_ref = rest.pop(0)
    side_out_ref = rest.pop(0) if has_side else None
    w_scr = rest.pop(0)
    rows = slab_ref.shape[0]
    r0 = pl.multiple_of(i * rows, rows)

    @pl.when(j < nj)
    def _():
        dst = w_scr.at[j % 2]
        if has_next:
            tile_rows = w_scr.shape[1]
            shifted = tile_of(j) * tile_rows >= W_IN_GATE_LO

            @pl.when(jnp.logical_not(shifted))
            def _():
                dst[pl.ds(r0, rows), :] = slab_ref[...].astype(BF16)

            @pl.when(shifted)
            def _():
                dst[pl.ds(r0, rows - N_GATES), :] = slab_ref[N_GATES:, :].astype(BF16)
                dst[pl.ds(r0 + (rows - N_GATES), N_GATES), :] = next_ref[...].astype(BF16)
        else:
            dst[pl.ds(r0, rows), :] = slab_ref[...].astype(BF16)

    @pl.when(j > 0)
    def _():
        w = w_scr[(j - 1) % 2]
        if transposed:
            acc = lax.dot_general(a_ref[...], w, _NT, preferred_element_type=F32)
        else:
            acc = jnp.dot(a_ref[...], w, preferred_element_type=F32)
        o_ref[...] = epilogue(acc, *[e[...] for e in extra]).astype(o_ref.dtype)
        if has_side:
            side_out_ref[...] = side_ref[...].astype(side_out_ref.dtype)


def _stream_matmul(a, w, *, n_out, out_dtype, epilogue=lambda acc: acc, extra=(), extra_off=(), transposed=False,
                   tile_of=None, side=None, tm=1024, tn=1024, name):
    m, k = a.shape
    nj, ni = n_out // tn, m // tm
    tile_of = (lambda j: j) if tile_of is None else tile_of

    def prev(j):
        return jnp.maximum(j - 1, 0)

    def cur(j):
        return jnp.minimum(j, nj - 1)

    def row(j, i):
        return jnp.where(j == 0, 0, i)

    in_specs = [pl.BlockSpec((tm, k), lambda j, i: (row(j, i), 0))]
    args = [a]
    if transposed:
        slab_rows = tn // ni
        in_specs.append(pl.BlockSpec((slab_rows, k), lambda j, i: (tile_of(cur(j)) * ni + i, 0)))
        in_specs.append(pl.BlockSpec(
            (N_GATES, k), lambda j, i: ((tile_of(cur(j)) * tn + (i + 1) * slab_rows) // N_GATES, 0)))
        args += [w, w]
        scratch = pltpu.VMEM((2, tn, k), BF16)
    else:
        in_specs.append(pl.BlockSpec((None, k // ni, tn), lambda j, i: (0, i, cur(j))))
        args.append(w)
        scratch = pltpu.VMEM((2, k, tn), BF16)
    for arr, off in zip(extra, extra_off):
        in_specs.append(pl.BlockSpec((tm, tn), lambda j, i, off=off: (row(j, i), off // tn + prev(j))))
        args.append(arr)
    out_specs = [pl.BlockSpec((tm, tn), lambda j, i: (row(j, i), prev(j)))]
    out_shape = [jax.ShapeDtypeStruct((m, n_out), out_dtype)]
    if side is not None:
        s_rows, s_cols = side.shape[1:]
        slab = s_rows // (nj * ni)

        def step(j, i):
            return jnp.maximum((j - 1) * ni + i, 0)

        in_specs.append(pl.BlockSpec((None, slab, s_cols), lambda j, i: (0, step(j, i), 0)))
        args.append(side)
        out_specs.append(pl.BlockSpec((slab, s_cols), lambda j, i: (step(j, i), 0)))
        out_shape.append(jax.ShapeDtypeStruct((s_rows, s_cols), BF16))
    out = pl.pallas_call(
        functools.partial(_stream_kernel, nj=nj, transposed=transposed, tile_of=tile_of, epilogue=epilogue,
                          n_extra=len(extra), has_next=transposed, has_side=side is not None),
        grid=(nj + 1, ni),
        in_specs=in_specs,
        out_specs=out_specs,
        out_shape=out_shape,
        scratch_shapes=[scratch],
        compiler_params=_params(2),
        name=name,
    )(*args)
    return out if side is not None else out[0]


def _gate_epilogue(acc, g):
    return _sigmoid(g.astype(F32)) * acc


def _gate_add_epilogue(acc, g, prev):
    return prev.astype(F32) + _sigmoid(g.astype(F32)) * acc


def _relu2_epilogue(acc):
    return jnp.square(jnp.maximum(acc, 0.0))


def _gate_proj_kernel(a_ref, w_ref, b_ref, o_ref):
    o_ref[...] = lax.dot_general(a_ref[...], w_ref[...], _NT, preferred_element_type=F32) + b_ref[...]


def _gate_proj(a, w_gate_t, bias, tm=1024, name="gate_proj"):
    m, k = a.shape
    n = w_gate_t.shape[0]
    return pl.pallas_call(
        _gate_proj_kernel,
        grid=(m // tm,),
        in_specs=[pl.BlockSpec((tm, k), lambda i: (i, 0)),
                  pl.BlockSpec((n, k), lambda i: (0, 0)),
                  pl.BlockSpec((1, n), lambda i: (0, 0))],
        out_specs=pl.BlockSpec((tm, n), lambda i: (i, 0)),
        out_shape=jax.ShapeDtypeStruct((m, n), F32),
        compiler_params=_params(1),
        name=name,
    )(a, w_gate_t, bias)


def _matmul_ksplit_kernel(a_ref, w_ref, o_ref):
    @pl.when(pl.program_id(2) == 0)
    def _():
        o_ref[...] = jnp.zeros_like(o_ref)

    o_ref[...] += jnp.dot(a_ref[...], w_ref[...], preferred_element_type=F32)


def _matmul_ksplit(a, w, *, tm, tn, tk, name):
    m, k = a.shape
    n = w.shape[1]
    return pl.pallas_call(
        _matmul_ksplit_kernel,
        grid=(n // tn, m // tm, k // tk),
        in_specs=[pl.BlockSpec((tm, tk), lambda j, i, kk: (i, kk)),
                  pl.BlockSpec((tk, tn), lambda j, i, kk: (kk, j))],
        out_specs=pl.BlockSpec((tm, tn), lambda j, i, kk: (i, j)),
        out_shape=jax.ShapeDtypeStruct((m, n), F32),
        compiler_params=_params(3),
        name=name,
    )(a, w)


def _mlstm_chunk(q_ref, k_ref, v_ref, gc_ref, gr_ref, ct_ref, n_ref, m_ref, h_ref, *, reverse):
    length = k_ref.shape[0]
    ji, jf = (2, 3) if reverse else (0, 1)
    gc = gc_ref[...]
    gr = gr_ref[...]
    ig_c = gc[:, ji:ji + 1]
    lf_c = _log_sigmoid(gc[:, jf:jf + 1])
    ig_r = gr[ji:ji + 1, :]
    lf_r = _log_sigmoid(gr[jf:jf + 1, :])
    t_idx = lax.broadcasted_iota(jnp.int32, (length, length), 0)
    s_idx = lax.broadcasted_iota(jnp.int32, (length, length), 1)
    lower = s_idx <= t_idx
    upper = s_idx >= t_idx
    seen, seen_t = (upper, lower) if reverse else (lower, upper)
    b_c = jnp.sum(jnp.where(seen, lf_r, 0.0), axis=1, keepdims=True)
    b_r = jnp.sum(jnp.where(seen_t, lf_c, 0.0), axis=0, keepdims=True)
    b_tot = jnp.sum(lf_r, axis=1, keepdims=True)
    m_prev = m_ref[...]
    k = k_ref[...]
    v = v_ref[...]

    if h_ref is not None:
        q = q_ref[...]
        a_c = b_c + m_prev
        dm = jnp.where(seen, b_c - b_r + ig_r, -jnp.inf)
        m_t = jnp.maximum(a_c, jnp.max(dm, axis=1, keepdims=True))
        qk = lax.dot_general(q, k, (((1,), (1,)), ((), ())), preferred_element_type=F32)
        s = qk * Q_SCALE * jnp.exp(dm - m_t)
        inter = jnp.exp(a_c - m_t) * Q_SCALE
        qc = jnp.dot(q, ct_ref[...].astype(BF16), preferred_element_type=F32)
        num = inter * qc + jnp.dot(s.astype(BF16), v, preferred_element_type=F32)
        qn = jnp.sum(q.astype(F32) * n_ref[...], axis=1, keepdims=True)
        den = inter * qn + jnp.sum(s, axis=1, keepdims=True)
        h_ref[...] = (num * (1.0 / jnp.maximum(jnp.abs(den), jnp.exp(-m_t)))).astype(h_ref.dtype)

    w_c = b_tot - b_c + ig_c
    m_new = jnp.maximum(b_tot + m_prev, jnp.max(w_c, axis=0, keepdims=True))
    decay = jnp.exp(b_tot + m_prev - m_new)
    kw = k.astype(F32) * jnp.exp(w_c - m_new)
    ct_ref[...] = decay * ct_ref[...] + lax.dot_general(
        kw.astype(BF16), v, (((0,), (0,)), ((), ())), preferred_element_type=F32)
    n_ref[...] = decay * n_ref[...] + jnp.sum(kw, axis=0, keepdims=True)
    m_ref[...] = m_new


def _mlstm_kernel(qf, kf, vf, gcf, grf, qb, kb, vb, gcb, grb, kc, vc, gcc, grc,
                  hf, hb, ctf, nf, mf, ctb, nb, mb):
    c = pl.program_id(2)

    @pl.when(c == 0)
    def _():
        for ct, n, m in ((ctf, nf, mf), (ctb, nb, mb)):
            ct[...] = jnp.zeros_like(ct)
            n[...] = jnp.zeros_like(n)
            m[...] = jnp.zeros_like(m)
        _mlstm_chunk(None, kc, vc, gcc, grc, ctf, nf, mf, None, reverse=False)
        _mlstm_chunk(None, kc, vc, gcc, grc, ctb, nb, mb, None, reverse=True)

    @pl.when(c > 0)
    def _():
        _mlstm_chunk(qf, kf, vf, gcf, grf, ctf, nf, mf, hf, reverse=False)
        _mlstm_chunk(qb, kb, vb, gcb, grb, ctb, nb, mb, hb, reverse=True)


def _mlstm(proj, proj_ctx, gc, gr, gc_ctx, gr_ctx, batch):
    length = M_CHUNK
    assert CTX_LEN == length
    seq = proj.shape[0] // batch
    nc = seq // length

    def fwd(c):
        return jnp.maximum(c - 1, 0)

    def bwd(c):
        return nc - jnp.maximum(c, 1)

    def lat_specs(chunk_of):
        return [
            pl.BlockSpec((length, M_DQK), lambda b, h, c: (b * nc + chunk_of(c), OFF_QM // M_DQK + h)),
            pl.BlockSpec((length, M_DQK), lambda b, h, c: (b * nc + chunk_of(c), OFF_KM // M_DQK + h)),
            pl.BlockSpec((length, M_DV), lambda b, h, c: (b * nc + chunk_of(c), OFF_VM // M_DV + h)),
            pl.BlockSpec((None, None, length, 4), lambda b, h, c: (b, h, chunk_of(c), 0)),
            pl.BlockSpec((None, None, 4, length), lambda b, h, c: (b, h, 0, chunk_of(c))),
        ]

    ctx_specs = [
        pl.BlockSpec((length, M_DQK), lambda b, h, c: (b, CTX_OFF_KM // M_DQK + h)),
        pl.BlockSpec((length, M_DV), lambda b, h, c: (b, CTX_OFF_VM // M_DV + h)),
        pl.BlockSpec((None, None, length, 4), lambda b, h, c: (b, h, 0, 0)),
        pl.BlockSpec((None, None, 4, length), lambda b, h, c: (b, h, 0, 0)),
    ]
    out_shape = jax.ShapeDtypeStruct((batch * seq, M_V_W), BF16)
    state = [pltpu.VMEM((M_DQK, M_DV), F32), pltpu.VMEM((1, M_DQK), F32), pltpu.VMEM((1, 1), F32)]
    return pl.pallas_call(
        _mlstm_kernel,
        grid=(batch, M_HEADS, nc + 1),
        in_specs=lat_specs(fwd) + lat_specs(bwd) + ctx_specs,
        out_specs=[pl.BlockSpec((length, M_DV), lambda b, h, c: (b * nc + fwd(c), h)),
                   pl.BlockSpec((length, M_DV), lambda b, h, c: (b * nc + bwd(c), h))],
        out_shape=[out_shape, out_shape],
        scratch_shapes=state + state,
        compiler_params=_params(3),
        name="mlstm",
    )(proj, proj, proj, gc, gr, proj, proj, proj, gc, gr, proj_ctx, proj_ctx, gc_ctx, gr_ctx)


def _readout_kernel(hf_ref, hb_ref, om_ref, g_ref, o_ref):
    h = hf_ref[...].astype(F32) + hb_ref[...].astype(F32)
    y = _rms(h) * g_ref[...] * _sigmoid(om_ref[...].astype(F32))
    o_ref[...] = y.astype(o_ref.dtype)


def _readout(hf, hb, proj, m_norm_g, tm=512):
    m = hf.shape[0]
    blk = pl.BlockSpec((tm, M_DV), lambda i, h: (i, h))
    return pl.pallas_call(
        _readout_kernel,
        grid=(m // tm, M_HEADS),
        in_specs=[blk, blk,
                  pl.BlockSpec((tm, M_DV), lambda i, h: (i, OFF_OM // M_DV + h)),
                  pl.BlockSpec((1, M_DV), lambda i, h: (0, h))],
        out_specs=blk,
        out_shape=jax.ShapeDtypeStruct((m, M_V_W), BF16),
        compiler_params=_params(2),
        name="mlstm_readout",
    )(hf, hb, proj, m_norm_g.reshape(1, M_V_W))


def _rope(x, cos, sin_signed):
    lane = lax.broadcasted_iota(jnp.int32, x.shape, 1)
    partner = jnp.where((lane % 64) < 32, pltpu.roll(x, LANES - 32, 1), pltpu.roll(x, 32, 1))
    return x * cos + partner * sin_signed


def _attn_kernel(sink_ref, q_ref, k_ref, v_ref, kc_ref, vc_ref, cos_ref, sin_ref, o_ref,
                 kpad, vpad, s_buf, p_buf, den_buf):
    seq = k_ref.shape[0]
    nb = seq // A_BLOCK
    kvh = pl.program_id(1)
    scale = A_HEAD_DIM ** -0.5
    rows = A_GROUP * A_BLOCK
    n_win = 3 * A_BLOCK
    zeros = jnp.zeros((A_BLOCK, A_HEAD_DIM), BF16)
    kpad[0:A_BLOCK, :] = zeros
    kpad[A_BLOCK + seq:, :] = zeros
    vpad[0:A_BLOCK, :] = zeros
    vpad[A_BLOCK + seq:, :] = zeros
    kpad[A_BLOCK:A_BLOCK + seq, :] = _rope(k_ref[...].astype(F32), cos_ref[...], sin_ref[...]).astype(BF16)
    vpad[A_BLOCK:A_BLOCK + seq, :] = v_ref[...]
    qi = lax.broadcasted_iota(jnp.int32, (rows, n_win), 0) & (A_BLOCK - 1)
    kj = lax.broadcasted_iota(jnp.int32, (rows, n_win), 1)
    band = (kj >= qi) & (kj <= qi + 2 * WINDOW)
    kj_row = lax.broadcasted_iota(jnp.int32, (1, n_win), 1)
    sink = jnp.concatenate(
        [jnp.full((A_BLOCK, 1), sink_ref[kvh * A_GROUP + g], F32) for g in range(A_GROUP)], axis=0)
    nt = (((1,), (1,)), ((), ()))

    def scores(n):
        r0 = pl.multiple_of(n * A_BLOCK, A_BLOCK)
        cos = cos_ref[pl.ds(r0, A_BLOCK), :]
        sin = sin_ref[pl.ds(r0, A_BLOCK), :]
        q = jnp.concatenate(
            [_rope(q_ref[pl.ds(r0, A_BLOCK), g * A_HEAD_DIM:(g + 1) * A_HEAD_DIM].astype(F32), cos, sin)
             for g in range(A_GROUP)], axis=0).astype(BF16)
        kpos = r0 - A_BLOCK + kj_row
        valid = band & ((kpos >= 0) & (kpos < seq))
        sw = lax.dot_general(q, kpad[pl.ds(r0, n_win), :], nt, preferred_element_type=F32) * scale
        s_buf[:, 0:n_win] = jnp.where(valid, sw, -jnp.inf)
        s_buf[:, n_win:] = lax.dot_general(q, kc_ref[...], nt, preferred_element_type=F32) * scale

    def softmax():
        s = s_buf[...]
        mx = jnp.maximum(jnp.max(s, axis=1, keepdims=True), sink)
        p = jnp.exp(s - mx)
        den_buf[...] = jnp.sum(p, axis=1, keepdims=True) + jnp.exp(sink - mx)
        p_buf[...] = p.astype(p_buf.dtype)

    def output(n):
        r0 = pl.multiple_of(n * A_BLOCK, A_BLOCK)
        o = (jnp.dot(p_buf[:, 0:n_win], vpad[pl.ds(r0, n_win), :], preferred_element_type=F32)
             + jnp.dot(p_buf[:, n_win:], vc_ref[...], preferred_element_type=F32)) * (1.0 / den_buf[...])
        for g in range(A_GROUP):
            o_ref[pl.ds(r0, A_BLOCK), g * A_HEAD_DIM:(g + 1) * A_HEAD_DIM] = (
                o[g * A_BLOCK:(g + 1) * A_BLOCK, :].astype(o_ref.dtype))

    scores(0)
    softmax()
    scores(1)

    def body(t, carry):
        output(t - 1)
        softmax()
        scores(t + 1)
        return carry

    lax.fori_loop(1, nb - 1, body, 0)
    output(nb - 2)
    softmax()
    output(nb - 1)


def _attention(proj, proj_ctx, sink, cos, sin, batch):
    seq = proj.shape[0] // batch
    qw = A_GROUP * A_HEAD_DIM
    tab = pl.BlockSpec((seq, A_HEAD_DIM), lambda b, h: (0, 0))
    return pl.pallas_call(
        _attn_kernel,
        grid=(batch, A_KV_HEADS),
        in_specs=[pl.BlockSpec(memory_space=pltpu.SMEM),
                  pl.BlockSpec((seq, qw), lambda b, h: (b, OFF_QA // qw + h)),
                  pl.BlockSpec((seq, A_HEAD_DIM), lambda b, h: (b, OFF_KA // A_HEAD_DIM + h)),
                  pl.BlockSpec((seq, A_HEAD_DIM), lambda b, h: (b, OFF_VA // A_HEAD_DIM + h)),
                  pl.BlockSpec((CTX_LEN, A_HEAD_DIM), lambda b, h: (b, CTX_OFF_KA // A_HEAD_DIM + h)),
                  pl.BlockSpec((CTX_LEN, A_HEAD_DIM), lambda b, h: (b, CTX_OFF_VA // A_HEAD_DIM + h)),
                  tab, tab],
        out_specs=pl.BlockSpec((seq, qw), lambda b, h: (b, h)),
        out_shape=jax.ShapeDtypeStruct((batch * seq, A_Q_W), BF16),
        scratch_shapes=[pltpu.VMEM((seq + 2 * A_BLOCK, A_HEAD_DIM), BF16),
                        pltpu.VMEM((seq + 2 * A_BLOCK, A_HEAD_DIM), BF16),
                        pltpu.VMEM((A_GROUP * A_BLOCK, 3 * A_BLOCK + CTX_LEN), F32),
                        pltpu.VMEM((A_GROUP * A_BLOCK, 3 * A_BLOCK + CTX_LEN), BF16),
                        pltpu.VMEM((A_GROUP * A_BLOCK, 1), F32)],
        compiler_params=_params(2),
        name="window_attention",
    )(sink, proj, proj, proj, proj_ctx, proj_ctx, cos, sin)


def _rope_tables(seq):
    t = jnp.arange(seq)
    nf = A_HEAD_DIM // 4
    freqs = ROPE_BASE ** (-jnp.arange(nf, dtype=F32) / nf)
    ang_r = (t // GRID_W).astype(F32)[:, None] * freqs
    ang_c = (t % GRID_W).astype(F32)[:, None] * freqs
    cos = jnp.concatenate([jnp.cos(ang_r), jnp.cos(ang_r), jnp.cos(ang_c), jnp.cos(ang_c)], axis=-1)
    sin = jnp.concatenate([-jnp.sin(ang_r), jnp.sin(ang_r), -jnp.sin(ang_c), jnp.sin(ang_c)], axis=-1)
    return cos, sin


def _gate_layouts(g, batch):
    g4 = g[:, :N_GATES].reshape(batch, -1, 4, M_HEADS)
    return g4.transpose(0, 3, 1, 2), g4.transpose(0, 3, 2, 1)


def kernel(x, c, ctx, c_ctx, w_mod, b_mod, norm_g, w_in, m_gate_b, m_norm_g, attn_sink,
           w_out_m, w_out_a, w_o, w_ff1, w_ff2):
    batch, seq, d = x.shape
    n_ctx = ctx.shape[1]
    x2 = x.reshape(batch * seq, d)
    ctx2 = ctx.reshape(batch * n_ctx, d)

    w_in_t = w_in[0].T
    w_gate_t = jnp.pad(w_in_t[W_IN_GATE_LO:W_IN_GATE_HI], ((0, GATE_PAD - N_GATES), (0, 0))).astype(BF16)
    gate_bias = jnp.pad(m_gate_b[0].reshape(1, N_GATES), ((0, 0), (0, GATE_PAD - N_GATES)))

    c_all = jnp.zeros((MOD_ROWS, d), F32).at[:batch].set(c).at[batch].set(c_ctx)
    mod = _modulation(c_all, w_mod[0], b_mod[0]).reshape(MOD_ROWS, 6, 1, d)
    gn = norm_g[0].reshape(4, 1, d)

    u = _prenorm(x2, gn, mod, seq)
    uc = _prenorm(ctx2, gn, mod, n_ctx, fixed_row=batch)

    proj = _stream_matmul(u, w_in_t, n_out=PROJ_ALIGNED_W, out_dtype=BF16, transposed=True, name="in_proj")
    g_lat = _gate_proj(u, w_gate_t, gate_bias)
    ctx_tn = 512
    lat_tiles = (OFF_OM - OFF_KM) // ctx_tn
    proj_ctx = _stream_matmul(
        uc, w_in_t, n_out=CTX_PROJ_W, out_dtype=BF16, transposed=True, tn=ctx_tn, name="in_proj_ctx",
        tile_of=lambda j: jnp.where(j < lat_tiles, j + OFF_KM // ctx_tn, j - lat_tiles + OFF_KA // ctx_tn))
    g_ctx = _gate_proj(uc, w_gate_t, gate_bias, name="gate_proj_ctx")

    gc, gr = _gate_layouts(g_lat, batch)
    gc_ctx, gr_ctx = _gate_layouts(g_ctx, batch)
    hf, hb = _mlstm(proj, proj_ctx, gc, gr, gc_ctx, gr_ctx, batch)
    hm = _readout(hf, hb, proj, m_norm_g[0])

    cos, sin = _rope_tables(seq)
    at = _attention(proj, proj_ctx, attn_sink[0], cos, sin, batch)

    zm = _stream_matmul(hm, w_out_m, n_out=d, out_dtype=BF16, epilogue=_gate_epilogue,
                        extra=[proj], extra_off=[OFF_BGM], name="out_m")
    z = _stream_matmul(at, w_out_a, n_out=d, out_dtype=BF16, epilogue=_gate_add_epilogue,
                       extra=[proj, zm], extra_off=[OFF_BGA, 0], name="out_a_merge")
    mix = _stream_matmul(z, w_o, n_out=d, out_dtype=F32, name="out_proj")
    h1, u2 = _post_mix(x2, mix, gn, mod, seq)

    hid, w_ff2_bf16 = _stream_matmul(u2, w_ff1, n_out=D_FF, out_dtype=BF16, epilogue=_relu2_epilogue,
                                     side=w_ff2, name="ff1")
    y = _matmul_ksplit(hid, w_ff2_bf16, tm=1024, tn=1024, tk=4096, name="ff2")
    out = _post_mlp(h1, y, gn, mod, seq)
    return out.reshape(batch, seq, d)
```

```python
import functools

import jax
import jax.numpy as jnp
from jax import lax
from jax.experimental import pallas as pl
from jax.experimental.pallas import tpu as pltpu

F32 = jnp.float32
BF16 = jnp.bfloat16

D_MODEL = 4096
GRID_W = 64
CTX_LEN = 256
M_HEADS = 8
M_DQK = 256
M_DV = 512
A_HEADS = 32
A_KV_HEADS = 8
A_HEAD_DIM = 128
A_GROUP = A_HEADS // A_KV_HEADS
WINDOW = 128
A_BLOCK = 128
ROPE_BASE = 10000.0
D_FF = 4 * D_MODEL
EPS = 1e-6

M_QK_W = M_HEADS * M_DQK
M_V_W = M_HEADS * M_DV
A_Q_W = A_HEADS * A_HEAD_DIM
A_KV_W = A_KV_HEADS * A_HEAD_DIM
N_GATES = 4 * M_HEADS

W_IN_GATE_LO = 2 * M_QK_W + 2 * M_V_W
W_IN_GATE_HI = W_IN_GATE_LO + N_GATES
OFF_QM = 0
OFF_KM = OFF_QM + M_QK_W
OFF_VM = OFF_KM + M_QK_W
OFF_OM = OFF_VM + M_V_W
OFF_QA = OFF_OM + M_V_W
OFF_KA = OFF_QA + A_Q_W
OFF_VA = OFF_KA + A_KV_W
OFF_BGM = OFF_VA + A_KV_W
OFF_BGA = OFF_BGM + D_MODEL
PROJ_ALIGNED_W = OFF_BGA + D_MODEL
CTX_OFF_KM = 0
CTX_OFF_VM = M_QK_W
CTX_OFF_KA = M_QK_W + M_V_W
CTX_OFF_VA = CTX_OFF_KA + A_KV_W
CTX_PROJ_W = CTX_OFF_VA + A_KV_W

M_CHUNK = 256
Q_SCALE = M_DQK ** -0.5
LANES = 128
GATE_PAD = LANES
MOD_ROWS = 8
VMEM_LIMIT = 60000 * 1024


def _params(n_axes):
    return pltpu.CompilerParams(dimension_semantics=("arbitrary",) * n_axes, vmem_limit_bytes=VMEM_LIMIT)


def _sigmoid(x):
    return 1.0 / (1.0 + jnp.exp(-x))


def _log_sigmoid(x):
    return jnp.minimum(x, 0.0) - jnp.log1p(jnp.exp(-jnp.abs(x)))


def _rms(x):
    return x * lax.rsqrt(jnp.mean(x * x, axis=-1, keepdims=True) + EPS)


def _mod_kernel(c_ref, w_ref, b_ref, o_ref):
    c = c_ref[...]
    sc = (c * _sigmoid(c)).astype(BF16)
    o_ref[...] = jnp.dot(sc, w_ref[...].astype(BF16), preferred_element_type=F32) + b_ref[...]


def _modulation(c_all, w_mod, b_mod, tn=1024):
    d, n = w_mod.shape
    return pl.pallas_call(
        _mod_kernel,
        grid=(n // tn,),
        in_specs=[pl.BlockSpec((MOD_ROWS, d), lambda j: (0, 0)),
                  pl.BlockSpec((d, tn), lambda j: (0, j)),
                  pl.BlockSpec((1, tn), lambda j: (0, j))],
        out_specs=pl.BlockSpec((MOD_ROWS, tn), lambda j: (0, j)),
        out_shape=jax.ShapeDtypeStruct((MOD_ROWS, n), F32),
        compiler_params=_params(1),
        name="modulation",
    )(c_all, w_mod, b_mod.reshape(1, n))


def _prenorm_kernel(x_ref, g_ref, sh_ref, sc_ref, o_ref):
    y = _rms(x_ref[...]) * g_ref[...]
    o_ref[...] = (y * (1.0 + sc_ref[...]) + sh_ref[...]).astype(o_ref.dtype)


def _mod_spec(slot, rows_per_mod_row, tm, fixed_row=None):
    if fixed_row is None:
        return pl.BlockSpec((None, None, 1, D_MODEL), lambda i: ((i * tm) // rows_per_mod_row, slot, 0, 0))
    return pl.BlockSpec((None, None, 1, D_MODEL), lambda i: (fixed_row, slot, 0, 0))


def _norm_spec(slot):
    return pl.BlockSpec((None, 1, D_MODEL), lambda i: (slot, 0, 0))


def _prenorm(x2, norm_g, mod, rows_per_mod_row, fixed_row=None, tm=256):
    m = x2.shape[0]
    return pl.pallas_call(
        _prenorm_kernel,
        grid=(m // tm,),
        in_specs=[pl.BlockSpec((tm, D_MODEL), lambda i: (i, 0)),
                  _norm_spec(0),
                  _mod_spec(0, rows_per_mod_row, tm, fixed_row),
                  _mod_spec(1, rows_per_mod_row, tm, fixed_row)],
        out_specs=pl.BlockSpec((tm, D_MODEL), lambda i: (i, 0)),
        out_shape=jax.ShapeDtypeStruct((m, D_MODEL), BF16),
        compiler_params=_params(1),
        name="prenorm",
    )(x2, norm_g, mod, mod)


def _post_mix_kernel(x_ref, mix_ref, gn1_ref, gn2_ref, g1_ref, sh2_ref, s2_ref, h1_ref, u2_ref):
    h1 = x_ref[...] + g1_ref[...] * (_rms(mix_ref[...].astype(F32)) * gn1_ref[...])
    h1_ref[...] = h1
    y2 = _rms(h1) * gn2_ref[...]
    u2_ref[...] = (y2 * (1.0 + s2_ref[...]) + sh2_ref[...]).astype(u2_ref.dtype)


def _post_mix(x2, mix, norm_g, mod, rows_per_mod_row, tm=256):
    m = x2.shape[0]
    row = pl.BlockSpec((tm, D_MODEL), lambda i: (i, 0))
    return pl.pallas_call(
        _post_mix_kernel,
        grid=(m // tm,),
        in_specs=[row, row, _norm_spec(1), _norm_spec(2),
                  _mod_spec(2, rows_per_mod_row, tm), _mod_spec(3, rows_per_mod_row, tm),
                  _mod_spec(4, rows_per_mod_row, tm)],
        out_specs=[row, row],
        out_shape=[jax.ShapeDtypeStruct((m, D_MODEL), F32), jax.ShapeDtypeStruct((m, D_MODEL), BF16)],
        compiler_params=_params(1),
        name="post_mix",
    )(x2, mix, norm_g, norm_g, mod, mod, mod)


def _post_mlp_kernel(h1_ref, y_ref, gn_ref, g2_ref, o_ref):
    o_ref[...] = h1_ref[...] + g2_ref[...] * (_rms(y_ref[...].astype(F32)) * gn_ref[...])


def _post_mlp(h1, y, norm_g, mod, rows_per_mod_row, tm=256):
    m = h1.shape[0]
    row = pl.BlockSpec((tm, D_MODEL), lambda i: (i, 0))
    return pl.pallas_call(
        _post_mlp_kernel,
        grid=(m // tm,),
        in_specs=[row, row, _norm_spec(3), _mod_spec(5, rows_per_mod_row, tm)],
        out_specs=row,
        out_shape=jax.ShapeDtypeStruct((m, D_MODEL), F32),
        compiler_params=_params(1),
        name="post_mlp",
    )(h1, y, norm_g, mod)


_NT = (((1,), (1,)), ((), ()))


def _stream_kernel(a_ref, slab_ref, *rest, nj, transposed, tile_of, epilogue, n_extra, has_next, has_side):
    j = pl.program_id(0)
    i = pl.program_id(1)
    rest = list(rest)
    next_ref = rest.pop(0) if has_next else None
    extra = [rest.pop(0) for _ in range(n_extra)]
    side_ref = rest.pop(0) if has_side else None
    o_ref = rest.pop(0)
    side_out_ref = rest.pop(0) if has_side else None
    w_scr = rest.pop(0)
    rows = slab_ref.shape[0]
    r0 = pl.multiple_of(i * rows, rows)

    @pl.when(j < nj)
    def _():
        dst = w_scr.at[j % 2]
        if has_next:
            tile_rows = w_scr.shape[1]
            shifted = tile_of(j) * tile_rows >= W_IN_GATE_LO

            @pl.when(jnp.logical_not(shifted))
            def _():
                dst[pl.ds(r0, rows), :] = slab_ref[...].astype(BF16)

            @pl.when(shifted)
            def _():
                dst[pl.ds(r0, rows - N_GATES), :] = slab_ref[N_GATES:, :].astype(BF16)
                dst[pl.ds(r0 + (rows - N_GATES), N_GATES), :] = next_ref[...].astype(BF16)
        else:
            dst[pl.ds(r0, rows), :] = slab_ref[...].astype(BF16)

    @pl.when(j > 0)
    def _():
        w = w_scr[(j - 1) % 2]
        if transposed:
            acc = lax.dot_general(a_ref[...], w, _NT, preferred_element_type=F32)
        else:
            acc = jnp.dot(a_ref[...], w, preferred_element_type=F32)
        o_ref[...] = epilogue(acc, *[e[...] for e in extra]).astype(o_ref.dtype)
        if has_side:
            side_out_ref[...] = side_ref[...].astype(side_out_ref.dtype)


def _stream_matmul(a, w, *, n_out, out_dtype, epilogue=lambda acc: acc, extra=(), extra_off=(), transposed=False,
                   tile_of=None, side=None, tm=1024, tn=1024, name):
    m, k = a.shape
    nj, ni = n_out // tn, m // tm
    tile_of = (lambda j: j) if tile_of is None else tile_of

    def prev(j):
        return jnp.maximum(j - 1, 0)

    def cur(j):
        return jnp.minimum(j, nj - 1)

    def row(j, i):
        return jnp.where(j == 0, 0, i)

    in_specs = [pl.BlockSpec((tm, k), lambda j, i: (row(j, i), 0))]
    args = [a]
    if transposed:
        slab_rows = tn // ni
        in_specs.append(pl.BlockSpec((slab_rows, k), lambda j, i: (tile_of(cur(j)) * ni + i, 0)))
        in_specs.append(pl.BlockSpec(
            (N_GATES, k), lambda j, i: ((tile_of(cur(j)) * tn + (i + 1) * slab_rows) // N_GATES, 0)))
        args += [w, w]
        scratch = pltpu.VMEM((2, tn, k), BF16)
    else:
        in_specs.append(pl.BlockSpec((None, k // ni, tn), lambda j, i: (0, i, cur(j))))
        args.append(w)
        scratch = pltpu.VMEM((2, k, tn), BF16)
    for arr, off in zip(extra, extra_off):
        in_specs.append(pl.BlockSpec((tm, tn), lambda j, i, off=off: (row(j, i), off // tn + prev(j))))
        args.append(arr)
    out_specs = [pl.BlockSpec((tm, tn), lambda j, i: (row(j, i), prev(j)))]
    out_shape = [jax.ShapeDtypeStruct((m, n_out), out_dtype)]
    if side is not None:
        s_rows, s_cols = side.shape[1:]
        slab = s_rows // (nj * ni)

        def step(j, i):
            return jnp.maximum((j - 1) * ni + i, 0)

        in_specs.append(pl.BlockSpec((None, slab, s_cols), lambda j, i: (0, step(j, i), 0)))
        args.append(side)
        out_specs.append(pl.BlockSpec((slab, s_cols), lambda j, i: (step(j, i), 0)))
        out_shape.append(jax.ShapeDtypeStruct((s_rows, s_cols), BF16))
    out = pl.pallas_call(
        functools.partial(_stream_kernel, nj=nj, transposed=transposed, tile_of=tile_of, epilogue=epilogue,
                          n_extra=len(extra), has_next=transposed, has_side=side is not None),
        grid=(nj + 1, ni),
        in_specs=in_specs,
        out_specs=out_specs,
        out_shape=out_shape,
        scratch_shapes=[scratch],
        compiler_params=_params(2),
        name=name,
    )(*args)
    return out if side is not None else out[0]


def _gate_epilogue(acc, g):
    return _sigmoid(g.astype(F32)) * acc


def _gate_add_epilogue(acc, g, prev):
    return prev.astype(F32) + _sigmoid(g.astype(F32)) * acc


def _relu2_epilogue(acc):
    return jnp.square(jnp.maximum(acc, 0.0))


def _gate_table(g):
    length = g.shape[0]
    lf = _log_sigmoid(g)
    row = lax.broadcasted_iota(jnp.int32, g.shape, 0)
    col = lax.broadcasted_iota(jnp.int32, g.shape, 1)
    prefix = lf
    shift = 1
    while shift < length:
        prefix = prefix + jnp.where(row >= shift, pltpu.roll(prefix, shift, 0), 0.0)
        shift *= 2
    suffix = prefix[length - 1:length, :] - prefix + lf
    is_f_fwd = (col >= M_HEADS) & (col < 2 * M_HEADS)
    is_f_bwd = (col >= 3 * M_HEADS) & (col < 4 * M_HEADS)
    return jnp.where(is_f_fwd, prefix, jnp.where(is_f_bwd, suffix, g))


def _gate_proj_kernel(a_ref, w_ref, b_ref, o_ref):
    g = lax.dot_general(a_ref[...], w_ref[...], _NT, preferred_element_type=F32) + b_ref[...]
    for c in range(g.shape[0] // M_CHUNK):
        rows = slice(c * M_CHUNK, (c + 1) * M_CHUNK)
        o_ref[rows, :] = _gate_table(g[rows, :])


def _gate_proj(a, w_gate_t, bias, tm=1024, name="gate_proj"):
    m, k = a.shape
    n = w_gate_t.shape[0]
    return pl.pallas_call(
        _gate_proj_kernel,
        grid=(m // tm,),
        in_specs=[pl.BlockSpec((tm, k), lambda i: (i, 0)),
                  pl.BlockSpec((n, k), lambda i: (0, 0)),
                  pl.BlockSpec((1, n), lambda i: (0, 0))],
        out_specs=pl.BlockSpec((tm, n), lambda i: (i, 0)),
        out_shape=jax.ShapeDtypeStruct((m, n), F32),
        compiler_params=_params(1),
        name=name,
    )(a, w_gate_t, bias)


def _matmul_ksplit_kernel(a_ref, w_ref, o_ref, acc_ref):
    kk = pl.program_id(2)

    @pl.when(kk == 0)
    def _():
        acc_ref[...] = jnp.zeros_like(acc_ref)

    acc_ref[...] += jnp.dot(a_ref[...], w_ref[...], preferred_element_type=F32)

    @pl.when(kk == pl.num_programs(2) - 1)
    def _():
        o_ref[...] = acc_ref[...].astype(o_ref.dtype)


def _matmul_ksplit(a, w, *, tm, tn, tk, name):
    m, k = a.shape
    n = w.shape[1]
    return pl.pallas_call(
        _matmul_ksplit_kernel,
        grid=(n // tn, m // tm, k // tk),
        in_specs=[pl.BlockSpec((tm, tk), lambda j, i, kk: (i, kk)),
                  pl.BlockSpec((tk, tn), lambda j, i, kk: (kk, j))],
        out_specs=pl.BlockSpec((tm, tn), lambda j, i, kk: (i, j)),
        out_shape=jax.ShapeDtypeStruct((m, n), BF16),
        scratch_shapes=[pltpu.VMEM((tm, tn), F32)],
        compiler_params=_params(3),
        name=name,
    )(a, w)


def _tile_lanes(x, width):
    return jnp.concatenate([x] * (width // x.shape[1]), axis=1)


def _mlstm_chunk(q_ref, k_ref, v_ref, gc_ref, gr_ref, ct_ref, m_ref, h_ref, *, reverse):
    length = k_ref.shape[0]
    dv = v_ref.shape[1]
    k = k_ref[...]
    v_aug = jnp.concatenate([v_ref[...], jnp.ones((length, LANES), BF16)], axis=1)
    if h_ref is not None:
        q = q_ref[...]
        qk = lax.dot_general(q, k, _NT, preferred_element_type=F32)
        qc_aug = jnp.dot(q, ct_ref[...].astype(BF16), preferred_element_type=F32)
    yield

    ji, jf = (2, 3) if reverse else (0, 1)
    ig_c = jnp.broadcast_to(gc_ref[ji], (length, LANES))
    b_c = jnp.broadcast_to(gc_ref[jf], (length, LANES))
    gr = gr_ref[...]
    ig_r = gr[ji:ji + 1, :]
    b_r = gr[jf:jf + 1, :]
    b_tot = b_r[:, 0:1] if reverse else b_r[:, length - 1:length]
    m_prev = m_ref[...]
    w_c = b_tot - b_c + ig_c
    m_new = jnp.maximum(b_tot + m_prev, jnp.max(w_c, axis=0, keepdims=True)[:, 0:1])
    decay = jnp.exp(b_tot + m_prev - m_new)
    kw = k.astype(F32) * _tile_lanes(jnp.exp(w_c - m_new), k.shape[1])
    if h_ref is not None:
        t_idx = lax.broadcasted_iota(jnp.int32, (length, length), 0)
        s_idx = lax.broadcasted_iota(jnp.int32, (length, length), 1)
        seen = (s_idx >= t_idx) if reverse else (s_idx <= t_idx)
        a_c = b_c + m_prev
        dm = jnp.where(seen, _tile_lanes(b_c, length) - b_r + ig_r, -jnp.inf)
        m_t = jnp.maximum(a_c, jnp.broadcast_to(jnp.max(dm, axis=1, keepdims=True), (length, LANES)))
        s = qk * Q_SCALE * jnp.exp(dm - _tile_lanes(m_t, length))
        inter = jnp.exp(a_c - m_t) * Q_SCALE
    yield

    if h_ref is not None:
        sv_aug = jnp.dot(s.astype(BF16), v_aug, preferred_element_type=F32)
        den = inter * qc_aug[:, dv:] + sv_aug[:, dv:]
        scale = 1.0 / jnp.maximum(jnp.abs(den), jnp.exp(-m_t))
        num = _tile_lanes(inter, dv) * qc_aug[:, :dv] + sv_aug[:, :dv]
        h_ref[...] = (num * _tile_lanes(scale, dv)).astype(h_ref.dtype)
    ct_ref[...] = decay * ct_ref[...] + lax.dot_general(
        kw.astype(BF16), v_aug, (((0,), (0,)), ((), ())), preferred_element_type=F32)
    m_ref[...] = m_new
    yield


def _interleave(*chunks):
    for _ in range(3):
        for chunk in chunks:
            next(chunk)


def _mlstm_kernel(qf, kf, vf, gcf, grf, qb, kb, vb, gcb, grb, kc, vc, gcc, grc,
                  hf, hb, ctf, mf, ctb, mb):
    c = pl.program_id(2)

    @pl.when(c == 0)
    def _():
        for ct, m in ((ctf, mf), (ctb, mb)):
            ct[...] = jnp.zeros_like(ct)
            m[...] = jnp.zeros_like(m)
        _interleave(_mlstm_chunk(None, kc, vc, gcc, grc, ctf, mf, None, reverse=False),
                    _mlstm_chunk(None, kc, vc, gcc, grc, ctb, mb, None, reverse=True))

    @pl.when(c > 0)
    def _():
        _interleave(_mlstm_chunk(qf, kf, vf, gcf, grf, ctf, mf, hf, reverse=False),
                    _mlstm_chunk(qb, kb, vb, gcb, grb, ctb, mb, hb, reverse=True))


def _mlstm(proj, proj_ctx, gc, gr, gc_ctx, gr_ctx, batch):
    length = M_CHUNK
    assert CTX_LEN == length
    seq = proj.shape[0] // batch
    nc = seq // length

    def fwd(c):
        return jnp.maximum(c - 1, 0)

    def bwd(c):
        return nc - jnp.maximum(c, 1)

    def lat_specs(chunk_of):
        return [
            pl.BlockSpec((length, M_DQK), lambda b, h, c: (b * nc + chunk_of(c), OFF_QM // M_DQK + h)),
            pl.BlockSpec((length, M_DQK), lambda b, h, c: (b * nc + chunk_of(c), OFF_KM // M_DQK + h)),
            pl.BlockSpec((length, M_DV), lambda b, h, c: (b * nc + chunk_of(c), OFF_VM // M_DV + h)),
            pl.BlockSpec((None, None, 4, length, 1), lambda b, h, c: (b, h, 0, chunk_of(c), 0)),
            pl.BlockSpec((None, None, 4, length), lambda b, h, c: (b, h, 0, chunk_of(c))),
        ]

    ctx_specs = [
        pl.BlockSpec((length, M_DQK), lambda b, h, c: (b, CTX_OFF_KM // M_DQK + h)),
        pl.BlockSpec((length, M_DV), lambda b, h, c: (b, CTX_OFF_VM // M_DV + h)),
        pl.BlockSpec((None, None, 4, length, 1), lambda b, h, c: (b, h, 0, 0, 0)),
        pl.BlockSpec((None, None, 4, length), lambda b, h, c: (b, h, 0, 0)),
    ]
    out_shape = jax.ShapeDtypeStruct((batch * seq, M_V_W), BF16)
    state = [pltpu.VMEM((M_DQK, M_DV + LANES), F32), pltpu.VMEM((1, 1), F32)]
    return pl.pallas_call(
        _mlstm_kernel,
        grid=(batch, M_HEADS, nc + 1),
        in_specs=lat_specs(fwd) + lat_specs(bwd) + ctx_specs,
        out_specs=[pl.BlockSpec((length, M_DV), lambda b, h, c: (b * nc + fwd(c), h)),
                   pl.BlockSpec((length, M_DV), lambda b, h, c: (b * nc + bwd(c), h))],
        out_shape=[out_shape, out_shape],
        scratch_shapes=state + state,
        compiler_params=_params(3),
        name="mlstm",
    )(proj, proj, proj, gc, gr, proj, proj, proj, gc, gr, proj_ctx, proj_ctx, gc_ctx, gr_ctx)


def _readout_kernel(hf_ref, hb_ref, om_ref, g_ref, o_ref):
    h = hf_ref[...].astype(F32) + hb_ref[...].astype(F32)
    y = _rms(h) * g_ref[...] * _sigmoid(om_ref[...].astype(F32))
    o_ref[...] = y.astype(o_ref.dtype)


def _readout(hf, hb, proj, m_norm_g, tm=512):
    m = hf.shape[0]
    blk = pl.BlockSpec((tm, M_DV), lambda i, h: (i, h))
    return pl.pallas_call(
        _readout_kernel,
        grid=(m // tm, M_HEADS),
        in_specs=[blk, blk,
                  pl.BlockSpec((tm, M_DV), lambda i, h: (i, OFF_OM // M_DV + h)),
                  pl.BlockSpec((1, M_DV), lambda i, h: (0, h))],
        out_specs=blk,
        out_shape=jax.ShapeDtypeStruct((m, M_V_W), BF16),
        compiler_params=_params(2),
        name="mlstm_readout",
    )(hf, hb, proj, m_norm_g.reshape(1, M_V_W))


def _rope(x, cos, sin_signed):
    lane = lax.broadcasted_iota(jnp.int32, x.shape, 1)
    partner = jnp.where((lane % 64) < 32, pltpu.roll(x, LANES - 32, 1), pltpu.roll(x, 32, 1))
    return x * cos + partner * sin_signed


def _attn_kernel(sink_ref, q_ref, k_ref, v_ref, kc_ref, vc_ref, cos_ref, sin_ref, o_ref,
                 kpad, vpad, s_buf, p_buf, den_buf):
    seq = k_ref.shape[0]
    nb = seq // A_BLOCK
    kvh = pl.program_id(1)
    scale = A_HEAD_DIM ** -0.5
    rows = A_GROUP * A_BLOCK
    n_win = 3 * A_BLOCK
    zeros = jnp.zeros((A_BLOCK, A_HEAD_DIM), BF16)
    kpad[0:A_BLOCK, :] = zeros
    kpad[A_BLOCK + seq:, :] = zeros
    vpad[0:A_BLOCK, :] = zeros
    vpad[A_BLOCK + seq:, :] = zeros
    kpad[A_BLOCK:A_BLOCK + seq, :] = _rope(k_ref[...].astype(F32), cos_ref[...], sin_ref[...]).astype(BF16)
    vpad[A_BLOCK:A_BLOCK + seq, :] = v_ref[...]
    qi = lax.broadcasted_iota(jnp.int32, (rows, n_win), 0) & (A_BLOCK - 1)
    kj = lax.broadcasted_iota(jnp.int32, (rows, n_win), 1)
    band = (kj >= qi) & (kj <= qi + 2 * WINDOW)
    kj_row = lax.broadcasted_iota(jnp.int32, (1, n_win), 1)
    sink = jnp.concatenate(
        [jnp.full((A_BLOCK, 1), sink_ref[kvh * A_GROUP + g], F32) for g in range(A_GROUP)], axis=0)
    nt = (((1,), (1,)), ((), ()))

    def scores(n):
        r0 = pl.multiple_of(n * A_BLOCK, A_BLOCK)
        cos = cos_ref[pl.ds(r0, A_BLOCK), :]
        sin = sin_ref[pl.ds(r0, A_BLOCK), :]
        q = jnp.concatenate(
            [_rope(q_ref[pl.ds(r0, A_BLOCK), g * A_HEAD_DIM:(g + 1) * A_HEAD_DIM].astype(F32), cos, sin)
             for g in range(A_GROUP)], axis=0).astype(BF16)
        kpos = r0 - A_BLOCK + kj_row
        valid = band & ((kpos >= 0) & (kpos < seq))
        sw = lax.dot_general(q, kpad[pl.ds(r0, n_win), :], nt, preferred_element_type=F32) * scale
        s_buf[:, 0:n_win] = jnp.where(valid, sw, -jnp.inf)
        s_buf[:, n_win:] = lax.dot_general(q, kc_ref[...], nt, preferred_element_type=F32) * scale

    def softmax():
        s = s_buf[...]
        mx = jnp.maximum(jnp.max(s, axis=1, keepdims=True), sink)
        p = jnp.exp(s - mx)
        den_buf[...] = jnp.sum(p, axis=1, keepdims=True) + jnp.exp(sink - mx)
        p_buf[...] = p.astype(p_buf.dtype)

    def output(n):
        r0 = pl.multiple_of(n * A_BLOCK, A_BLOCK)
        o = (jnp.dot(p_buf[:, 0:n_win], vpad[pl.ds(r0, n_win), :], preferred_element_type=F32)
             + jnp.dot(p_buf[:, n_win:], vc_ref[...], preferred_element_type=F32)) * (1.0 / den_buf[...])
        for g in range(A_GROUP):
            o_ref[pl.ds(r0, A_BLOCK), g * A_HEAD_DIM:(g + 1) * A_HEAD_DIM] = (
                o[g * A_BLOCK:(g + 1) * A_BLOCK, :].astype(o_ref.dtype))

    scores(0)
    softmax()
    scores(1)

    def body(t, carry):
        output(t - 1)
        softmax()
        scores(t + 1)
        return carry

    lax.fori_loop(1, nb - 1, body, 0)
    output(nb - 2)
    softmax()
    output(nb - 1)


def _attention(proj, proj_ctx, sink, cos, sin, batch):
    seq = proj.shape[0] // batch
    qw = A_GROUP * A_HEAD_DIM
    tab = pl.BlockSpec((seq, A_HEAD_DIM), lambda b, h: (0, 0))
    return pl.pallas_call(
        _attn_kernel,
        grid=(batch, A_KV_HEADS),
        in_specs=[pl.BlockSpec(memory_space=pltpu.SMEM),
                  pl.BlockSpec((seq, qw), lambda b, h: (b, OFF_QA // qw + h)),
                  pl.BlockSpec((seq, A_HEAD_DIM), lambda b, h: (b, OFF_KA // A_HEAD_DIM + h)),
                  pl.BlockSpec((seq, A_HEAD_DIM), lambda b, h: (b, OFF_VA // A_HEAD_DIM + h)),
                  pl.BlockSpec((CTX_LEN, A_HEAD_DIM), lambda b, h: (b, CTX_OFF_KA // A_HEAD_DIM + h)),
                  pl.BlockSpec((CTX_LEN, A_HEAD_DIM), lambda b, h: (b, CTX_OFF_VA // A_HEAD_DIM + h)),
                  tab, tab],
        out_specs=pl.BlockSpec((seq, qw), lambda b, h: (b, h)),
        out_shape=jax.ShapeDtypeStruct((batch * seq, A_Q_W), BF16),
        scratch_shapes=[pltpu.VMEM((seq + 2 * A_BLOCK, A_HEAD_DIM), BF16),
                        pltpu.VMEM((seq + 2 * A_BLOCK, A_HEAD_DIM), BF16),
                        pltpu.VMEM((A_GROUP * A_BLOCK, 3 * A_BLOCK + CTX_LEN), F32),
                        pltpu.VMEM((A_GROUP * A_BLOCK, 3 * A_BLOCK + CTX_LEN), BF16),
                        pltpu.VMEM((A_GROUP * A_BLOCK, 1), F32)],
        compiler_params=_params(2),
        name="window_attention",
    )(sink, proj, proj, proj, proj_ctx, proj_ctx, cos, sin)


def _rope_tables(seq):
    t = jnp.arange(seq)
    nf = A_HEAD_DIM // 4
    freqs = ROPE_BASE ** (-jnp.arange(nf, dtype=F32) / nf)
    ang_r = (t // GRID_W).astype(F32)[:, None] * freqs
    ang_c = (t % GRID_W).astype(F32)[:, None] * freqs
    cos = jnp.concatenate([jnp.cos(ang_r), jnp.cos(ang_r), jnp.cos(ang_c), jnp.cos(ang_c)], axis=-1)
    sin = jnp.concatenate([-jnp.sin(ang_r), jnp.sin(ang_r), -jnp.sin(ang_c), jnp.sin(ang_c)], axis=-1)
    return cos, sin


def _gate_layouts(g, batch):
    gr = g[:, :N_GATES].reshape(batch, -1, 4, M_HEADS).transpose(0, 3, 2, 1)
    return gr[..., None], gr


def kernel(x, c, ctx, c_ctx, w_mod, b_mod, norm_g, w_in, m_gate_b, m_norm_g, attn_sink,
           w_out_m, w_out_a, w_o, w_ff1, w_ff2):
    batch, seq, d = x.shape
    n_ctx = ctx.shape[1]
    x2 = x.reshape(batch * seq, d)
    ctx2 = ctx.reshape(batch * n_ctx, d)

    w_in_t = w_in[0].T
    w_gate_t = jnp.pad(w_in_t[W_IN_GATE_LO:W_IN_GATE_HI], ((0, GATE_PAD - N_GATES), (0, 0))).astype(BF16)
    gate_bias = jnp.pad(m_gate_b[0].reshape(1, N_GATES), ((0, 0), (0, GATE_PAD - N_GATES)))

    c_all = jnp.zeros((MOD_ROWS, d), F32).at[:batch].set(c).at[batch].set(c_ctx)
    mod = _modulation(c_all, w_mod[0], b_mod[0]).reshape(MOD_ROWS, 6, 1, d)
    gn = norm_g[0].reshape(4, 1, d)

    u = _prenorm(x2, gn, mod, seq)
    uc = _prenorm(ctx2, gn, mod, n_ctx, fixed_row=batch)

    proj = _stream_matmul(u, w_in_t, n_out=PROJ_ALIGNED_W, out_dtype=BF16, transposed=True, name="in_proj")
    g_lat = _gate_proj(u, w_gate_t, gate_bias)
    ctx_tn = 512
    lat_tiles = (OFF_OM - OFF_KM) // ctx_tn
    proj_ctx = _stream_matmul(
        uc, w_in_t, n_out=CTX_PROJ_W, out_dtype=BF16, transposed=True, tn=ctx_tn, name="in_proj_ctx",
        tile_of=lambda j: jnp.where(j < lat_tiles, j + OFF_KM // ctx_tn, j - lat_tiles + OFF_KA // ctx_tn))
    g_ctx = _gate_proj(uc, w_gate_t, gate_bias, name="gate_proj_ctx")

    gc, gr = _gate_layouts(g_lat, batch)
    gc_ctx, gr_ctx = _gate_layouts(g_ctx, batch)
    hf, hb = _mlstm(proj, proj_ctx, gc, gr, gc_ctx, gr_ctx, batch)
    hm = _readout(hf, hb, proj, m_norm_g[0])

    cos, sin = _rope_tables(seq)
    at = _attention(proj, proj_ctx, attn_sink[0], cos, sin, batch)

    zm = _stream_matmul(hm, w_out_m, n_out=d, out_dtype=BF16, epilogue=_gate_epilogue,
                        extra=[proj], extra_off=[OFF_BGM], name="out_m")
    z = _stream_matmul(at, w_out_a, n_out=d, out_dtype=BF16, epilogue=_gate_add_epilogue,
                       extra=[proj, zm], extra_off=[OFF_BGA, 0], name="out_a_merge")
    mix = _stream_matmul(z, w_o, n_out=d, out_dtype=BF16, name="out_proj")
    h1, u2 = _post_mix(x2, mix, gn, mod, seq)

    hid, w_ff2_bf16 = _stream_matmul(u2, w_ff1, n_out=D_FF, out_dtype=BF16, epilogue=_relu2_epilogue,
                                     side=w_ff2, name="ff1")
    y = _matmul_ksplit(hid, w_ff2_bf16, tm=1024, tn=1024, tk=4096, name="ff2")
    out = _post_mlp(h1, y, gn, mod, seq)
    return out.reshape(batch, seq, d)
```

```python
import functools

import jax
import jax.numpy as jnp
from jax import lax
from jax.experimental import pallas as pl
from jax.experimental.pallas import tpu as pltpu

F32 = jnp.float32
BF16 = jnp.bfloat16

D_MODEL = 4096
GRID_W = 64
CTX_LEN = 256
M_HEADS = 8
M_DQK = 256
M_DV = 512
A_HEADS = 32
A_KV_HEADS = 8
A_HEAD_DIM = 128
A_GROUP = A_HEADS // A_KV_HEADS
WINDOW = 128
A_BLOCK = 128
ROPE_BASE = 10000.0
D_FF = 4 * D_MODEL
EPS = 1e-6

M_QK_W = M_HEADS * M_DQK
M_V_W = M_HEADS * M_DV
A_Q_W = A_HEADS * A_HEAD_DIM
A_KV_W = A_KV_HEADS * A_HEAD_DIM
N_GATES = 4 * M_HEADS

W_IN_GATE_LO = 2 * M_QK_W + 2 * M_V_W
W_IN_GATE_HI = W_IN_GATE_LO + N_GATES
OFF_QM = 0
OFF_KM = OFF_QM + M_QK_W
OFF_VM = OFF_KM + M_QK_W
OFF_OM = OFF_VM + M_V_W
OFF_QA = OFF_OM + M_V_W
OFF_KA = OFF_QA + A_Q_W
OFF_VA = OFF_KA + A_KV_W
OFF_BGM = OFF_VA + A_KV_W
OFF_BGA = OFF_BGM + D_MODEL
PROJ_ALIGNED_W = OFF_BGA + D_MODEL
CTX_OFF_KM = 0
CTX_OFF_VM = M_QK_W
CTX_OFF_KA = M_QK_W + M_V_W
CTX_OFF_VA = CTX_OFF_KA + A_KV_W
CTX_PROJ_W = CTX_OFF_VA + A_KV_W

M_CHUNK = 256
M_HEADS_PER_STEP = 2
Q_SCALE = M_DQK ** -0.5
LOG2_E = 1.4426950408889634
LANES = 128
GATE_PAD = LANES
MOD_ROWS = 8
VMEM_LIMIT = 60000 * 1024


def _params(n_axes):
    return pltpu.CompilerParams(dimension_semantics=("arbitrary",) * n_axes, vmem_limit_bytes=VMEM_LIMIT)


def _sigmoid(x):
    return 1.0 / (1.0 + jnp.exp(-x))


def _log_sigmoid(x):
    return jnp.minimum(x, 0.0) - jnp.log1p(jnp.exp(-jnp.abs(x)))


def _rms(x):
    return x * lax.rsqrt(jnp.mean(x * x, axis=-1, keepdims=True) + EPS)


def _mod_kernel(c_ref, w_ref, b_ref, o_ref):
    c = c_ref[...]
    sc = (c * _sigmoid(c)).astype(BF16)
    o_ref[...] = jnp.dot(sc, w_ref[...].astype(BF16), preferred_element_type=F32) + b_ref[...]


def _modulation(c_all, w_mod, b_mod, tn=1024):
    d, n = w_mod.shape
    return pl.pallas_call(
        _mod_kernel,
        grid=(n // tn,),
        in_specs=[pl.BlockSpec((MOD_ROWS, d), lambda j: (0, 0)),
                  pl.BlockSpec((d, tn), lambda j: (0, j)),
                  pl.BlockSpec((1, tn), lambda j: (0, j))],
        out_specs=pl.BlockSpec((MOD_ROWS, tn), lambda j: (0, j)),
        out_shape=jax.ShapeDtypeStruct((MOD_ROWS, n), F32),
        compiler_params=_params(1),
        name="modulation",
    )(c_all, w_mod, b_mod.reshape(1, n))


def _prenorm_kernel(x_ref, g_ref, sh_ref, sc_ref, o_ref):
    y = _rms(x_ref[...]) * g_ref[...]
    o_ref[...] = (y * (1.0 + sc_ref[...]) + sh_ref[...]).astype(o_ref.dtype)


def _mod_spec(slot, rows_per_mod_row, tm, fixed_row=None):
    if fixed_row is None:
        return pl.BlockSpec((None, None, 1, D_MODEL), lambda i: ((i * tm) // rows_per_mod_row, slot, 0, 0))
    return pl.BlockSpec((None, None, 1, D_MODEL), lambda i: (fixed_row, slot, 0, 0))


def _norm_spec(slot):
    return pl.BlockSpec((None, 1, D_MODEL), lambda i: (slot, 0, 0))


def _prenorm(x2, norm_g, mod, rows_per_mod_row, fixed_row=None, tm=256):
    m = x2.shape[0]
    return pl.pallas_call(
        _prenorm_kernel,
        grid=(m // tm,),
        in_specs=[pl.BlockSpec((tm, D_MODEL), lambda i: (i, 0)),
                  _norm_spec(0),
                  _mod_spec(0, rows_per_mod_row, tm, fixed_row),
                  _mod_spec(1, rows_per_mod_row, tm, fixed_row)],
        out_specs=pl.BlockSpec((tm, D_MODEL), lambda i: (i, 0)),
        out_shape=jax.ShapeDtypeStruct((m, D_MODEL), BF16),
        compiler_params=_params(1),
        name="prenorm",
    )(x2, norm_g, mod, mod)


def _post_mix_kernel(x_ref, mix_ref, gn1_ref, gn2_ref, g1_ref, sh2_ref, s2_ref, h1_ref, u2_ref):
    h1 = x_ref[...] + g1_ref[...] * (_rms(mix_ref[...].astype(F32)) * gn1_ref[...])
    h1_ref[...] = h1
    y2 = _rms(h1) * gn2_ref[...]
    u2_ref[...] = (y2 * (1.0 + s2_ref[...]) + sh2_ref[...]).astype(u2_ref.dtype)


def _post_mix(x2, mix, norm_g, mod, rows_per_mod_row, tm=256):
    m = x2.shape[0]
    row = pl.BlockSpec((tm, D_MODEL), lambda i: (i, 0))
    return pl.pallas_call(
        _post_mix_kernel,
        grid=(m // tm,),
        in_specs=[row, row, _norm_spec(1), _norm_spec(2),
                  _mod_spec(2, rows_per_mod_row, tm), _mod_spec(3, rows_per_mod_row, tm),
                  _mod_spec(4, rows_per_mod_row, tm)],
        out_specs=[row, row],
        out_shape=[jax.ShapeDtypeStruct((m, D_MODEL), F32), jax.ShapeDtypeStruct((m, D_MODEL), BF16)],
        compiler_params=_params(1),
        name="post_mix",
    )(x2, mix, norm_g, norm_g, mod, mod, mod)


def _post_mlp_kernel(h1_ref, y_ref, gn_ref, g2_ref, o_ref):
    o_ref[...] = h1_ref[...] + g2_ref[...] * (_rms(y_ref[...].astype(F32)) * gn_ref[...])


def _post_mlp(h1, y, norm_g, mod, rows_per_mod_row, tm=256):
    m = h1.shape[0]
    row = pl.BlockSpec((tm, D_MODEL), lambda i: (i, 0))
    return pl.pallas_call(
        _post_mlp_kernel,
        grid=(m // tm,),
        in_specs=[row, row, _norm_spec(3), _mod_spec(5, rows_per_mod_row, tm)],
        out_specs=row,
        out_shape=jax.ShapeDtypeStruct((m, D_MODEL), F32),
        compiler_params=_params(1),
        name="post_mlp",
    )(h1, y, norm_g, mod)


_NT = (((1,), (1,)), ((), ()))


def _stream_kernel(a_ref, slab_ref, *rest, nj, transposed, tile_of, epilogue, n_extra, has_next, has_side):
    j = pl.program_id(0)
    i = pl.program_id(1)
    rest = list(rest)
    next_ref = rest.pop(0) if has_next else None
    extra = [rest.pop(0) for _ in range(n_extra)]
    side_ref = rest.pop(0) if has_side else None
    o_ref = rest.pop(0)
    side_out_ref = rest.pop(0) if has_side else None
    w_scr = rest.pop(0)
    rows = slab_ref.shape[0]
    r0 = pl.multiple_of(i * rows, rows)

    @pl.when(j < nj)
    def _():
        dst = w_scr.at[j % 2]
        if has_next:
            tile_rows = w_scr.shape[1]
            shifted = tile_of(j) * tile_rows >= W_IN_GATE_LO

            @pl.when(jnp.logical_not(shifted))
            def _():
                dst[pl.ds(r0, rows), :] = slab_ref[...].astype(BF16)

            @pl.when(shifted)
            def _():
                dst[pl.ds(r0, rows - N_GATES), :] = slab_ref[N_GATES:, :].astype(BF16)
                dst[pl.ds(r0 + (rows - N_GATES), N_GATES), :] = next_ref[...].astype(BF16)
        else:
            dst[pl.ds(r0, rows), :] = slab_ref[...].astype(BF16)

    @pl.when(j > 0)
    def _():
        w = w_scr[(j - 1) % 2]
        if transposed:
            acc = lax.dot_general(a_ref[...], w, _NT, preferred_element_type=F32)
        else:
            acc = jnp.dot(a_ref[...], w, preferred_element_type=F32)
        o_ref[...] = epilogue(acc, *[e[...] for e in extra]).astype(o_ref.dtype)
        if has_side:
            side_out_ref[...] = side_ref[...].astype(side_out_ref.dtype)


def _stream_matmul(a, w, *, n_out, out_dtype, epilogue=lambda acc: acc, extra=(), extra_off=(), transposed=False,
                   tile_of=None, side=None, tm=1024, tn=1024, name):
    m, k = a.shape
    nj, ni = n_out // tn, m // tm
    tile_of = (lambda j: j) if tile_of is None else tile_of

    def prev(j):
        return jnp.maximum(j - 1, 0)

    def cur(j):
        return jnp.minimum(j, nj - 1)

    def row(j, i):
        return jnp.where(j == 0, 0, i)

    in_specs = [pl.BlockSpec((tm, k), lambda j, i: (row(j, i), 0))]
    args = [a]
    if transposed:
        slab_rows = tn // ni
        in_specs.append(pl.BlockSpec((slab_rows, k), lambda j, i: (tile_of(cur(j)) * ni + i, 0)))
        in_specs.append(pl.BlockSpec(
            (N_GATES, k), lambda j, i: ((tile_of(cur(j)) * tn + (i + 1) * slab_rows) // N_GATES, 0)))
        args += [w, w]
        scratch = pltpu.VMEM((2, tn, k), BF16)
    else:
        in_specs.append(pl.BlockSpec((None, k // ni, tn), lambda j, i: (0, i, cur(j))))
        args.append(w)
        scratch = pltpu.VMEM((2, k, tn), BF16)
    for arr, off in zip(extra, extra_off):
        in_specs.append(pl.BlockSpec((tm, tn), lambda j, i, off=off: (row(j, i), off // tn + prev(j))))
        args.append(arr)
    out_specs = [pl.BlockSpec((tm, tn), lambda j, i: (row(j, i), prev(j)))]
    out_shape = [jax.ShapeDtypeStruct((m, n_out), out_dtype)]
    if side is not None:
        s_rows, s_cols = side.shape[1:]
        slab = s_rows // (nj * ni)

        def step(j, i):
            return jnp.maximum((j - 1) * ni + i, 0)

        in_specs.append(pl.BlockSpec((None, slab, s_cols), lambda j, i: (0, step(j, i), 0)))
        args.append(side)
        out_specs.append(pl.BlockSpec((slab, s_cols), lambda j, i: (step(j, i), 0)))
        out_shape.append(jax.ShapeDtypeStruct((s_rows, s_cols), BF16))
    out = pl.pallas_call(
        functools.partial(_stream_kernel, nj=nj, transposed=transposed, tile_of=tile_of, epilogue=epilogue,
                          n_extra=len(extra), has_next=transposed, has_side=side is not None),
        grid=(nj + 1, ni),
        in_specs=in_specs,
        out_specs=out_specs,
        out_shape=out_shape,
        scratch_shapes=[scratch],
        compiler_params=_params(2),
        name=name,
    )(*args)
    return out if side is not None else out[0]


def _gate_epilogue(acc, g):
    return _sigmoid(g.astype(F32)) * acc


def _gate_add_epilogue(acc, g, prev):
    return prev.astype(F32) + _sigmoid(g.astype(F32)) * acc


def _relu2_epilogue(acc):
    return jnp.square(jnp.maximum(acc, 0.0))


def _gate_table(g):
    length = g.shape[0]
    lf = _log_sigmoid(g)
    row = lax.broadcasted_iota(jnp.int32, g.shape, 0)
    col = lax.broadcasted_iota(jnp.int32, g.shape, 1)
    prefix = lf
    shift = 1
    while shift < length:
        prefix = prefix + jnp.where(row >= shift, pltpu.roll(prefix, shift, 0), 0.0)
        shift *= 2
    suffix = prefix[length - 1:length, :] - prefix + lf
    is_f_fwd = (col >= M_HEADS) & (col < 2 * M_HEADS)
    is_f_bwd = (col >= 3 * M_HEADS) & (col < 4 * M_HEADS)
    return jnp.where(is_f_fwd, prefix, jnp.where(is_f_bwd, suffix, g))


def _gate_proj_kernel(a_ref, w_ref, b_ref, o_ref):
    g = lax.dot_general(a_ref[...], w_ref[...], _NT, preferred_element_type=F32) + b_ref[...]
    for c in range(g.shape[0] // M_CHUNK):
        rows = slice(c * M_CHUNK, (c + 1) * M_CHUNK)
        o_ref[rows, :] = _gate_table(g[rows, :])


def _gate_proj(a, w_gate_t, bias, tm=1024, name="gate_proj"):
    m, k = a.shape
    n = w_gate_t.shape[0]
    return pl.pallas_call(
        _gate_proj_kernel,
        grid=(m // tm,),
        in_specs=[pl.BlockSpec((tm, k), lambda i: (i, 0)),
                  pl.BlockSpec((n, k), lambda i: (0, 0)),
                  pl.BlockSpec((1, n), lambda i: (0, 0))],
        out_specs=pl.BlockSpec((tm, n), lambda i: (i, 0)),
        out_shape=jax.ShapeDtypeStruct((m, n), F32),
        compiler_params=_params(1),
        name=name,
    )(a, w_gate_t, bias)


def _matmul_ksplit_kernel(a_ref, w_ref, o_ref, acc_ref):
    kk = pl.program_id(2)

    @pl.when(kk == 0)
    def _():
        acc_ref[...] = jnp.zeros_like(acc_ref)

    acc_ref[...] += jnp.dot(a_ref[...], w_ref[...], preferred_element_type=F32)

    @pl.when(kk == pl.num_programs(2) - 1)
    def _():
        o_ref[...] = acc_ref[...].astype(o_ref.dtype)


def _matmul_ksplit(a, w, *, tm, tn, tk, name):
    m, k = a.shape
    n = w.shape[1]
    return pl.pallas_call(
        _matmul_ksplit_kernel,
        grid=(n // tn, m // tm, k // tk),
        in_specs=[pl.BlockSpec((tm, tk), lambda j, i, kk: (i, kk)),
                  pl.BlockSpec((tk, tn), lambda j, i, kk: (kk, j))],
        out_specs=pl.BlockSpec((tm, tn), lambda j, i, kk: (i, j)),
        out_shape=jax.ShapeDtypeStruct((m, n), BF16),
        scratch_shapes=[pltpu.VMEM((tm, tn), F32)],
        compiler_params=_params(3),
        name=name,
    )(a, w)


def _tile_lanes(x, width):
    return jnp.concatenate([x] * (width // x.shape[1]), axis=1)


def _rows_to_replicated_columns(x_row, y_row):
    length = x_row.shape[1]

    def pieces(v):
        hi = v.astype(BF16)
        rest = v - hi.astype(F32)
        mid = rest.astype(BF16)
        return [hi, mid, (rest - mid.astype(F32)).astype(BF16)]

    n_rows = 16
    stacked = jnp.concatenate(
        pieces(x_row) + pieces(y_row) + [jnp.zeros((n_rows - 6, length), BF16)], axis=0)
    piece = lax.broadcasted_iota(jnp.int32, (n_rows, 2 * LANES), 0)
    lane = lax.broadcasted_iota(jnp.int32, (n_rows, 2 * LANES), 1)
    selector = jnp.where((lane < LANES) == (piece < 3), 1.0, 0.0)
    selector = jnp.where(piece < 6, selector, 0.0).astype(BF16)
    both = lax.dot_general(stacked, selector, (((0,), (0,)), ((), ())), preferred_element_type=F32)
    return both[:, :LANES], both[:, LANES:]


def _mlstm_chunk(q_ref, k_ref, v_ref, gr_ref, ct_ref, m_ref, h_ref, *, reverse):
    length = k_ref.shape[0]
    dv = v_ref.shape[1]
    k = k_ref[...]
    v_aug = jnp.concatenate([v_ref[...], jnp.ones((length, LANES), BF16)], axis=1)
    if h_ref is not None:
        q = q_ref[...]
        qk = lax.dot_general(q, k, _NT, preferred_element_type=F32)
        qc_aug = jnp.dot(q, ct_ref[...].astype(BF16), preferred_element_type=F32)
    yield

    ji, jf = (2, 3) if reverse else (0, 1)
    gr = gr_ref[...]
    ig_r = gr[ji:ji + 1, :]
    b_r = gr[jf:jf + 1, :]
    b_c, ig_c = _rows_to_replicated_columns(b_r, ig_r)
    b_tot = b_r[:, 0:1] if reverse else b_r[:, length - 1:length]
    m_prev = m_ref[...]
    w_c = b_tot - b_c + ig_c
    m_new = jnp.maximum(b_tot + m_prev, jnp.max(w_c, axis=0, keepdims=True)[:, 0:1])
    decay = jnp.exp(b_tot + m_prev - m_new)
    kw = k.astype(F32) * _tile_lanes(jnp.exp(w_c - m_new), k.shape[1])
    if h_ref is not None:
        t_idx = lax.broadcasted_iota(jnp.int32, (length, length), 0)
        s_idx = lax.broadcasted_iota(jnp.int32, (length, length), 1)
        seen = (s_idx >= t_idx) if reverse else (s_idx <= t_idx)
        a_c = b_c + m_prev
        dm = jnp.where(seen, _tile_lanes(b_c, length) - b_r + ig_r, -jnp.inf)
        m_t = jnp.maximum(a_c, jnp.broadcast_to(jnp.max(dm, axis=1, keepdims=True), (length, LANES)))
        s = qk * Q_SCALE * jnp.exp(dm - _tile_lanes(m_t, length))
        inter = jnp.exp(a_c - m_t) * Q_SCALE
    yield

    if h_ref is not None:
        sv_aug = jnp.dot(s.astype(BF16), v_aug, preferred_element_type=F32)
        den = inter * qc_aug[:, dv:] + sv_aug[:, dv:]
        scale = 1.0 / jnp.maximum(jnp.abs(den), jnp.exp(-m_t))
        num = _tile_lanes(inter, dv) * qc_aug[:, :dv] + sv_aug[:, :dv]
        h_ref[...] = (num * _tile_lanes(scale, dv)).astype(h_ref.dtype)
    ct_ref[...] = decay * ct_ref[...] + lax.dot_general(
        kw.astype(BF16), v_aug, (((0,), (0,)), ((), ())), preferred_element_type=F32)
    m_ref[...] = m_new
    yield


def _interleave(*chunks):
    for _ in range(3):
        for chunk in chunks:
            next(chunk)


def _mlstm_kernel(qf, kf, vf, grf, qb, kb, vb, grb, kc, vc, grc, hf, hb, *state):
    c = pl.program_id(2)
    fwd_state = [state[2 * i:2 * i + 2] for i in range(M_HEADS_PER_STEP)]
    bwd_state = [state[2 * (M_HEADS_PER_STEP + i):2 * (M_HEADS_PER_STEP + i) + 2] for i in range(M_HEADS_PER_STEP)]

    def head(ref, i, width):
        return ref.at[:, i * width:(i + 1) * width]

    @pl.when(c == 0)
    def _():
        for ref in state:
            ref[...] = jnp.zeros_like(ref)
        chunks = []
        for i in range(M_HEADS_PER_STEP):
            k, v, g = head(kc, i, M_DQK), head(vc, i, M_DV), grc.at[i]
            chunks.append(_mlstm_chunk(None, k, v, g, *fwd_state[i], None, reverse=False))
            chunks.append(_mlstm_chunk(None, k, v, g, *bwd_state[i], None, reverse=True))
        _interleave(*chunks)

    @pl.when(c > 0)
    def _():
        chunks = []
        for i in range(M_HEADS_PER_STEP):
            chunks.append(_mlstm_chunk(head(qf, i, M_DQK), head(kf, i, M_DQK), head(vf, i, M_DV), grf.at[i],
                                       *fwd_state[i], head(hf, i, M_DV), reverse=False))
            chunks.append(_mlstm_chunk(head(qb, i, M_DQK), head(kb, i, M_DQK), head(vb, i, M_DV), grb.at[i],
                                       *bwd_state[i], head(hb, i, M_DV), reverse=True))
        _interleave(*chunks)


def _mlstm(proj, proj_ctx, gr, gr_ctx, batch):
    length = M_CHUNK
    assert CTX_LEN == length
    seq = proj.shape[0] // batch
    nc = seq // length
    hps = M_HEADS_PER_STEP
    qk_w, v_w = hps * M_DQK, hps * M_DV

    def fwd(c):
        return jnp.maximum(c - 1, 0)

    def bwd(c):
        return nc - jnp.maximum(c, 1)

    def lat_specs(chunk_of):
        return [
            pl.BlockSpec((length, qk_w), lambda b, h, c: (b * nc + chunk_of(c), OFF_QM // qk_w + h)),
            pl.BlockSpec((length, qk_w), lambda b, h, c: (b * nc + chunk_of(c), OFF_KM // qk_w + h)),
            pl.BlockSpec((length, v_w), lambda b, h, c: (b * nc + chunk_of(c), OFF_VM // v_w + h)),
            pl.BlockSpec((None, hps, 4, length), lambda b, h, c: (b, h, 0, chunk_of(c))),
        ]

    ctx_specs = [
        pl.BlockSpec((length, qk_w), lambda b, h, c: (b, CTX_OFF_KM // qk_w + h)),
        pl.BlockSpec((length, v_w), lambda b, h, c: (b, CTX_OFF_VM // v_w + h)),
        pl.BlockSpec((None, hps, 4, length), lambda b, h, c: (b, h, 0, 0)),
    ]
    out_shape = jax.ShapeDtypeStruct((batch * seq, M_V_W), BF16)
    state = [pltpu.VMEM((M_DQK, M_DV + LANES), F32), pltpu.VMEM((1, 1), F32)]
    return pl.pallas_call(
        _mlstm_kernel,
        grid=(batch, M_HEADS // hps, nc + 1),
        in_specs=lat_specs(fwd) + lat_specs(bwd) + ctx_specs,
        out_specs=[pl.BlockSpec((length, v_w), lambda b, h, c: (b * nc + fwd(c), h)),
                   pl.BlockSpec((length, v_w), lambda b, h, c: (b * nc + bwd(c), h))],
        out_shape=[out_shape, out_shape],
        scratch_shapes=state * (2 * hps),
        compiler_params=_params(3),
        name="mlstm",
    )(proj, proj, proj, gr, proj, proj, proj, gr, proj_ctx, proj_ctx, gr_ctx)


def _readout_kernel(hf_ref, hb_ref, om_ref, g_ref, o_ref):
    h = hf_ref[...].astype(F32) + hb_ref[...].astype(F32)
    y = _rms(h) * g_ref[...] * _sigmoid(om_ref[...].astype(F32))
    o_ref[...] = y.astype(o_ref.dtype)


def _readout(hf, hb, proj, m_norm_g, tm=512):
    m = hf.shape[0]
    blk = pl.BlockSpec((tm, M_DV), lambda i, h: (i, h))
    return pl.pallas_call(
        _readout_kernel,
        grid=(m // tm, M_HEADS),
        in_specs=[blk, blk,
                  pl.BlockSpec((tm, M_DV), lambda i, h: (i, OFF_OM // M_DV + h)),
                  pl.BlockSpec((1, M_DV), lambda i, h: (0, h))],
        out_specs=blk,
        out_shape=jax.ShapeDtypeStruct((m, M_V_W), BF16),
        compiler_params=_params(2),
        name="mlstm_readout",
    )(hf, hb, proj, m_norm_g.reshape(1, M_V_W))


def _rope(x, cos, sin_signed):
    lane = lax.broadcasted_iota(jnp.int32, x.shape, 1)
    partner = jnp.where((lane % 64) < 32, pltpu.roll(x, LANES - 32, 1), pltpu.roll(x, 32, 1))
    return x * cos + partner * sin_signed


def _attn_kernel(sink_ref, q_ref, k_ref, v_ref, kc_ref, vc_ref, cos_ref, sin_ref, o_ref,
                 kpad, vpad, s_buf, p_buf, den_buf):
    seq = k_ref.shape[0]
    nb = seq // A_BLOCK
    kvh = pl.program_id(1)
    qscale = (A_HEAD_DIM ** -0.5) * LOG2_E
    rows = A_GROUP * A_BLOCK
    n_win = 3 * A_BLOCK
    zeros = jnp.zeros((A_BLOCK, A_HEAD_DIM), BF16)
    kpad[0:A_BLOCK, :] = zeros
    kpad[A_BLOCK + seq:, :] = zeros
    vpad[0:A_BLOCK, :] = zeros
    vpad[A_BLOCK + seq:, :] = zeros
    kpad[A_BLOCK:A_BLOCK + seq, :] = _rope(k_ref[...].astype(F32), cos_ref[...], sin_ref[...]).astype(BF16)
    vpad[A_BLOCK:A_BLOCK + seq, :] = v_ref[...]
    qi = lax.broadcasted_iota(jnp.int32, (rows, n_win), 0) & (A_BLOCK - 1)
    kj = lax.broadcasted_iota(jnp.int32, (rows, n_win), 1)
    band = (kj >= qi) & (kj <= qi + 2 * WINDOW)
    kj_row = lax.broadcasted_iota(jnp.int32, (1, n_win), 1)
    sink = jnp.concatenate(
        [jnp.full((A_BLOCK, 1), sink_ref[kvh * A_GROUP + g] * LOG2_E, F32) for g in range(A_GROUP)], axis=0)
    nt = (((1,), (1,)), ((), ()))

    def scores(n):
        r0 = pl.multiple_of(n * A_BLOCK, A_BLOCK)
        cos = cos_ref[pl.ds(r0, A_BLOCK), :] * qscale
        sin = sin_ref[pl.ds(r0, A_BLOCK), :] * qscale
        q = jnp.concatenate(
            [_rope(q_ref[pl.ds(r0, A_BLOCK), g * A_HEAD_DIM:(g + 1) * A_HEAD_DIM].astype(F32), cos, sin)
             for g in range(A_GROUP)], axis=0).astype(BF16)
        kpos = r0 - A_BLOCK + kj_row
        valid = band & ((kpos >= 0) & (kpos < seq))
        sw = lax.dot_general(q, kpad[pl.ds(r0, n_win), :], nt, preferred_element_type=F32)
        s_buf[:, 0:n_win] = jnp.where(valid, sw, -jnp.inf)
        s_buf[:, n_win:] = lax.dot_general(q, kc_ref[...], nt, preferred_element_type=F32)

    def softmax():
        s = s_buf[...]
        mx = jnp.maximum(jnp.max(s, axis=1, keepdims=True), sink)
        p = jnp.exp2(s - mx)
        den_buf[...] = jnp.sum(p, axis=1, keepdims=True) + jnp.exp2(sink - mx)
        p_buf[...] = p.astype(p_buf.dtype)

    def output(n):
        r0 = pl.multiple_of(n * A_BLOCK, A_BLOCK)
        o = (jnp.dot(p_buf[:, 0:n_win], vpad[pl.ds(r0, n_win), :], preferred_element_type=F32)
             + jnp.dot(p_buf[:, n_win:], vc_ref[...], preferred_element_type=F32)) * (1.0 / den_buf[...])
        for g in range(A_GROUP):
            o_ref[pl.ds(r0, A_BLOCK), g * A_HEAD_DIM:(g + 1) * A_HEAD_DIM] = (
                o[g * A_BLOCK:(g + 1) * A_BLOCK, :].astype(o_ref.dtype))

    scores(0)
    softmax()
    scores(1)

    def body(t, carry):
        output(t - 1)
        softmax()
        scores(t + 1)
        return carry

    lax.fori_loop(1, nb - 1, body, 0)
    output(nb - 2)
    softmax()
    output(nb - 1)


def _attention(proj, proj_ctx, sink, cos, sin, batch):
    seq = proj.shape[0] // batch
    qw = A_GROUP * A_HEAD_DIM
    tab = pl.BlockSpec((seq, A_HEAD_DIM), lambda b, h: (0, 0))
    return pl.pallas_call(
        _attn_kernel,
        grid=(batch, A_KV_HEADS),
        in_specs=[pl.BlockSpec(memory_space=pltpu.SMEM),
                  pl.BlockSpec((seq, qw), lambda b, h: (b, OFF_QA // qw + h)),
                  pl.BlockSpec((seq, A_HEAD_DIM), lambda b, h: (b, OFF_KA // A_HEAD_DIM + h)),
                  pl.BlockSpec((seq, A_HEAD_DIM), lambda b, h: (b, OFF_VA // A_HEAD_DIM + h)),
                  pl.BlockSpec((CTX_LEN, A_HEAD_DIM), lambda b, h: (b, CTX_OFF_KA // A_HEAD_DIM + h)),
                  pl.BlockSpec((CTX_LEN, A_HEAD_DIM), lambda b, h: (b, CTX_OFF_VA // A_HEAD_DIM + h)),
                  tab, tab],
        out_specs=pl.BlockSpec((seq, qw), lambda b, h: (b, h)),
        out_shape=jax.ShapeDtypeStruct((batch * seq, A_Q_W), BF16),
        scratch_shapes=[pltpu.VMEM((seq + 2 * A_BLOCK, A_HEAD_DIM), BF16),
                        pltpu.VMEM((seq + 2 * A_BLOCK, A_HEAD_DIM), BF16),
                        pltpu.VMEM((A_GROUP * A_BLOCK, 3 * A_BLOCK + CTX_LEN), F32),
                        pltpu.VMEM((A_GROUP * A_BLOCK, 3 * A_BLOCK + CTX_LEN), BF16),
                        pltpu.VMEM((A_GROUP * A_BLOCK, 1), F32)],
        compiler_params=_params(2),
        name="window_attention",
    )(sink, proj, proj, proj, proj_ctx, proj_ctx, cos, sin)


def _rope_tables(seq):
    t = jnp.arange(seq)
    nf = A_HEAD_DIM // 4
    freqs = ROPE_BASE ** (-jnp.arange(nf, dtype=F32) / nf)
    ang_r = (t // GRID_W).astype(F32)[:, None] * freqs
    ang_c = (t % GRID_W).astype(F32)[:, None] * freqs
    cos = jnp.concatenate([jnp.cos(ang_r), jnp.cos(ang_r), jnp.cos(ang_c), jnp.cos(ang_c)], axis=-1)
    sin = jnp.concatenate([-jnp.sin(ang_r), jnp.sin(ang_r), -jnp.sin(ang_c), jnp.sin(ang_c)], axis=-1)
    return cos, sin


def _gate_rows(g, batch):
    return g[:, :N_GATES].reshape(batch, -1, 4, M_HEADS).transpose(0, 3, 2, 1)


def kernel(x, c, ctx, c_ctx, w_mod, b_mod, norm_g, w_in, m_gate_b, m_norm_g, attn_sink,
           w_out_m, w_out_a, w_o, w_ff1, w_ff2):
    batch, seq, d = x.shape
    n_ctx = ctx.shape[1]
    x2 = x.reshape(batch * seq, d)
    ctx2 = ctx.reshape(batch * n_ctx, d)

    w_in_t = w_in[0].T
    w_gate_t = jnp.pad(w_in_t[W_IN_GATE_LO:W_IN_GATE_HI], ((0, GATE_PAD - N_GATES), (0, 0))).astype(BF16)
    gate_bias = jnp.pad(m_gate_b[0].reshape(1, N_GATES), ((0, 0), (0, GATE_PAD - N_GATES)))

    c_all = jnp.zeros((MOD_ROWS, d), F32).at[:batch].set(c).at[batch].set(c_ctx)
    mod = _modulation(c_all, w_mod[0], b_mod[0]).reshape(MOD_ROWS, 6, 1, d)
    gn = norm_g[0].reshape(4, 1, d)

    u = _prenorm(x2, gn, mod, seq)
    uc = _prenorm(ctx2, gn, mod, n_ctx, fixed_row=batch)

    proj = _stream_matmul(u, w_in_t, n_out=PROJ_ALIGNED_W, out_dtype=BF16, transposed=True, name="in_proj")
    g_lat = _gate_proj(u, w_gate_t, gate_bias)
    ctx_tn = 512
    lat_tiles = (OFF_OM - OFF_KM) // ctx_tn
    proj_ctx = _stream_matmul(
        uc, w_in_t, n_out=CTX_PROJ_W, out_dtype=BF16, transposed=True, tn=ctx_tn, name="in_proj_ctx",
        tile_of=lambda j: jnp.where(j < lat_tiles, j + OFF_KM // ctx_tn, j - lat_tiles + OFF_KA // ctx_tn))
    g_ctx = _gate_proj(uc, w_gate_t, gate_bias, name="gate_proj_ctx")

    hf, hb = _mlstm(proj, proj_ctx, _gate_rows(g_lat, batch), _gate_rows(g_ctx, batch), batch)
    hm = _readout(hf, hb, proj, m_norm_g[0])

    cos, sin = _rope_tables(seq)
    at = _attention(proj, proj_ctx, attn_sink[0], cos, sin, batch)

    zm = _stream_matmul(hm, w_out_m, n_out=d, out_dtype=BF16, epilogue=_gate_epilogue,
                        extra=[proj], extra_off=[OFF_BGM], name="out_m")
    z = _stream_matmul(at, w_out_a, n_out=d, out_dtype=BF16, epilogue=_gate_add_epilogue,
                       extra=[proj, zm], extra_off=[OFF_BGA, 0], name="out_a_merge")
    mix = _stream_matmul(z, w_o, n_out=d, out_dtype=BF16, name="out_proj")
    h1, u2 = _post_mix(x2, mix, gn, mod, seq)

    hid, w_ff2_bf16 = _stream_matmul(u2, w_ff1, n_out=D_FF, out_dtype=BF16, epilogue=_relu2_epilogue,
                                     side=w_ff2, name="ff1")
    y = _matmul_ksplit(hid, w_ff2_bf16, tm=1024, tn=1024, tk=4096, name="ff2")
    out = _post_mlp(h1, y, gn, mod, seq)
    return out.reshape(batch, seq, d)
```

```python
import functools

import jax
import jax.numpy as jnp
from jax import lax
from jax.experimental import pallas as pl
from jax.experimental.pallas import tpu as pltpu

F32 = jnp.float32
BF16 = jnp.bfloat16

D_MODEL = 4096
GRID_W = 64
CTX_LEN = 256
M_HEADS = 8
M_DQK = 256
M_DV = 512
A_HEADS = 32
A_KV_HEADS = 8
A_HEAD_DIM = 128
A_GROUP = A_HEADS // A_KV_HEADS
WINDOW = 128
A_BLOCK = 128
ROPE_BASE = 10000.0
D_FF = 4 * D_MODEL
EPS = 1e-6

M_QK_W = M_HEADS * M_DQK
M_V_W = M_HEADS * M_DV
A_Q_W = A_HEADS * A_HEAD_DIM
A_KV_W = A_KV_HEADS * A_HEAD_DIM
N_GATES = 4 * M_HEADS

W_IN_GATE_LO = 2 * M_QK_W + 2 * M_V_W
W_IN_GATE_HI = W_IN_GATE_LO + N_GATES
OFF_QM = 0
OFF_KM = OFF_QM + M_QK_W
OFF_VM = OFF_KM + M_QK_W
OFF_OM = OFF_VM + M_V_W
OFF_QA = OFF_OM + M_V_W
OFF_KA = OFF_QA + A_Q_W
OFF_VA = OFF_KA + A_KV_W
OFF_BGM = OFF_VA + A_KV_W
OFF_BGA = OFF_BGM + D_MODEL
PROJ_ALIGNED_W = OFF_BGA + D_MODEL
CTX_OFF_KM = 0
CTX_OFF_VM = M_QK_W
CTX_OFF_KA = M_QK_W + M_V_W
CTX_OFF_VA = CTX_OFF_KA + A_KV_W
CTX_PROJ_W = CTX_OFF_VA + A_KV_W

M_CHUNK = 256
M_HEADS_PER_STEP = 2
Q_SCALE = M_DQK ** -0.5
LOG2_E = 1.4426950408889634
LANES = 128
GATE_PAD = LANES
MOD_ROWS = 8
VMEM_LIMIT = 60000 * 1024


def _params(n_axes):
    return pltpu.CompilerParams(dimension_semantics=("arbitrary",) * n_axes, vmem_limit_bytes=VMEM_LIMIT)


def _sigmoid(x):
    return 1.0 / (1.0 + jnp.exp(-x))


def _log_sigmoid(x):
    return jnp.minimum(x, 0.0) - jnp.log1p(jnp.exp(-jnp.abs(x)))


def _rms(x):
    return x * lax.rsqrt(jnp.mean(x * x, axis=-1, keepdims=True) + EPS)


def _mod_kernel(c_ref, w_ref, b_ref, o_ref):
    c = c_ref[...]
    sc = (c * _sigmoid(c)).astype(BF16)
    o_ref[...] = jnp.dot(sc, w_ref[...].astype(BF16), preferred_element_type=F32) + b_ref[...]


def _modulation(c_all, w_mod, b_mod, tn=1024):
    d, n = w_mod.shape
    return pl.pallas_call(
        _mod_kernel,
        grid=(n // tn,),
        in_specs=[pl.BlockSpec((MOD_ROWS, d), lambda j: (0, 0)),
                  pl.BlockSpec((d, tn), lambda j: (0, j)),
                  pl.BlockSpec((1, tn), lambda j: (0, j))],
        out_specs=pl.BlockSpec((MOD_ROWS, tn), lambda j: (0, j)),
        out_shape=jax.ShapeDtypeStruct((MOD_ROWS, n), F32),
        compiler_params=_params(1),
        name="modulation",
    )(c_all, w_mod, b_mod.reshape(1, n))


def _prenorm_kernel(x_ref, g_ref, sh_ref, sc_ref, o_ref):
    y = _rms(x_ref[...]) * g_ref[...]
    o_ref[...] = (y * (1.0 + sc_ref[...]) + sh_ref[...]).astype(o_ref.dtype)


def _mod_spec(slot, rows_per_mod_row, tm, fixed_row=None):
    if fixed_row is None:
        return pl.BlockSpec((None, None, 1, D_MODEL), lambda i: ((i * tm) // rows_per_mod_row, slot, 0, 0))
    return pl.BlockSpec((None, None, 1, D_MODEL), lambda i: (fixed_row, slot, 0, 0))


def _norm_spec(slot):
    return pl.BlockSpec((None, 1, D_MODEL), lambda i: (slot, 0, 0))


def _prenorm(x2, norm_g, mod, rows_per_mod_row, fixed_row=None, tm=256):
    m = x2.shape[0]
    return pl.pallas_call(
        _prenorm_kernel,
        grid=(m // tm,),
        in_specs=[pl.BlockSpec((tm, D_MODEL), lambda i: (i, 0)),
                  _norm_spec(0),
                  _mod_spec(0, rows_per_mod_row, tm, fixed_row),
                  _mod_spec(1, rows_per_mod_row, tm, fixed_row)],
        out_specs=pl.BlockSpec((tm, D_MODEL), lambda i: (i, 0)),
        out_shape=jax.ShapeDtypeStruct((m, D_MODEL), BF16),
        compiler_params=_params(1),
        name="prenorm",
    )(x2, norm_g, mod, mod)


def _post_mix_kernel(x_ref, mix_ref, gn1_ref, gn2_ref, g1_ref, sh2_ref, s2_ref, h1_ref, u2_ref):
    h1 = x_ref[...] + g1_ref[...] * (_rms(mix_ref[...].astype(F32)) * gn1_ref[...])
    h1_ref[...] = h1
    y2 = _rms(h1) * gn2_ref[...]
    u2_ref[...] = (y2 * (1.0 + s2_ref[...]) + sh2_ref[...]).astype(u2_ref.dtype)


def _post_mix(x2, mix, norm_g, mod, rows_per_mod_row, tm=256):
    m = x2.shape[0]
    row = pl.BlockSpec((tm, D_MODEL), lambda i: (i, 0))
    return pl.pallas_call(
        _post_mix_kernel,
        grid=(m // tm,),
        in_specs=[row, row, _norm_spec(1), _norm_spec(2),
                  _mod_spec(2, rows_per_mod_row, tm), _mod_spec(3, rows_per_mod_row, tm),
                  _mod_spec(4, rows_per_mod_row, tm)],
        out_specs=[row, row],
        out_shape=[jax.ShapeDtypeStruct((m, D_MODEL), F32), jax.ShapeDtypeStruct((m, D_MODEL), BF16)],
        compiler_params=_params(1),
        name="post_mix",
    )(x2, mix, norm_g, norm_g, mod, mod, mod)


def _post_mlp_kernel(h1_ref, y_ref, gn_ref, g2_ref, o_ref):
    o_ref[...] = h1_ref[...] + g2_ref[...] * (_rms(y_ref[...].astype(F32)) * gn_ref[...])


def _post_mlp(h1, y, norm_g, mod, rows_per_mod_row, tm=256):
    m = h1.shape[0]
    row = pl.BlockSpec((tm, D_MODEL), lambda i: (i, 0))
    return pl.pallas_call(
        _post_mlp_kernel,
        grid=(m // tm,),
        in_specs=[row, row, _norm_spec(3), _mod_spec(5, rows_per_mod_row, tm)],
        out_specs=row,
        out_shape=jax.ShapeDtypeStruct((m, D_MODEL), F32),
        compiler_params=_params(1),
        name="post_mlp",
    )(h1, y, norm_g, mod)


_NT = (((1,), (1,)), ((), ()))


def _stream_kernel(a_ref, slab_ref, *rest, nj, transposed, tile_of, epilogue, n_extra, has_next, has_side):
    j = pl.program_id(0)
    i = pl.program_id(1)
    rest = list(rest)
    next_ref = rest.pop(0) if has_next else None
    extra = [rest.pop(0) for _ in range(n_extra)]
    side_ref = rest.pop(0) if has_side else None
    o_ref = rest.pop(0)
    side_out_ref = rest.pop(0) if has_side else None
    w_scr = rest.pop(0)
    rows = slab_ref.shape[0]
    r0 = pl.multiple_of(i * rows, rows)

    def cast_slab():
        dst = w_scr.at[j % 2]
        if has_next:
            tile_rows = w_scr.shape[1]
            shifted = tile_of(jnp.minimum(j, nj - 1)) * tile_rows >= W_IN_GATE_LO
            body = rows - N_GATES
            start = pl.multiple_of(jnp.where(shifted, N_GATES, 0), N_GATES)
            dst[pl.ds(r0, body), :] = slab_ref[pl.ds(start, body), :].astype(BF16)
            tail = jnp.where(shifted, next_ref[...], slab_ref[body:, :])
            dst[pl.ds(r0 + body, N_GATES), :] = tail.astype(BF16)
        else:
            dst[pl.ds(r0, rows), :] = slab_ref[...].astype(BF16)

    @pl.when(j == 0)
    def _():
        cast_slab()

    @pl.when(j > 0)
    def _():
        cast_slab()
        w = w_scr[(j - 1) % 2]
        if transposed:
            acc = lax.dot_general(a_ref[...], w, _NT, preferred_element_type=F32)
        else:
            acc = jnp.dot(a_ref[...], w, preferred_element_type=F32)
        o_ref[...] = epilogue(acc, *[e[...] for e in extra]).astype(o_ref.dtype)
        if has_side:
            side_out_ref[...] = side_ref[...].astype(side_out_ref.dtype)


def _stream_matmul(a, w, *, n_out, out_dtype, epilogue=lambda acc: acc, extra=(), extra_off=(), transposed=False,
                   tile_of=None, side=None, tm=1024, tn=1024, name):
    m, k = a.shape
    nj, ni = n_out // tn, m // tm
    tile_of = (lambda j: j) if tile_of is None else tile_of

    def prev(j):
        return jnp.maximum(j - 1, 0)

    def cur(j):
        return jnp.minimum(j, nj - 1)

    def row(j, i):
        return jnp.where(j == 0, 0, i)

    in_specs = [pl.BlockSpec((tm, k), lambda j, i: (row(j, i), 0))]
    args = [a]
    if transposed:
        slab_rows = tn // ni
        in_specs.append(pl.BlockSpec((slab_rows, k), lambda j, i: (tile_of(cur(j)) * ni + i, 0)))
        in_specs.append(pl.BlockSpec(
            (N_GATES, k), lambda j, i: ((tile_of(cur(j)) * tn + (i + 1) * slab_rows) // N_GATES, 0)))
        args += [w, w]
        scratch = pltpu.VMEM((2, tn, k), BF16)
    else:
        in_specs.append(pl.BlockSpec((None, k // ni, tn), lambda j, i: (0, i, cur(j))))
        args.append(w)
        scratch = pltpu.VMEM((2, k, tn), BF16)
    for arr, off in zip(extra, extra_off):
        in_specs.append(pl.BlockSpec((tm, tn), lambda j, i, off=off: (row(j, i), off // tn + prev(j))))
        args.append(arr)
    out_specs = [pl.BlockSpec((tm, tn), lambda j, i: (row(j, i), prev(j)))]
    out_shape = [jax.ShapeDtypeStruct((m, n_out), out_dtype)]
    if side is not None:
        s_rows, s_cols = side.shape[1:]
        slab = s_rows // (nj * ni)

        def step(j, i):
            return jnp.maximum((j - 1) * ni + i, 0)

        in_specs.append(pl.BlockSpec((None, slab, s_cols), lambda j, i: (0, step(j, i), 0)))
        args.append(side)
        out_specs.append(pl.BlockSpec((slab, s_cols), lambda j, i: (step(j, i), 0)))
        out_shape.append(jax.ShapeDtypeStruct((s_rows, s_cols), BF16))
    out = pl.pallas_call(
        functools.partial(_stream_kernel, nj=nj, transposed=transposed, tile_of=tile_of, epilogue=epilogue,
                          n_extra=len(extra), has_next=transposed, has_side=side is not None),
        grid=(nj + 1, ni),
        in_specs=in_specs,
        out_specs=out_specs,
        out_shape=out_shape,
        scratch_shapes=[scratch],
        compiler_params=_params(2),
        name=name,
    )(*args)
    return out if side is not None else out[0]


def _gate_epilogue(acc, g):
    return _sigmoid(g.astype(F32)) * acc


def _gate_add_epilogue(acc, g, prev):
    return prev.astype(F32) + _sigmoid(g.astype(F32)) * acc


def _relu2_epilogue(acc):
    return jnp.square(jnp.maximum(acc, 0.0))


def _gate_table(g):
    length = g.shape[0]
    lf = _log_sigmoid(g)
    row = lax.broadcasted_iota(jnp.int32, g.shape, 0)
    col = lax.broadcasted_iota(jnp.int32, g.shape, 1)
    prefix = lf
    shift = 1
    while shift < length:
        prefix = prefix + jnp.where(row >= shift, pltpu.roll(prefix, shift, 0), 0.0)
        shift *= 2
    suffix = prefix[length - 1:length, :] - prefix + lf
    is_f_fwd = (col >= M_HEADS) & (col < 2 * M_HEADS)
    is_f_bwd = (col >= 3 * M_HEADS) & (col < 4 * M_HEADS)
    return jnp.where(is_f_fwd, prefix, jnp.where(is_f_bwd, suffix, g))


def _gate_proj_kernel(a_ref, w_ref, b_ref, o_ref):
    g = lax.dot_general(a_ref[...], w_ref[...], _NT, preferred_element_type=F32) + b_ref[...]
    for c in range(g.shape[0] // M_CHUNK):
        rows = slice(c * M_CHUNK, (c + 1) * M_CHUNK)
        o_ref[rows, :] = _gate_table(g[rows, :])


def _gate_proj(a, w_gate_t, bias, tm=1024, name="gate_proj"):
    m, k = a.shape
    n = w_gate_t.shape[0]
    return pl.pallas_call(
        _gate_proj_kernel,
        grid=(m // tm,),
        in_specs=[pl.BlockSpec((tm, k), lambda i: (i, 0)),
                  pl.BlockSpec((n, k), lambda i: (0, 0)),
                  pl.BlockSpec((1, n), lambda i: (0, 0))],
        out_specs=pl.BlockSpec((tm, n), lambda i: (i, 0)),
        out_shape=jax.ShapeDtypeStruct((m, n), F32),
        compiler_params=_params(1),
        name=name,
    )(a, w_gate_t, bias)


def _matmul_ksplit_kernel(a_ref, w_ref, o_ref, acc_ref):
    kk = pl.program_id(2)

    @pl.when(kk == 0)
    def _():
        acc_ref[...] = jnp.zeros_like(acc_ref)

    acc_ref[...] += jnp.dot(a_ref[...], w_ref[...], preferred_element_type=F32)

    @pl.when(kk == pl.num_programs(2) - 1)
    def _():
        o_ref[...] = acc_ref[...].astype(o_ref.dtype)


def _matmul_ksplit(a, w, *, tm, tn, tk, name):
    m, k = a.shape
    n = w.shape[1]
    return pl.pallas_call(
        _matmul_ksplit_kernel,
        grid=(n // tn, m // tm, k // tk),
        in_specs=[pl.BlockSpec((tm, tk), lambda j, i, kk: (i, kk)),
                  pl.BlockSpec((tk, tn), lambda j, i, kk: (kk, j))],
        out_specs=pl.BlockSpec((tm, tn), lambda j, i, kk: (i, j)),
        out_shape=jax.ShapeDtypeStruct((m, n), BF16),
        scratch_shapes=[pltpu.VMEM((tm, tn), F32)],
        compiler_params=_params(3),
        name=name,
    )(a, w)


def _tile_lanes(x, width):
    return jnp.concatenate([x] * (width // x.shape[1]), axis=1)


def _rows_to_replicated_columns(x_row, y_row):
    length = x_row.shape[1]

    def pieces(v):
        hi = v.astype(BF16)
        rest = v - hi.astype(F32)
        mid = rest.astype(BF16)
        return [hi, mid, (rest - mid.astype(F32)).astype(BF16)]

    n_rows = 16
    stacked = jnp.concatenate(
        pieces(x_row) + pieces(y_row) + [jnp.zeros((n_rows - 6, length), BF16)], axis=0)
    piece = lax.broadcasted_iota(jnp.int32, (n_rows, 2 * LANES), 0)
    lane = lax.broadcasted_iota(jnp.int32, (n_rows, 2 * LANES), 1)
    selector = jnp.where((lane < LANES) == (piece < 3), 1.0, 0.0)
    selector = jnp.where(piece < 6, selector, 0.0).astype(BF16)
    both = lax.dot_general(stacked, selector, (((0,), (0,)), ((), ())), preferred_element_type=F32)
    return both[:, :LANES], both[:, LANES:]


def _mlstm_chunk(q_ref, k_ref, v_ref, gr_ref, ct_ref, m_ref, h_ref, *, reverse):
    length = k_ref.shape[0]
    dv = v_ref.shape[1]
    k = k_ref[...]
    v_aug = jnp.concatenate([v_ref[...], jnp.ones((length, LANES), BF16)], axis=1)
    if h_ref is not None:
        q = q_ref[...]
        qk = lax.dot_general(q, k, _NT, preferred_element_type=F32)
        qc_aug = jnp.dot(q, ct_ref[...].astype(BF16), preferred_element_type=F32)
    yield

    ji, jf = (2, 3) if reverse else (0, 1)
    gr = gr_ref[...]
    ig_r = gr[ji:ji + 1, :]
    b_r = gr[jf:jf + 1, :]
    b_c, ig_c = _rows_to_replicated_columns(b_r, ig_r)
    b_tot = b_r[:, 0:1] if reverse else b_r[:, length - 1:length]
    m_prev = m_ref[...]
    w_c = b_tot - b_c + ig_c
    m_new = jnp.maximum(b_tot + m_prev, jnp.max(w_c, axis=0, keepdims=True)[:, 0:1])
    decay = jnp.exp(b_tot + m_prev - m_new)
    kw = k.astype(F32) * _tile_lanes(jnp.exp(w_c - m_new), k.shape[1])
    if h_ref is not None:
        t_idx = lax.broadcasted_iota(jnp.int32, (length, length), 0)
        s_idx = lax.broadcasted_iota(jnp.int32, (length, length), 1)
        seen = (s_idx >= t_idx) if reverse else (s_idx <= t_idx)
        a_c = b_c + m_prev
        dm = jnp.where(seen, _tile_lanes(b_c, length) - b_r + ig_r, -jnp.inf)
        m_t = jnp.maximum(a_c, jnp.broadcast_to(jnp.max(dm, axis=1, keepdims=True), (length, LANES)))
        s = qk * Q_SCALE * jnp.exp(dm - _tile_lanes(m_t, length))
        inter = jnp.exp(a_c - m_t) * Q_SCALE
        s_sum = jnp.broadcast_to(jnp.sum(s, axis=1, keepdims=True), (length, LANES))
        den = inter * qc_aug[:, dv:] + s_sum
        scale = 1.0 / jnp.maximum(jnp.abs(den), jnp.exp(-m_t))
    yield

    if h_ref is not None:
        sv = jnp.dot(s.astype(BF16), v_aug[:, :dv], preferred_element_type=F32)
        num = _tile_lanes(inter, dv) * qc_aug[:, :dv] + sv
        h_ref[...] = (num * _tile_lanes(scale, dv)).astype(h_ref.dtype)
    ct_ref[...] = decay * ct_ref[...] + lax.dot_general(
        kw.astype(BF16), v_aug, (((0,), (0,)), ((), ())), preferred_element_type=F32)
    m_ref[...] = m_new
    yield


def _interleave(*chunks):
    for _ in range(3):
        for chunk in chunks:
            next(chunk)


def _mlstm_kernel(qf, kf, vf, grf, qb, kb, vb, grb, kc, vc, grc, hf, hb, *state):
    c = pl.program_id(2)
    fwd_state = [state[2 * i:2 * i + 2] for i in range(M_HEADS_PER_STEP)]
    bwd_state = [state[2 * (M_HEADS_PER_STEP + i):2 * (M_HEADS_PER_STEP + i) + 2] for i in range(M_HEADS_PER_STEP)]

    def head(ref, i, width):
        return ref.at[:, i * width:(i + 1) * width]

    @pl.when(c == 0)
    def _():
        for ref in state:
            ref[...] = jnp.zeros_like(ref)
        chunks = []
        for i in range(M_HEADS_PER_STEP):
            k, v, g = head(kc, i, M_DQK), head(vc, i, M_DV), grc.at[i]
            chunks.append(_mlstm_chunk(None, k, v, g, *fwd_state[i], None, reverse=False))
            chunks.append(_mlstm_chunk(None, k, v, g, *bwd_state[i], None, reverse=True))
        _interleave(*chunks)

    @pl.when(c > 0)
    def _():
        chunks = []
        for i in range(M_HEADS_PER_STEP):
            chunks.append(_mlstm_chunk(head(qf, i, M_DQK), head(kf, i, M_DQK), head(vf, i, M_DV), grf.at[i],
                                       *fwd_state[i], head(hf, i, M_DV), reverse=False))
            chunks.append(_mlstm_chunk(head(qb, i, M_DQK), head(kb, i, M_DQK), head(vb, i, M_DV), grb.at[i],
                                       *bwd_state[i], head(hb, i, M_DV), reverse=True))
        _interleave(*chunks)


def _mlstm(proj, proj_ctx, gr, gr_ctx, batch):
    length = M_CHUNK
    assert CTX_LEN == length
    seq = proj.shape[0] // batch
    nc = seq // length
    hps = M_HEADS_PER_STEP
    qk_w, v_w = hps * M_DQK, hps * M_DV

    def fwd(c):
        return jnp.maximum(c - 1, 0)

    def bwd(c):
        return nc - jnp.maximum(c, 1)

    def lat_specs(chunk_of):
        return [
            pl.BlockSpec((length, qk_w), lambda b, h, c: (b * nc + chunk_of(c), OFF_QM // qk_w + h)),
            pl.BlockSpec((length, qk_w), lambda b, h, c: (b * nc + chunk_of(c), OFF_KM // qk_w + h)),
            pl.BlockSpec((length, v_w), lambda b, h, c: (b * nc + chunk_of(c), OFF_VM // v_w + h)),
            pl.BlockSpec((None, hps, 4, length), lambda b, h, c: (b, h, 0, chunk_of(c))),
        ]

    ctx_specs = [
        pl.BlockSpec((length, qk_w), lambda b, h, c: (b, CTX_OFF_KM // qk_w + h)),
        pl.BlockSpec((length, v_w), lambda b, h, c: (b, CTX_OFF_VM // v_w + h)),
        pl.BlockSpec((None, hps, 4, length), lambda b, h, c: (b, h, 0, 0)),
    ]
    out_shape = jax.ShapeDtypeStruct((batch * seq, M_V_W), BF16)
    state = [pltpu.VMEM((M_DQK, M_DV + LANES), F32), pltpu.VMEM((1, 1), F32)]
    return pl.pallas_call(
        _mlstm_kernel,
        grid=(batch, M_HEADS // hps, nc + 1),
        in_specs=lat_specs(fwd) + lat_specs(bwd) + ctx_specs,
        out_specs=[pl.BlockSpec((length, v_w), lambda b, h, c: (b * nc + fwd(c), h)),
                   pl.BlockSpec((length, v_w), lambda b, h, c: (b * nc + bwd(c), h))],
        out_shape=[out_shape, out_shape],
        scratch_shapes=state * (2 * hps),
        compiler_params=_params(3),
        name="mlstm",
    )(proj, proj, proj, gr, proj, proj, proj, gr, proj_ctx, proj_ctx, gr_ctx)


def _readout_kernel(hf_ref, hb_ref, om_ref, g_ref, o_ref):
    for h in range(M_HEADS):
        cols = slice(h * M_DV, (h + 1) * M_DV)
        x = hf_ref[:, cols].astype(F32) + hb_ref[:, cols].astype(F32)
        y = _rms(x) * g_ref[:, cols] * _sigmoid(om_ref[:, cols].astype(F32))
        o_ref[:, cols] = y.astype(o_ref.dtype)


def _readout(hf, hb, proj, m_norm_g, tm=256):
    m = hf.shape[0]
    row = pl.BlockSpec((tm, M_V_W), lambda i: (i, 0))
    return pl.pallas_call(
        _readout_kernel,
        grid=(m // tm,),
        in_specs=[row, row,
                  pl.BlockSpec((tm, M_V_W), lambda i: (i, OFF_OM // M_V_W)),
                  pl.BlockSpec((1, M_V_W), lambda i: (0, 0))],
        out_specs=row,
        out_shape=jax.ShapeDtypeStruct((m, M_V_W), BF16),
        compiler_params=_params(1),
        name="mlstm_readout",
    )(hf, hb, proj, m_norm_g.reshape(1, M_V_W))


def _rope(x, cos, sin_signed):
    lane = lax.broadcasted_iota(jnp.int32, x.shape, 1)
    partner = jnp.where((lane % 64) < 32, pltpu.roll(x, LANES - 32, 1), pltpu.roll(x, 32, 1))
    return x * cos + partner * sin_signed


def _attn_kernel(sink_ref, q_ref, k_ref, v_ref, kc_ref, vc_ref, cos_ref, sin_ref, o_ref,
                 kpad, vpad, s_buf, p_buf, den_buf):
    seq = k_ref.shape[0]
    nb = seq // A_BLOCK
    kvh = pl.program_id(1)
    qscale = (A_HEAD_DIM ** -0.5) * LOG2_E
    rows = A_GROUP * A_BLOCK
    n_win = 3 * A_BLOCK
    zeros = jnp.zeros((A_BLOCK, A_HEAD_DIM), BF16)
    kpad[0:A_BLOCK, :] = zeros
    kpad[A_BLOCK + seq:, :] = zeros
    vpad[0:A_BLOCK, :] = zeros
    vpad[A_BLOCK + seq:, :] = zeros
    kpad[A_BLOCK:A_BLOCK + seq, :] = _rope(k_ref[...].astype(F32), cos_ref[...], sin_ref[...]).astype(BF16)
    vpad[A_BLOCK:A_BLOCK + seq, :] = v_ref[...]
    qi = lax.broadcasted_iota(jnp.int32, (rows, n_win), 0) & (A_BLOCK - 1)
    kj = lax.broadcasted_iota(jnp.int32, (rows, n_win), 1)
    band = (kj >= qi) & (kj <= qi + 2 * WINDOW)
    kj_row = lax.broadcasted_iota(jnp.int32, (1, n_win), 1)
    sink = jnp.concatenate(
        [jnp.full((A_BLOCK, 1), sink_ref[kvh * A_GROUP + g] * LOG2_E, F32) for g in range(A_GROUP)], axis=0)
    nt = (((1,), (1,)), ((), ()))

    def scores(n):
        r0 = pl.multiple_of(n * A_BLOCK, A_BLOCK)
        cos = cos_ref[pl.ds(r0, A_BLOCK), :] * qscale
        sin = sin_ref[pl.ds(r0, A_BLOCK), :] * qscale
        q = jnp.concatenate(
            [_rope(q_ref[pl.ds(r0, A_BLOCK), g * A_HEAD_DIM:(g + 1) * A_HEAD_DIM].astype(F32), cos, sin)
             for g in range(A_GROUP)], axis=0).astype(BF16)
        kpos = r0 - A_BLOCK + kj_row
        valid = band & ((kpos >= 0) & (kpos < seq))
        sw = lax.dot_general(q, kpad[pl.ds(r0, n_win), :], nt, preferred_element_type=F32)
        s_buf[:, 0:n_win] = jnp.where(valid, sw, -jnp.inf)
        s_buf[:, n_win:] = lax.dot_general(q, kc_ref[...], nt, preferred_element_type=F32)

    def softmax():
        s = s_buf[...]
        mx = jnp.maximum(jnp.max(s, axis=1, keepdims=True), sink)
        p = jnp.exp2(s - mx)
        den_buf[...] = jnp.sum(p, axis=1, keepdims=True) + jnp.exp2(sink - mx)
        p_buf[...] = p.astype(p_buf.dtype)

    def output(n):
        r0 = pl.multiple_of(n * A_BLOCK, A_BLOCK)
        o = (jnp.dot(p_buf[:, 0:n_win], vpad[pl.ds(r0, n_win), :], preferred_element_type=F32)
             + jnp.dot(p_buf[:, n_win:], vc_ref[...], preferred_element_type=F32)) * (1.0 / den_buf[...])
        for g in range(A_GROUP):
            o_ref[pl.ds(r0, A_BLOCK), g * A_HEAD_DIM:(g + 1) * A_HEAD_DIM] = (
                o[g * A_BLOCK:(g + 1) * A_BLOCK, :].astype(o_ref.dtype))

    scores(0)
    softmax()
    scores(1)

    def body(t, carry):
        output(t - 1)
        softmax()
        scores(t + 1)
        return carry

    lax.fori_loop(1, nb - 1, body, 0)
    output(nb - 2)
    softmax()
    output(nb - 1)


def _attention(proj, proj_ctx, sink, cos, sin, batch):
    seq = proj.shape[0] // batch
    qw = A_GROUP * A_HEAD_DIM
    tab = pl.BlockSpec((seq, A_HEAD_DIM), lambda b, h: (0, 0))
    return pl.pallas_call(
        _attn_kernel,
        grid=(batch, A_KV_HEADS),
        in_specs=[pl.BlockSpec(memory_space=pltpu.SMEM),
                  pl.BlockSpec((seq, qw), lambda b, h: (b, OFF_QA // qw + h)),
                  pl.BlockSpec((seq, A_HEAD_DIM), lambda b, h: (b, OFF_KA // A_HEAD_DIM + h)),
                  pl.BlockSpec((seq, A_HEAD_DIM), lambda b, h: (b, OFF_VA // A_HEAD_DIM + h)),
                  pl.BlockSpec((CTX_LEN, A_HEAD_DIM), lambda b, h: (b, CTX_OFF_KA // A_HEAD_DIM + h)),
                  pl.BlockSpec((CTX_LEN, A_HEAD_DIM), lambda b, h: (b, CTX_OFF_VA // A_HEAD_DIM + h)),
                  tab, tab],
        out_specs=pl.BlockSpec((seq, qw), lambda b, h: (b, h)),
        out_shape=jax.ShapeDtypeStruct((batch * seq, A_Q_W), BF16),
        scratch_shapes=[pltpu.VMEM((seq + 2 * A_BLOCK, A_HEAD_DIM), BF16),
                        pltpu.VMEM((seq + 2 * A_BLOCK, A_HEAD_DIM), BF16),
                        pltpu.VMEM((A_GROUP * A_BLOCK, 3 * A_BLOCK + CTX_LEN), F32),
                        pltpu.VMEM((A_GROUP * A_BLOCK, 3 * A_BLOCK + CTX_LEN), BF16),
                        pltpu.VMEM((A_GROUP * A_BLOCK, 1), F32)],
        compiler_params=_params(2),
        name="window_attention",
    )(sink, proj, proj, proj, proj_ctx, proj_ctx, cos, sin)


def _rope_tables(seq):
    t = jnp.arange(seq)
    nf = A_HEAD_DIM // 4
    freqs = ROPE_BASE ** (-jnp.arange(nf, dtype=F32) / nf)
    ang_r = (t // GRID_W).astype(F32)[:, None] * freqs
    ang_c = (t % GRID_W).astype(F32)[:, None] * freqs
    cos = jnp.concatenate([jnp.cos(ang_r), jnp.cos(ang_r), jnp.cos(ang_c), jnp.cos(ang_c)], axis=-1)
    sin = jnp.concatenate([-jnp.sin(ang_r), jnp.sin(ang_r), -jnp.sin(ang_c), jnp.sin(ang_c)], axis=-1)
    return cos, sin


def _gate_rows(g, batch):
    return g[:, :N_GATES].reshape(batch, -1, 4, M_HEADS).transpose(0, 3, 2, 1)


def kernel(x, c, ctx, c_ctx, w_mod, b_mod, norm_g, w_in, m_gate_b, m_norm_g, attn_sink,
           w_out_m, w_out_a, w_o, w_ff1, w_ff2):
    batch, seq, d = x.shape
    n_ctx = ctx.shape[1]
    x2 = x.reshape(batch * seq, d)
    ctx2 = ctx.reshape(batch * n_ctx, d)

    w_in_t = w_in[0].T
    w_gate_t = jnp.pad(w_in_t[W_IN_GATE_LO:W_IN_GATE_HI], ((0, GATE_PAD - N_GATES), (0, 0))).astype(BF16)
    gate_bias = jnp.pad(m_gate_b[0].reshape(1, N_GATES), ((0, 0), (0, GATE_PAD - N_GATES)))

    c_all = jnp.zeros((MOD_ROWS, d), F32).at[:batch].set(c).at[batch].set(c_ctx)
    mod = _modulation(c_all, w_mod[0], b_mod[0]).reshape(MOD_ROWS, 6, 1, d)
    gn = norm_g[0].reshape(4, 1, d)

    u = _prenorm(x2, gn, mod, seq)
    uc = _prenorm(ctx2, gn, mod, n_ctx, fixed_row=batch)

    proj = _stream_matmul(u, w_in_t, n_out=PROJ_ALIGNED_W, out_dtype=BF16, transposed=True, name="in_proj")
    g_lat = _gate_proj(u, w_gate_t, gate_bias)
    ctx_tn = 512
    lat_tiles = (OFF_OM - OFF_KM) // ctx_tn
    proj_ctx = _stream_matmul(
        uc, w_in_t, n_out=CTX_PROJ_W, out_dtype=BF16, transposed=True, tn=ctx_tn, name="in_proj_ctx",
        tile_of=lambda j: jnp.where(j < lat_tiles, j + OFF_KM // ctx_tn, j - lat_tiles + OFF_KA // ctx_tn))
    g_ctx = _gate_proj(uc, w_gate_t, gate_bias, name="gate_proj_ctx")

    hf, hb = _mlstm(proj, proj_ctx, _gate_rows(g_lat, batch), _gate_rows(g_ctx, batch), batch)
    hm = _readout(hf, hb, proj, m_norm_g[0])

    cos, sin = _rope_tables(seq)
    at = _attention(proj, proj_ctx, attn_sink[0], cos, sin, batch)

    zm = _stream_matmul(hm, w_out_m, n_out=d, out_dtype=BF16, epilogue=_gate_epilogue,
                        extra=[proj], extra_off=[OFF_BGM], name="out_m")
    z = _stream_matmul(at, w_out_a, n_out=d, out_dtype=BF16, epilogue=_gate_add_epilogue,
                       extra=[proj, zm], extra_off=[OFF_BGA, 0], name="out_a_merge")
    mix = _stream_matmul(z, w_o, n_out=d, out_dtype=BF16, name="out_proj")
    h1, u2 = _post_mix(x2, mix, gn, mod, seq)

    hid, w_ff2_bf16 = _stream_matmul(u2, w_ff1, n_out=D_FF, out_dtype=BF16, epilogue=_relu2_epilogue,
                                     side=w_ff2, name="ff1")
    y = _matmul_ksplit(hid, w_ff2_bf16, tm=1024, tn=1024, tk=4096, name="ff2")
    out = _post_mlp(h1, y, gn, mod, seq)
    return out.reshape(batch, seq, d)
```

```python
import functools

import jax
import jax.numpy as jnp
from jax import lax
from jax.experimental import pallas as pl
from jax.experimental.pallas import tpu as pltpu

F32 = jnp.float32
BF16 = jnp.bfloat16

D_MODEL = 4096
GRID_W = 64
CTX_LEN = 256
M_HEADS = 8
M_DQK = 256
M_DV = 512
A_HEADS = 32
A_KV_HEADS = 8
A_HEAD_DIM = 128
A_GROUP = A_HEADS // A_KV_HEADS
WINDOW = 128
A_BLOCK = 128
ROPE_BASE = 10000.0
D_FF = 4 * D_MODEL
EPS = 1e-6

M_QK_W = M_HEADS * M_DQK
M_V_W = M_HEADS * M_DV
A_Q_W = A_HEADS * A_HEAD_DIM
A_KV_W = A_KV_HEADS * A_HEAD_DIM
N_GATES = 4 * M_HEADS

W_IN_GATE_LO = 2 * M_QK_W + 2 * M_V_W
W_IN_GATE_HI = W_IN_GATE_LO + N_GATES
OFF_QM = 0
OFF_KM = OFF_QM + M_QK_W
OFF_VM = OFF_KM + M_QK_W
OFF_OM = OFF_VM + M_V_W
OFF_QA = OFF_OM + M_V_W
OFF_KA = OFF_QA + A_Q_W
OFF_VA = OFF_KA + A_KV_W
OFF_BGM = OFF_VA + A_KV_W
OFF_BGA = OFF_BGM + D_MODEL
PROJ_ALIGNED_W = OFF_BGA + D_MODEL
CTX_OFF_KM = 0
CTX_OFF_VM = M_QK_W
CTX_OFF_KA = M_QK_W + M_V_W
CTX_OFF_VA = CTX_OFF_KA + A_KV_W
CTX_PROJ_W = CTX_OFF_VA + A_KV_W

M_CHUNK = 256
M_HEADS_PER_STEP = 2
Q_SCALE = M_DQK ** -0.5
LANES = 128
GATE_PAD = LANES
MOD_ROWS = 8
VMEM_LIMIT = 60000 * 1024


def _params(n_axes):
    return pltpu.CompilerParams(dimension_semantics=("arbitrary",) * n_axes, vmem_limit_bytes=VMEM_LIMIT)


def _sigmoid(x):
    return 1.0 / (1.0 + jnp.exp(-x))


def _log_sigmoid(x):
    return jnp.minimum(x, 0.0) - jnp.log1p(jnp.exp(-jnp.abs(x)))


def _rms(x):
    return x * lax.rsqrt(jnp.mean(x * x, axis=-1, keepdims=True) + EPS)


def _mod_kernel(c_ref, w_ref, b_ref, o_ref):
    c = c_ref[...]
    sc = (c * _sigmoid(c)).astype(BF16)
    o_ref[...] = jnp.dot(sc, w_ref[...].astype(BF16), preferred_element_type=F32) + b_ref[...]


def _modulation(c_all, w_mod, b_mod, tn=1024):
    d, n = w_mod.shape
    return pl.pallas_call(
        _mod_kernel,
        grid=(n // tn,),
        in_specs=[pl.BlockSpec((MOD_ROWS, d), lambda j: (0, 0)),
                  pl.BlockSpec((d, tn), lambda j: (0, j)),
                  pl.BlockSpec((1, tn), lambda j: (0, j))],
        out_specs=pl.BlockSpec((MOD_ROWS, tn), lambda j: (0, j)),
        out_shape=jax.ShapeDtypeStruct((MOD_ROWS, n), F32),
        compiler_params=_params(1),
        name="modulation",
    )(c_all, w_mod, b_mod.reshape(1, n))


def _prenorm_kernel(x_ref, g_ref, sh_ref, sc_ref, o_ref):
    y = _rms(x_ref[...]) * g_ref[...]
    o_ref[...] = (y * (1.0 + sc_ref[...]) + sh_ref[...]).astype(o_ref.dtype)


def _mod_spec(slot, rows_per_mod_row, tm, fixed_row=None):
    if fixed_row is None:
        return pl.BlockSpec((None, None, 1, D_MODEL), lambda i: ((i * tm) // rows_per_mod_row, slot, 0, 0))
    return pl.BlockSpec((None, None, 1, D_MODEL), lambda i: (fixed_row, slot, 0, 0))


def _norm_spec(slot):
    return pl.BlockSpec((None, 1, D_MODEL), lambda i: (slot, 0, 0))


def _prenorm(x2, norm_g, mod, rows_per_mod_row, fixed_row=None, tm=256):
    m = x2.shape[0]
    return pl.pallas_call(
        _prenorm_kernel,
        grid=(m // tm,),
        in_specs=[pl.BlockSpec((tm, D_MODEL), lambda i: (i, 0)),
                  _norm_spec(0),
                  _mod_spec(0, rows_per_mod_row, tm, fixed_row),
                  _mod_spec(1, rows_per_mod_row, tm, fixed_row)],
        out_specs=pl.BlockSpec((tm, D_MODEL), lambda i: (i, 0)),
        out_shape=jax.ShapeDtypeStruct((m, D_MODEL), BF16),
        compiler_params=_params(1),
        name="prenorm",
    )(x2, norm_g, mod, mod)


def _post_mix_kernel(x_ref, mix_ref, gn1_ref, gn2_ref, g1_ref, sh2_ref, s2_ref, h1_ref, u2_ref):
    h1 = x_ref[...] + g1_ref[...] * (_rms(mix_ref[...].astype(F32)) * gn1_ref[...])
    h1_ref[...] = h1
    y2 = _rms(h1) * gn2_ref[...]
    u2_ref[...] = (y2 * (1.0 + s2_ref[...]) + sh2_ref[...]).astype(u2_ref.dtype)


def _post_mix(x2, mix, norm_g, mod, rows_per_mod_row, tm=256):
    m = x2.shape[0]
    row = pl.BlockSpec((tm, D_MODEL), lambda i: (i, 0))
    return pl.pallas_call(
        _post_mix_kernel,
        grid=(m // tm,),
        in_specs=[row, row, _norm_spec(1), _norm_spec(2),
                  _mod_spec(2, rows_per_mod_row, tm), _mod_spec(3, rows_per_mod_row, tm),
                  _mod_spec(4, rows_per_mod_row, tm)],
        out_specs=[row, row],
        out_shape=[jax.ShapeDtypeStruct((m, D_MODEL), F32), jax.ShapeDtypeStruct((m, D_MODEL), BF16)],
        compiler_params=_params(1),
        name="post_mix",
    )(x2, mix, norm_g, norm_g, mod, mod, mod)


def _post_mlp_kernel(h1_ref, y_ref, gn_ref, g2_ref, o_ref):
    o_ref[...] = h1_ref[...] + g2_ref[...] * (_rms(y_ref[...].astype(F32)) * gn_ref[...])


def _post_mlp(h1, y, norm_g, mod, rows_per_mod_row, tm=256):
    m = h1.shape[0]
    row = pl.BlockSpec((tm, D_MODEL), lambda i: (i, 0))
    return pl.pallas_call(
        _post_mlp_kernel,
        grid=(m // tm,),
        in_specs=[row, row, _norm_spec(3), _mod_spec(5, rows_per_mod_row, tm)],
        out_specs=row,
        out_shape=jax.ShapeDtypeStruct((m, D_MODEL), F32),
        compiler_params=_params(1),
        name="post_mlp",
    )(h1, y, norm_g, mod)


_NT = (((1,), (1,)), ((), ()))


def _stream_kernel(a_ref, slab_ref, *rest, nj, transposed, tile_of, epilogue, n_extra, has_next, has_side):
    j = pl.program_id(0)
    i = pl.program_id(1)
    rest = list(rest)
    next_ref = rest.pop(0) if has_next else None
    extra = [rest.pop(0) for _ in range(n_extra)]
    side_ref = rest.pop(0) if has_side else None
    o_ref = rest.pop(0)
    side_out_ref = rest.pop(0) if has_side else None
    w_scr = rest.pop(0)
    rows = slab_ref.shape[0]
    r0 = pl.multiple_of(i * rows, rows)

    def cast_slab():
        dst = w_scr.at[j % 2]
        if has_next:
            tile_rows = w_scr.shape[1]
            shifted = tile_of(jnp.minimum(j, nj - 1)) * tile_rows >= W_IN_GATE_LO
            body = rows - N_GATES
            start = pl.multiple_of(jnp.where(shifted, N_GATES, 0), N_GATES)
            dst[pl.ds(r0, body), :] = slab_ref[pl.ds(start, body), :].astype(BF16)
            tail = jnp.where(shifted, next_ref[...], slab_ref[body:, :])
            dst[pl.ds(r0 + body, N_GATES), :] = tail.astype(BF16)
        else:
            dst[pl.ds(r0, rows), :] = slab_ref[...].astype(BF16)

    @pl.when(j == 0)
    def _():
        cast_slab()

    @pl.when(j > 0)
    def _():
        cast_slab()
        w = w_scr[(j - 1) % 2]
        if transposed:
            acc = lax.dot_general(a_ref[...], w, _NT, preferred_element_type=F32)
        else:
            acc = jnp.dot(a_ref[...], w, preferred_element_type=F32)
        o_ref[...] = epilogue(acc, *[e[...] for e in extra]).astype(o_ref.dtype)
        if has_side:
            side_out_ref[...] = side_ref[...].astype(side_out_ref.dtype)


def _stream_matmul(a, w, *, n_out, out_dtype, epilogue=lambda acc: acc, extra=(), extra_off=(), transposed=False,
                   tile_of=None, side=None, tm=1024, tn=1024, name):
    m, k = a.shape
    nj, ni = n_out // tn, m // tm
    tile_of = (lambda j: j) if tile_of is None else tile_of

    def prev(j):
        return jnp.maximum(j - 1, 0)

    def cur(j):
        return jnp.minimum(j, nj - 1)

    def row(j, i):
        return jnp.where(j == 0, 0, i)

    in_specs = [pl.BlockSpec((tm, k), lambda j, i: (row(j, i), 0))]
    args = [a]
    if transposed:
        slab_rows = tn // ni
        in_specs.append(pl.BlockSpec((slab_rows, k), lambda j, i: (tile_of(cur(j)) * ni + i, 0)))
        in_specs.append(pl.BlockSpec(
            (N_GATES, k), lambda j, i: ((tile_of(cur(j)) * tn + (i + 1) * slab_rows) // N_GATES, 0)))
        args += [w, w]
        scratch = pltpu.VMEM((2, tn, k), BF16)
    else:
        in_specs.append(pl.BlockSpec((None, k // ni, tn), lambda j, i: (0, i, cur(j))))
        args.append(w)
        scratch = pltpu.VMEM((2, k, tn), BF16)
    for arr, off in zip(extra, extra_off):
        in_specs.append(pl.BlockSpec((tm, tn), lambda j, i, off=off: (row(j, i), off // tn + prev(j))))
        args.append(arr)
    out_specs = [pl.BlockSpec((tm, tn), lambda j, i: (row(j, i), prev(j)))]
    out_shape = [jax.ShapeDtypeStruct((m, n_out), out_dtype)]
    if side is not None:
        s_rows, s_cols = side.shape[1:]
        slab = s_rows // (nj * ni)

        def step(j, i):
            return jnp.maximum((j - 1) * ni + i, 0)

        in_specs.append(pl.BlockSpec((None, slab, s_cols), lambda j, i: (0, step(j, i), 0)))
        args.append(side)
        out_specs.append(pl.BlockSpec((slab, s_cols), lambda j, i: (step(j, i), 0)))
        out_shape.append(jax.ShapeDtypeStruct((s_rows, s_cols), BF16))
    out = pl.pallas_call(
        functools.partial(_stream_kernel, nj=nj, transposed=transposed, tile_of=tile_of, epilogue=epilogue,
                          n_extra=len(extra), has_next=transposed, has_side=side is not None),
        grid=(nj + 1, ni),
        in_specs=in_specs,
        out_specs=out_specs,
        out_shape=out_shape,
        scratch_shapes=[scratch],
        compiler_params=_params(2),
        name=name,
    )(*args)
    return out if side is not None else out[0]


def _gate_epilogue(acc, g):
    return _sigmoid(g.astype(F32)) * acc


def _gate_add_epilogue(acc, g, prev):
    return prev.astype(F32) + _sigmoid(g.astype(F32)) * acc


def _relu2_epilogue(acc):
    return jnp.square(jnp.maximum(acc, 0.0))


def _gate_table(g):
    length = g.shape[0]
    lf = _log_sigmoid(g)
    row = lax.broadcasted_iota(jnp.int32, g.shape, 0)
    col = lax.broadcasted_iota(jnp.int32, g.shape, 1)
    prefix = lf
    shift = 1
    while shift < length:
        prefix = prefix + jnp.where(row >= shift, pltpu.roll(prefix, shift, 0), 0.0)
        shift *= 2
    suffix = prefix[length - 1:length, :] - prefix + lf
    is_f_fwd = (col >= M_HEADS) & (col < 2 * M_HEADS)
    is_f_bwd = (col >= 3 * M_HEADS) & (col < 4 * M_HEADS)
    return jnp.where(is_f_fwd, prefix, jnp.where(is_f_bwd, suffix, g))


def _gate_proj_kernel(a_ref, w_ref, b_ref, o_ref):
    g = lax.dot_general(a_ref[...], w_ref[...], _NT, preferred_element_type=F32) + b_ref[...]
    for c in range(g.shape[0] // M_CHUNK):
        rows = slice(c * M_CHUNK, (c + 1) * M_CHUNK)
        o_ref[rows, :] = _gate_table(g[rows, :])


def _gate_proj(a, w_gate_t, bias, tm=1024, name="gate_proj"):
    m, k = a.shape
    n = w_gate_t.shape[0]
    return pl.pallas_call(
        _gate_proj_kernel,
        grid=(m // tm,),
        in_specs=[pl.BlockSpec((tm, k), lambda i: (i, 0)),
                  pl.BlockSpec((n, k), lambda i: (0, 0)),
                  pl.BlockSpec((1, n), lambda i: (0, 0))],
        out_specs=pl.BlockSpec((tm, n), lambda i: (i, 0)),
        out_shape=jax.ShapeDtypeStruct((m, n), F32),
        compiler_params=_params(1),
        name=name,
    )(a, w_gate_t, bias)


def _matmul_ksplit_kernel(a_ref, w_ref, o_ref, acc_ref):
    kk = pl.program_id(2)

    @pl.when(kk == 0)
    def _():
        acc_ref[...] = jnp.zeros_like(acc_ref)

    acc_ref[...] += jnp.dot(a_ref[...], w_ref[...], preferred_element_type=F32)

    @pl.when(kk == pl.num_programs(2) - 1)
    def _():
        o_ref[...] = acc_ref[...].astype(o_ref.dtype)


def _matmul_ksplit(a, w, *, tm, tn, tk, name):
    m, k = a.shape
    n = w.shape[1]
    return pl.pallas_call(
        _matmul_ksplit_kernel,
        grid=(n // tn, m // tm, k // tk),
        in_specs=[pl.BlockSpec((tm, tk), lambda j, i, kk: (i, kk)),
                  pl.BlockSpec((tk, tn), lambda j, i, kk: (kk, j))],
        out_specs=pl.BlockSpec((tm, tn), lambda j, i, kk: (i, j)),
        out_shape=jax.ShapeDtypeStruct((m, n), BF16),
        scratch_shapes=[pltpu.VMEM((tm, tn), F32)],
        compiler_params=_params(3),
        name=name,
    )(a, w)


def _tile_lanes(x, width):
    return jnp.concatenate([x] * (width // x.shape[1]), axis=1)


def _rows_to_replicated_columns(x_row, y_row):
    length = x_row.shape[1]

    def pieces(v):
        hi = v.astype(BF16)
        rest = v - hi.astype(F32)
        mid = rest.astype(BF16)
        return [hi, mid, (rest - mid.astype(F32)).astype(BF16)]

    n_rows = 16
    stacked = jnp.concatenate(
        pieces(x_row) + pieces(y_row) + [jnp.zeros((n_rows - 6, length), BF16)], axis=0)
    piece = lax.broadcasted_iota(jnp.int32, (n_rows, 2 * LANES), 0)
    lane = lax.broadcasted_iota(jnp.int32, (n_rows, 2 * LANES), 1)
    selector = jnp.where((lane < LANES) == (piece < 3), 1.0, 0.0)
    selector = jnp.where(piece < 6, selector, 0.0).astype(BF16)
    both = lax.dot_general(stacked, selector, (((0,), (0,)), ((), ())), preferred_element_type=F32)
    return both[:, :LANES], both[:, LANES:]


def _mlstm_chunk(q_ref, k_ref, v_ref, gr_ref, ct_ref, m_ref, h_ref, *, reverse):
    length = k_ref.shape[0]
    dv = v_ref.shape[1]
    k = k_ref[...]
    v_aug = jnp.concatenate([v_ref[...], jnp.ones((length, LANES), BF16)], axis=1)
    if h_ref is not None:
        q = q_ref[...]
        qk = lax.dot_general(q, k, _NT, preferred_element_type=F32)
        ct = ct_ref[...]
        n_hi = ct[:, dv:].astype(BF16)
        n_mid = (ct[:, dv:] - n_hi.astype(F32)).astype(BF16)
        qc_aug = jnp.dot(q, jnp.concatenate([ct[:, :dv].astype(BF16), n_hi, n_mid], axis=1),
                         preferred_element_type=F32)
        qn = qc_aug[:, dv:dv + LANES] + qc_aug[:, dv + LANES:]
    yield

    ji, jf = (2, 3) if reverse else (0, 1)
    gr = gr_ref[...]
    ig_r = gr[ji:ji + 1, :]
    b_r = gr[jf:jf + 1, :]
    b_c, ig_c = _rows_to_replicated_columns(b_r, ig_r)
    b_tot = b_r[:, 0:1] if reverse else b_r[:, length - 1:length]
    m_prev = m_ref[...]
    w_c = b_tot - b_c + ig_c
    m_new = jnp.maximum(b_tot + m_prev, jnp.max(w_c, axis=0, keepdims=True)[:, 0:1])
    decay = jnp.exp(b_tot + m_prev - m_new)
    kw = k.astype(F32) * _tile_lanes(jnp.exp(w_c - m_new), k.shape[1])
    if h_ref is not None:
        t_idx = lax.broadcasted_iota(jnp.int32, (length, length), 0)
        s_idx = lax.broadcasted_iota(jnp.int32, (length, length), 1)
        seen = (s_idx >= t_idx) if reverse else (s_idx <= t_idx)
        a_c = b_c + m_prev
        dm = jnp.where(seen, _tile_lanes(b_c, length) - b_r + ig_r, -jnp.inf)
        m_t = jnp.maximum(a_c, jnp.broadcast_to(jnp.max(dm, axis=1, keepdims=True), (length, LANES)))
        s = qk * Q_SCALE * jnp.exp(dm - _tile_lanes(m_t, length))
        inter = jnp.exp(a_c - m_t) * Q_SCALE
        s_sum = jnp.broadcast_to(jnp.sum(s, axis=1, keepdims=True), (length, LANES))
        den = inter * qn + s_sum
        scale = 1.0 / jnp.maximum(jnp.abs(den), jnp.exp(-m_t))
    yield

    if h_ref is not None:
        sv = jnp.dot(s.astype(BF16), v_aug[:, :dv], preferred_element_type=F32)
        num = _tile_lanes(inter, dv) * qc_aug[:, :dv] + sv
        h_ref[...] = (num * _tile_lanes(scale, dv)).astype(h_ref.dtype)
    ct_ref[...] = decay * ct_ref[...] + lax.dot_general(
        kw.astype(BF16), v_aug, (((0,), (0,)), ((), ())), preferred_element_type=F32)
    m_ref[...] = m_new
    yield


def _interleave(*chunks):
    for _ in range(3):
        for chunk in chunks:
            next(chunk)


def _mlstm_kernel(qf, kf, vf, grf, qb, kb, vb, grb, kc, vc, grc, hf, hb, *state):
    c = pl.program_id(2)
    fwd_state = [state[2 * i:2 * i + 2] for i in range(M_HEADS_PER_STEP)]
    bwd_state = [state[2 * (M_HEADS_PER_STEP + i):2 * (M_HEADS_PER_STEP + i) + 2] for i in range(M_HEADS_PER_STEP)]

    def head(ref, i, width):
        return ref.at[:, i * width:(i + 1) * width]

    @pl.when(c == 0)
    def _():
        for ref in state:
            ref[...] = jnp.zeros_like(ref)
        chunks = []
        for i in range(M_HEADS_PER_STEP):
            k, v, g = head(kc, i, M_DQK), head(vc, i, M_DV), grc.at[i]
            chunks.append(_mlstm_chunk(None, k, v, g, *fwd_state[i], None, reverse=False))
            chunks.append(_mlstm_chunk(None, k, v, g, *bwd_state[i], None, reverse=True))
        _interleave(*chunks)

    @pl.when(c > 0)
    def _():
        chunks = []
        for i in range(M_HEADS_PER_STEP):
            chunks.append(_mlstm_chunk(head(qf, i, M_DQK), head(kf, i, M_DQK), head(vf, i, M_DV), grf.at[i],
                                       *fwd_state[i], head(hf, i, M_DV), reverse=False))
            chunks.append(_mlstm_chunk(head(qb, i, M_DQK), head(kb, i, M_DQK), head(vb, i, M_DV), grb.at[i],
                                       *bwd_state[i], head(hb, i, M_DV), reverse=True))
        _interleave(*chunks)


def _mlstm(proj, proj_ctx, gr, gr_ctx, batch):
    length = M_CHUNK
    assert CTX_LEN == length
    seq = proj.shape[0] // batch
    nc = seq // length
    hps = M_HEADS_PER_STEP
    qk_w, v_w = hps * M_DQK, hps * M_DV

    def fwd(c):
        return jnp.maximum(c - 1, 0)

    def bwd(c):
        return nc - jnp.maximum(c, 1)

    def lat_specs(chunk_of):
        return [
            pl.BlockSpec((length, qk_w), lambda b, h, c: (b * nc + chunk_of(c), OFF_QM // qk_w + h)),
            pl.BlockSpec((length, qk_w), lambda b, h, c: (b * nc + chunk_of(c), OFF_KM // qk_w + h)),
            pl.BlockSpec((length, v_w), lambda b, h, c: (b * nc + chunk_of(c), OFF_VM // v_w + h)),
            pl.BlockSpec((None, hps, 4, length), lambda b, h, c: (b, h, 0, chunk_of(c))),
        ]

    ctx_specs = [
        pl.BlockSpec((length, qk_w), lambda b, h, c: (b, CTX_OFF_KM // qk_w + h)),
        pl.BlockSpec((length, v_w), lambda b, h, c: (b, CTX_OFF_VM // v_w + h)),
        pl.BlockSpec((None, hps, 4, length), lambda b, h, c: (b, h, 0, 0)),
    ]
    out_shape = jax.ShapeDtypeStruct((batch * seq, M_V_W), BF16)
    state = [pltpu.VMEM((M_DQK, M_DV + LANES), F32), pltpu.VMEM((1, 1), F32)]
    return pl.pallas_call(
        _mlstm_kernel,
        grid=(batch, M_HEADS // hps, nc + 1),
        in_specs=lat_specs(fwd) + lat_specs(bwd) + ctx_specs,
        out_specs=[pl.BlockSpec((length, v_w), lambda b, h, c: (b * nc + fwd(c), h)),
                   pl.BlockSpec((length, v_w), lambda b, h, c: (b * nc + bwd(c), h))],
        out_shape=[out_shape, out_shape],
        scratch_shapes=state * (2 * hps),
        compiler_params=_params(3),
        name="mlstm",
    )(proj, proj, proj, gr, proj, proj, proj, gr, proj_ctx, proj_ctx, gr_ctx)


def _readout_kernel(hf_ref, hb_ref, om_ref, g_ref, o_ref):
    for h in range(M_HEADS):
        cols = slice(h * M_DV, (h + 1) * M_DV)
        x = hf_ref[:, cols].astype(F32) + hb_ref[:, cols].astype(F32)
        y = _rms(x) * g_ref[:, cols] * _sigmoid(om_ref[:, cols].astype(F32))
        o_ref[:, cols] = y.astype(o_ref.dtype)


def _readout(hf, hb, proj, m_norm_g, tm=256):
    m = hf.shape[0]
    row = pl.BlockSpec((tm, M_V_W), lambda i: (i, 0))
    return pl.pallas_call(
        _readout_kernel,
        grid=(m // tm,),
        in_specs=[row, row,
                  pl.BlockSpec((tm, M_V_W), lambda i: (i, OFF_OM // M_V_W)),
                  pl.BlockSpec((1, M_V_W), lambda i: (0, 0))],
        out_specs=row,
        out_shape=jax.ShapeDtypeStruct((m, M_V_W), BF16),
        compiler_params=_params(1),
        name="mlstm_readout",
    )(hf, hb, proj, m_norm_g.reshape(1, M_V_W))


def _rope(x, cos, sin_signed):
    lane = lax.broadcasted_iota(jnp.int32, x.shape, 1)
    partner = jnp.where((lane % 64) < 32, pltpu.roll(x, LANES - 32, 1), pltpu.roll(x, 32, 1))
    return x * cos + partner * sin_signed


def _attn_kernel(sink_ref, q_ref, k_ref, v_ref, kc_ref, vc_ref, cos_ref, sin_ref, o_ref,
                 kpad, vpad, s_buf, p_buf, den_buf):
    seq = k_ref.shape[0]
    nb = seq // A_BLOCK
    kvh = pl.program_id(1)
    scale = A_HEAD_DIM ** -0.5
    rows = A_GROUP * A_BLOCK
    n_win = 3 * A_BLOCK
    zeros = jnp.zeros((A_BLOCK, A_HEAD_DIM), BF16)
    kpad[0:A_BLOCK, :] = zeros
    kpad[A_BLOCK + seq:, :] = zeros
    vpad[0:A_BLOCK, :] = zeros
    vpad[A_BLOCK + seq:, :] = zeros
    kpad[A_BLOCK:A_BLOCK + seq, :] = _rope(k_ref[...].astype(F32), cos_ref[...], sin_ref[...]).astype(BF16)
    vpad[A_BLOCK:A_BLOCK + seq, :] = v_ref[...]
    qi = lax.broadcasted_iota(jnp.int32, (rows, n_win), 0) & (A_BLOCK - 1)
    kj = lax.broadcasted_iota(jnp.int32, (rows, n_win), 1)
    band = (kj >= qi) & (kj <= qi + 2 * WINDOW)
    kj_row = lax.broadcasted_iota(jnp.int32, (1, n_win), 1)
    sink = jnp.concatenate(
        [jnp.full((A_BLOCK, 1), sink_ref[kvh * A_GROUP + g], F32) for g in range(A_GROUP)], axis=0)
    nt = (((1,), (1,)), ((), ()))

    def scores(n):
        r0 = pl.multiple_of(n * A_BLOCK, A_BLOCK)
        cos = cos_ref[pl.ds(r0, A_BLOCK), :]
        sin = sin_ref[pl.ds(r0, A_BLOCK), :]
        q = jnp.concatenate(
            [_rope(q_ref[pl.ds(r0, A_BLOCK), g * A_HEAD_DIM:(g + 1) * A_HEAD_DIM].astype(F32), cos, sin)
             for g in range(A_GROUP)], axis=0).astype(BF16)
        kpos = r0 - A_BLOCK + kj_row
        valid = band & ((kpos >= 0) & (kpos < seq))
        sw = lax.dot_general(q, kpad[pl.ds(r0, n_win), :], nt, preferred_element_type=F32) * scale
        s_buf[:, 0:n_win] = jnp.where(valid, sw, -jnp.inf)
        s_buf[:, n_win:] = lax.dot_general(q, kc_ref[...], nt, preferred_element_type=F32) * scale

    def softmax():
        s = s_buf[...]
        mx = jnp.maximum(jnp.max(s, axis=1, keepdims=True), sink)
        p = jnp.exp(s - mx)
        den_buf[...] = jnp.sum(p, axis=1, keepdims=True) + jnp.exp(sink - mx)
        p_buf[...] = p.astype(p_buf.dtype)

    def output(n):
        r0 = pl.multiple_of(n * A_BLOCK, A_BLOCK)
        o = (jnp.dot(p_buf[:, 0:n_win], vpad[pl.ds(r0, n_win), :], preferred_element_type=F32)
             + jnp.dot(p_buf[:, n_win:], vc_ref[...], preferred_element_type=F32)) * (1.0 / den_buf[...])
        for g in range(A_GROUP):
            o_ref[pl.ds(r0, A_BLOCK), g * A_HEAD_DIM:(g + 1) * A_HEAD_DIM] = (
                o[g * A_BLOCK:(g + 1) * A_BLOCK, :].astype(o_ref.dtype))

    scores(0)
    softmax()
    scores(1)

    def body(t, carry):
        output(t - 1)
        softmax()
        scores(t + 1)
        return carry

    lax.fori_loop(1, nb - 1, body, 0)
    output(nb - 2)
    softmax()
    output(nb - 1)


def _attention(proj, proj_ctx, sink, cos, sin, batch):
    seq = proj.shape[0] // batch
    qw = A_GROUP * A_HEAD_DIM
    tab = pl.BlockSpec((seq, A_HEAD_DIM), lambda b, h: (0, 0))
    return pl.pallas_call(
        _attn_kernel,
        grid=(batch, A_KV_HEADS),
        in_specs=[pl.BlockSpec(memory_space=pltpu.SMEM),
                  pl.BlockSpec((seq, qw), lambda b, h: (b, OFF_QA // qw + h)),
                  pl.BlockSpec((seq, A_HEAD_DIM), lambda b, h: (b, OFF_KA // A_HEAD_DIM + h)),
                  pl.BlockSpec((seq, A_HEAD_DIM), lambda b, h: (b, OFF_VA // A_HEAD_DIM + h)),
                  pl.BlockSpec((CTX_LEN, A_HEAD_DIM), lambda b, h: (b, CTX_OFF_KA // A_HEAD_DIM + h)),
                  pl.BlockSpec((CTX_LEN, A_HEAD_DIM), lambda b, h: (b, CTX_OFF_VA // A_HEAD_DIM + h)),
                  tab, tab],
        out_specs=pl.BlockSpec((seq, qw), lambda b, h: (b, h)),
        out_shape=jax.ShapeDtypeStruct((batch * seq, A_Q_W), BF16),
        scratch_shapes=[pltpu.VMEM((seq + 2 * A_BLOCK, A_HEAD_DIM), BF16),
                        pltpu.VMEM((seq + 2 * A_BLOCK, A_HEAD_DIM), BF16),
                        pltpu.VMEM((A_GROUP * A_BLOCK, 3 * A_BLOCK + CTX_LEN), F32),
                        pltpu.VMEM((A_GROUP * A_BLOCK, 3 * A_BLOCK + CTX_LEN), BF16),
                        pltpu.VMEM((A_GROUP * A_BLOCK, 1), F32)],
        compiler_params=_params(2),
        name="window_attention",
    )(sink, proj, proj, proj, proj_ctx, proj_ctx, cos, sin)


def _rope_tables(seq):
    t = jnp.arange(seq)
    nf = A_HEAD_DIM // 4
    freqs = ROPE_BASE ** (-jnp.arange(nf, dtype=F32) / nf)
    ang_r = (t // GRID_W).astype(F32)[:, None] * freqs
    ang_c = (t % GRID_W).astype(F32)[:, None] * freqs
    cos = jnp.concatenate([jnp.cos(ang_r), jnp.cos(ang_r), jnp.cos(ang_c), jnp.cos(ang_c)], axis=-1)
    sin = jnp.concatenate([-jnp.sin(ang_r), jnp.sin(ang_r), -jnp.sin(ang_c), jnp.sin(ang_c)], axis=-1)
    return cos, sin


def _gate_rows(g, batch):
    return g[:, :N_GATES].reshape(batch, -1, 4, M_HEADS).transpose(0, 3, 2, 1)


def kernel(x, c, ctx, c_ctx, w_mod, b_mod, norm_g, w_in, m_gate_b, m_norm_g, attn_sink,
           w_out_m, w_out_a, w_o, w_ff1, w_ff2):
    batch, seq, d = x.shape
    n_ctx = ctx.shape[1]
    x2 = x.reshape(batch * seq, d)
    ctx2 = ctx.reshape(batch * n_ctx, d)

    w_in_t = w_in[0].T
    w_gate_t = jnp.pad(w_in_t[W_IN_GATE_LO:W_IN_GATE_HI], ((0, GATE_PAD - N_GATES), (0, 0))).astype(BF16)
    gate_bias = jnp.pad(m_gate_b[0].reshape(1, N_GATES), ((0, 0), (0, GATE_PAD - N_GATES)))

    c_all = jnp.zeros((MOD_ROWS, d), F32).at[:batch].set(c).at[batch].set(c_ctx)
    mod = _modulation(c_all, w_mod[0], b_mod[0]).reshape(MOD_ROWS, 6, 1, d)
    gn = norm_g[0].reshape(4, 1, d)

    u = _prenorm(x2, gn, mod, seq)
    uc = _prenorm(ctx2, gn, mod, n_ctx, fixed_row=batch)

    proj = _stream_matmul(u, w_in_t, n_out=PROJ_ALIGNED_W, out_dtype=BF16, transposed=True, name="in_proj")
    g_lat = _gate_proj(u, w_gate_t, gate_bias)
    ctx_tn = 512
    lat_tiles = (OFF_OM - OFF_KM) // ctx_tn
    proj_ctx = _stream_matmul(
        uc, w_in_t, n_out=CTX_PROJ_W, out_dtype=BF16, transposed=True, tn=ctx_tn, name="in_proj_ctx",
        tile_of=lambda j: jnp.where(j < lat_tiles, j + OFF_KM // ctx_tn, j - lat_tiles + OFF_KA // ctx_tn))
    g_ctx = _gate_proj(uc, w_gate_t, gate_bias, name="gate_proj_ctx")

    hf, hb = _mlstm(proj, proj_ctx, _gate_rows(g_lat, batch), _gate_rows(g_ctx, batch), batch)
    hm = _readout(hf, hb, proj, m_norm_g[0])

    cos, sin = _rope_tables(seq)
    at = _attention(proj, proj_ctx, attn_sink[0], cos, sin, batch)

    zm = _stream_matmul(hm, w_out_m, n_out=d, out_dtype=BF16, epilogue=_gate_epilogue,
                        extra=[proj], extra_off=[OFF_BGM], name="out_m")
    z = _stream_matmul(at, w_out_a, n_out=d, out_dtype=BF16, epilogue=_gate_add_epilogue,
                       extra=[proj, zm], extra_off=[OFF_BGA, 0], name="out_a_merge")
    mix = _stream_matmul(z, w_o, n_out=d, out_dtype=BF16, name="out_proj")
    h1, u2 = _post_mix(x2, mix, gn, mod, seq)

    hid, w_ff2_bf16 = _stream_matmul(u2, w_ff1, n_out=D_FF, out_dtype=BF16, epilogue=_relu2_epilogue,
                                     side=w_ff2, name="ff1")
    y = _matmul_ksplit(hid, w_ff2_bf16, tm=1024, tn=1024, tk=4096, name="ff2")
    out = _post_mlp(h1, y, gn, mod, seq)
    return out.reshape(batch, seq, d)
```

```python
import functools

import jax
import jax.numpy as jnp
from jax import lax
from jax.experimental import pallas as pl
from jax.experimental.pallas import tpu as pltpu

F32 = jnp.float32
BF16 = jnp.bfloat16

D_MODEL = 4096
GRID_W = 64
CTX_LEN = 256
M_HEADS = 8
M_DQK = 256
M_DV = 512
A_HEADS = 32
A_KV_HEADS = 8
A_HEAD_DIM = 128
A_GROUP = A_HEADS // A_KV_HEADS
WINDOW = 128
A_BLOCK = 128
ROPE_BASE = 10000.0
D_FF = 4 * D_MODEL
EPS = 1e-6

M_QK_W = M_HEADS * M_DQK
M_V_W = M_HEADS * M_DV
A_Q_W = A_HEADS * A_HEAD_DIM
A_KV_W = A_KV_HEADS * A_HEAD_DIM
N_GATES = 4 * M_HEADS

W_IN_GATE_LO = 2 * M_QK_W + 2 * M_V_W
W_IN_GATE_HI = W_IN_GATE_LO + N_GATES
OFF_QM = 0
OFF_KM = OFF_QM + M_QK_W
OFF_VM = OFF_KM + M_QK_W
OFF_OM = OFF_VM + M_V_W
OFF_QA = OFF_OM + M_V_W
OFF_KA = OFF_QA + A_Q_W
OFF_VA = OFF_KA + A_KV_W
OFF_BGM = OFF_VA + A_KV_W
OFF_BGA = OFF_BGM + D_MODEL
PROJ_ALIGNED_W = OFF_BGA + D_MODEL
CTX_OFF_KM = 0
CTX_OFF_VM = M_QK_W
CTX_OFF_KA = M_QK_W + M_V_W
CTX_OFF_VA = CTX_OFF_KA + A_KV_W
CTX_PROJ_W = CTX_OFF_VA + A_KV_W

M_CHUNK = 256
Q_SCALE = M_DQK ** -0.5
LANES = 128
GATE_PAD = LANES
MOD_ROWS = 8
VMEM_LIMIT = 60000 * 1024


def _params(n_axes):
    return pltpu.CompilerParams(dimension_semantics=("arbitrary",) * n_axes, vmem_limit_bytes=VMEM_LIMIT)


def _sigmoid(x):
    return 1.0 / (1.0 + jnp.exp(-x))


def _log_sigmoid(x):
    return jnp.minimum(x, 0.0) - jnp.log1p(jnp.exp(-jnp.abs(x)))


def _rms(x):
    return x * lax.rsqrt(jnp.mean(x * x, axis=-1, keepdims=True) + EPS)


def _mod_kernel(c_ref, w_ref, b_ref, o_ref):
    c = c_ref[...]
    sc = (c * _sigmoid(c)).astype(BF16)
    o_ref[...] = jnp.dot(sc, w_ref[...].astype(BF16), preferred_element_type=F32) + b_ref[...]


def _modulation(c_all, w_mod, b_mod, tn=1024):
    d, n = w_mod.shape
    return pl.pallas_call(
        _mod_kernel,
        grid=(n // tn,),
        in_specs=[pl.BlockSpec((MOD_ROWS, d), lambda j: (0, 0)),
                  pl.BlockSpec((d, tn), lambda j: (0, j)),
                  pl.BlockSpec((1, tn), lambda j: (0, j))],
        out_specs=pl.BlockSpec((MOD_ROWS, tn), lambda j: (0, j)),
        out_shape=jax.ShapeDtypeStruct((MOD_ROWS, n), F32),
        compiler_params=_params(1),
        name="modulation",
    )(c_all, w_mod, b_mod.reshape(1, n))


def _prenorm_kernel(x_ref, g_ref, sh_ref, sc_ref, o_ref):
    y = _rms(x_ref[...]) * g_ref[...]
    o_ref[...] = (y * (1.0 + sc_ref[...]) + sh_ref[...]).astype(o_ref.dtype)


def _mod_spec(slot, rows_per_mod_row, tm, fixed_row=None):
    if fixed_row is None:
        return pl.BlockSpec((None, None, 1, D_MODEL), lambda i: ((i * tm) // rows_per_mod_row, slot, 0, 0))
    return pl.BlockSpec((None, None, 1, D_MODEL), lambda i: (fixed_row, slot, 0, 0))


def _norm_spec(slot):
    return pl.BlockSpec((None, 1, D_MODEL), lambda i: (slot, 0, 0))


def _prenorm(x2, norm_g, mod, rows_per_mod_row, fixed_row=None, tm=256):
    m = x2.shape[0]
    return pl.pallas_call(
        _prenorm_kernel,
        grid=(m // tm,),
        in_specs=[pl.BlockSpec((tm, D_MODEL), lambda i: (i, 0)),
                  _norm_spec(0),
                  _mod_spec(0, rows_per_mod_row, tm, fixed_row),
                  _mod_spec(1, rows_per_mod_row, tm, fixed_row)],
        out_specs=pl.BlockSpec((tm, D_MODEL), lambda i: (i, 0)),
        out_shape=jax.ShapeDtypeStruct((m, D_MODEL), BF16),
        compiler_params=_params(1),
        name="prenorm",
    )(x2, norm_g, mod, mod)


def _post_mix_kernel(x_ref, mix_ref, gn1_ref, gn2_ref, g1_ref, sh2_ref, s2_ref, h1_ref, u2_ref):
    h1 = x_ref[...] + g1_ref[...] * (_rms(mix_ref[...].astype(F32)) * gn1_ref[...])
    h1_ref[...] = h1
    y2 = _rms(h1) * gn2_ref[...]
    u2_ref[...] = (y2 * (1.0 + s2_ref[...]) + sh2_ref[...]).astype(u2_ref.dtype)


def _post_mix(x2, mix, norm_g, mod, rows_per_mod_row, tm=256):
    m = x2.shape[0]
    row = pl.BlockSpec((tm, D_MODEL), lambda i: (i, 0))
    return pl.pallas_call(
        _post_mix_kernel,
        grid=(m // tm,),
        in_specs=[row, row, _norm_spec(1), _norm_spec(2),
                  _mod_spec(2, rows_per_mod_row, tm), _mod_spec(3, rows_per_mod_row, tm),
                  _mod_spec(4, rows_per_mod_row, tm)],
        out_specs=[row, row],
        out_shape=[jax.ShapeDtypeStruct((m, D_MODEL), F32), jax.ShapeDtypeStruct((m, D_MODEL), BF16)],
        compiler_params=_params(1),
        name="post_mix",
    )(x2, mix, norm_g, norm_g, mod, mod, mod)


def _post_mlp_kernel(h1_ref, y_ref, gn_ref, g2_ref, o_ref):
    o_ref[...] = h1_ref[...] + g2_ref[...] * (_rms(y_ref[...].astype(F32)) * gn_ref[...])


def _post_mlp(h1, y, norm_g, mod, rows_per_mod_row, tm=256):
    m = h1.shape[0]
    row = pl.BlockSpec((tm, D_MODEL), lambda i: (i, 0))
    return pl.pallas_call(
        _post_mlp_kernel,
        grid=(m // tm,),
        in_specs=[row, row, _norm_spec(3), _mod_spec(5, rows_per_mod_row, tm)],
        out_specs=row,
        out_shape=jax.ShapeDtypeStruct((m, D_MODEL), F32),
        compiler_params=_params(1),
        name="post_mlp",
    )(h1, y, norm_g, mod)


_NT = (((1,), (1,)), ((), ()))


def _stream_kernel(a_ref, slab_ref, *rest, nj, transposed, tile_of, epilogue, n_extra, has_next, has_side):
    j = pl.program_id(0)
    i = pl.program_id(1)
    rest = list(rest)
    next_ref = rest.pop(0) if has_next else None
    extra = [rest.pop(0) for _ in range(n_extra)]
    side_ref = rest.pop(0) if has_side else None
    o_ref = rest.pop(0)
    side_out_ref = rest.pop(0) if has_side else None
    w_scr = rest.pop(0)
    rows = slab_ref.shape[0]
    r0 = pl.multiple_of(i * rows, rows)

    def cast_slab():
        dst = w_scr.at[j % 2]
        if has_next:
            tile_rows = w_scr.shape[1]
            shifted = tile_of(jnp.minimum(j, nj - 1)) * tile_rows >= W_IN_GATE_LO
            body = rows - N_GATES
            start = pl.multiple_of(jnp.where(shifted, N_GATES, 0), N_GATES)
            dst[pl.ds(r0, body), :] = slab_ref[pl.ds(start, body), :].astype(BF16)
            tail = jnp.where(shifted, next_ref[...], slab_ref[body:, :])
            dst[pl.ds(r0 + body, N_GATES), :] = tail.astype(BF16)
        else:
            dst[pl.ds(r0, rows), :] = slab_ref[...].astype(BF16)

    @pl.when(j == 0)
    def _():
        cast_slab()

    @pl.when(j > 0)
    def _():
        cast_slab()
        w = w_scr[(j - 1) % 2]
        if transposed:
            acc = lax.dot_general(a_ref[...], w, _NT, preferred_element_type=F32)
        else:
            acc = jnp.dot(a_ref[...], w, preferred_element_type=F32)
        o_ref[...] = epilogue(acc, *[e[...] for e in extra]).astype(o_ref.dtype)
        if has_side:
            side_out_ref[...] = side_ref[...].astype(side_out_ref.dtype)


def _stream_matmul(a, w, *, n_out, out_dtype, epilogue=lambda acc: acc, extra=(), extra_off=(), transposed=False,
                   tile_of=None, side=None, tm=1024, tn=1024, name):
    m, k = a.shape
    nj, ni = n_out // tn, m // tm
    tile_of = (lambda j: j) if tile_of is None else tile_of

    def prev(j):
        return jnp.maximum(j - 1, 0)

    def cur(j):
        return jnp.minimum(j, nj - 1)

    def row(j, i):
        return jnp.where(j == 0, 0, i)

    in_specs = [pl.BlockSpec((tm, k), lambda j, i: (row(j, i), 0))]
    args = [a]
    if transposed:
        slab_rows = tn // ni
        in_specs.append(pl.BlockSpec((slab_rows, k), lambda j, i: (tile_of(cur(j)) * ni + i, 0)))
        in_specs.append(pl.BlockSpec(
            (N_GATES, k), lambda j, i: ((tile_of(cur(j)) * tn + (i + 1) * slab_rows) // N_GATES, 0)))
        args += [w, w]
        scratch = pltpu.VMEM((2, tn, k), BF16)
    else:
        in_specs.append(pl.BlockSpec((None, k // ni, tn), lambda j, i: (0, i, cur(j))))
        args.append(w)
        scratch = pltpu.VMEM((2, k, tn), BF16)
    for arr, off in zip(extra, extra_off):
        in_specs.append(pl.BlockSpec((tm, tn), lambda j, i, off=off: (row(j, i), off // tn + prev(j))))
        args.append(arr)
    out_specs = [pl.BlockSpec((tm, tn), lambda j, i: (row(j, i), prev(j)))]
    out_shape = [jax.ShapeDtypeStruct((m, n_out), out_dtype)]
    if side is not None:
        s_rows, s_cols = side.shape[1:]
        slab = s_rows // (nj * ni)

        def step(j, i):
            return jnp.maximum((j - 1) * ni + i, 0)

        in_specs.append(pl.BlockSpec((None, slab, s_cols), lambda j, i: (0, step(j, i), 0)))
        args.append(side)
        out_specs.append(pl.BlockSpec((slab, s_cols), lambda j, i: (step(j, i), 0)))
        out_shape.append(jax.ShapeDtypeStruct((s_rows, s_cols), BF16))
    out = pl.pallas_call(
        functools.partial(_stream_kernel, nj=nj, transposed=transposed, tile_of=tile_of, epilogue=epilogue,
                          n_extra=len(extra), has_next=transposed, has_side=side is not None),
        grid=(nj + 1, ni),
        in_specs=in_specs,
        out_specs=out_specs,
        out_shape=out_shape,
        scratch_shapes=[scratch],
        compiler_params=_params(2),
        name=name,
    )(*args)
    return out if side is not None else out[0]


def _gate_epilogue(acc, g):
    return _sigmoid(g.astype(F32)) * acc


def _gate_add_epilogue(acc, g, prev):
    return prev.astype(F32) + _sigmoid(g.astype(F32)) * acc


def _relu2_epilogue(acc):
    return jnp.square(jnp.maximum(acc, 0.0))


def _gate_table(g):
    length = g.shape[0]
    lf = _log_sigmoid(g)
    row = lax.broadcasted_iota(jnp.int32, g.shape, 0)
    col = lax.broadcasted_iota(jnp.int32, g.shape, 1)
    prefix = lf
    shift = 1
    while shift < length:
        prefix = prefix + jnp.where(row >= shift, pltpu.roll(prefix, shift, 0), 0.0)
        shift *= 2
    suffix = prefix[length - 1:length, :] - prefix + lf
    is_f_fwd = (col >= M_HEADS) & (col < 2 * M_HEADS)
    is_f_bwd = (col >= 3 * M_HEADS) & (col < 4 * M_HEADS)
    return jnp.where(is_f_fwd, prefix, jnp.where(is_f_bwd, suffix, g))


def _gate_proj_kernel(a_ref, w_ref, b_ref, o_ref):
    g = lax.dot_general(a_ref[...], w_ref[...], _NT, preferred_element_type=F32) + b_ref[...]
    for c in range(g.shape[0] // M_CHUNK):
        rows = slice(c * M_CHUNK, (c + 1) * M_CHUNK)
        o_ref[rows, :] = _gate_table(g[rows, :])


def _gate_proj(a, w_gate_t, bias, tm=1024, name="gate_proj"):
    m, k = a.shape
    n = w_gate_t.shape[0]
    return pl.pallas_call(
        _gate_proj_kernel,
        grid=(m // tm,),
        in_specs=[pl.BlockSpec((tm, k), lambda i: (i, 0)),
                  pl.BlockSpec((n, k), lambda i: (0, 0)),
                  pl.BlockSpec((1, n), lambda i: (0, 0))],
        out_specs=pl.BlockSpec((tm, n), lambda i: (i, 0)),
        out_shape=jax.ShapeDtypeStruct((m, n), F32),
        compiler_params=_params(1),
        name=name,
    )(a, w_gate_t, bias)


def _matmul_ksplit_kernel(a_ref, w_ref, o_ref, acc_ref):
    kk = pl.program_id(2)

    @pl.when(kk == 0)
    def _():
        acc_ref[...] = jnp.zeros_like(acc_ref)

    acc_ref[...] += jnp.dot(a_ref[...], w_ref[...], preferred_element_type=F32)

    @pl.when(kk == pl.num_programs(2) - 1)
    def _():
        o_ref[...] = acc_ref[...].astype(o_ref.dtype)


def _matmul_ksplit(a, w, *, tm, tn, tk, name):
    m, k = a.shape
    n = w.shape[1]
    return pl.pallas_call(
        _matmul_ksplit_kernel,
        grid=(n // tn, m // tm, k // tk),
        in_specs=[pl.BlockSpec((tm, tk), lambda j, i, kk: (i, kk)),
                  pl.BlockSpec((tk, tn), lambda j, i, kk: (kk, j))],
        out_specs=pl.BlockSpec((tm, tn), lambda j, i, kk: (i, j)),
        out_shape=jax.ShapeDtypeStruct((m, n), BF16),
        scratch_shapes=[pltpu.VMEM((tm, tn), F32)],
        compiler_params=_params(3),
        name=name,
    )(a, w)


def _tile_lanes(x, width):
    return jnp.concatenate([x] * (width // x.shape[1]), axis=1)


def _rows_to_replicated_columns(x_row, y_row):
    length = x_row.shape[1]

    def pieces(v):
        hi = v.astype(BF16)
        rest = v - hi.astype(F32)
        mid = rest.astype(BF16)
        return [hi, mid, (rest - mid.astype(F32)).astype(BF16)]

    n_rows = 16
    stacked = jnp.concatenate(
        pieces(x_row) + pieces(y_row) + [jnp.zeros((n_rows - 6, length), BF16)], axis=0)
    piece = lax.broadcasted_iota(jnp.int32, (n_rows, 2 * LANES), 0)
    lane = lax.broadcasted_iota(jnp.int32, (n_rows, 2 * LANES), 1)
    selector = jnp.where((lane < LANES) == (piece < 3), 1.0, 0.0)
    selector = jnp.where(piece < 6, selector, 0.0).astype(BF16)
    both = lax.dot_general(stacked, selector, (((0,), (0,)), ((), ())), preferred_element_type=F32)
    return both[:, :LANES], both[:, LANES:]


def _mlstm_kernel(q_ref, k_ref, v_ref, g_ref, kc_ref, vc_ref, gc_ref, hf_ref, hb_ref, *scratch):
    length = M_CHUNK
    dv = M_DV
    nc = q_ref.shape[0] // length
    per_dir = len(scratch) // 2
    dirs = [(False, hf_ref) + tuple(scratch[:per_dir]), (True, hb_ref) + tuple(scratch[per_dir:])]

    def rows(reverse, t):
        r = (nc - 1 - t) if reverse else t
        return r, pl.ds(pl.multiple_of(r * length, length), length)

    def gate_vectors(g4, reverse):
        ji, jf = (2, 3) if reverse else (0, 1)
        ig_r = g4[ji:ji + 1, :]
        b_r = g4[jf:jf + 1, :]
        b_tot = b_r[:, 0:1] if reverse else b_r[:, length - 1:length]
        b_c, ig_c = _rows_to_replicated_columns(b_r, ig_r)
        return ig_r, b_r, b_tot, b_c, ig_c

    def update_state(ct_ref, m_ref, k, v, b_tot, b_c, ig_c):
        m_prev = m_ref[...]
        w_c = b_tot - b_c + ig_c
        m_new = jnp.maximum(b_tot + m_prev, jnp.max(w_c, axis=0, keepdims=True)[:, 0:1])
        decay = jnp.exp(b_tot + m_prev - m_new)
        kw = k.astype(F32) * _tile_lanes(jnp.exp(w_c - m_new), k.shape[1])
        v_aug = jnp.concatenate([v, jnp.ones((length, LANES), BF16)], axis=1)
        ct_ref[...] = decay * ct_ref[...] + lax.dot_general(
            kw.astype(BF16), v_aug, (((0,), (0,)), ((), ())), preferred_element_type=F32)
        m_ref[...] = m_new

    def score_matmul(d, t, slot):
        reverse, _, _, _, qk_buf = d[:5]
        _, rs = rows(reverse, t)
        qk_buf[slot] = lax.dot_general(q_ref[rs, :], k_ref[rs, :], _NT, preferred_element_type=F32)

    def state_matmul(d, t):
        reverse, _, ct_ref, _, _, qc_buf = d[:6]
        _, rs = rows(reverse, t)
        ct = ct_ref[...]
        n_hi = ct[:, dv:].astype(BF16)
        n_mid = (ct[:, dv:] - n_hi.astype(F32)).astype(BF16)
        qc_buf[...] = jnp.dot(q_ref[rs, :], jnp.concatenate([ct[:, :dv].astype(BF16), n_hi, n_mid], axis=1),
                              preferred_element_type=F32)

    def gates(d, t, gate_vecs, qk):
        reverse, _, _, m_ref, _, qc_buf, s_buf, inter_buf, scale_buf = d
        ig_r, b_r, _, b_c, _ = gate_vecs
        t_idx = lax.broadcasted_iota(jnp.int32, (length, length), 0)
        s_idx = lax.broadcasted_iota(jnp.int32, (length, length), 1)
        seen = (s_idx >= t_idx) if reverse else (s_idx <= t_idx)
        a_c = b_c + m_ref[...]
        dm = jnp.where(seen, _tile_lanes(b_c, length) - b_r + ig_r, -jnp.inf)
        m_t = jnp.maximum(a_c, jnp.broadcast_to(jnp.max(dm, axis=1, keepdims=True), (length, LANES)))
        s = qk * Q_SCALE * jnp.exp(dm - _tile_lanes(m_t, length))
        inter = jnp.exp(a_c - m_t) * Q_SCALE
        qc = qc_buf[...]
        qn = qc[:, dv:dv + LANES] + qc[:, dv + LANES:]
        s_sum = jnp.broadcast_to(jnp.sum(s, axis=1, keepdims=True), (length, LANES))
        den = inter * qn + s_sum
        s_buf[...] = s.astype(s_buf.dtype)
        inter_buf[...] = _tile_lanes(inter, dv) * qc[:, :dv]
        scale_buf[...] = 1.0 / jnp.maximum(jnp.abs(den), jnp.exp(-m_t))

    def outputs(d, t):
        reverse, h_ref = d[:2]
        s_buf, inter_buf, scale_buf = d[6:]
        _, rs = rows(reverse, t)
        sv = jnp.dot(s_buf[...], v_ref[rs, :], preferred_element_type=F32)
        h_ref[rs, :] = ((inter_buf[...] + sv) * _tile_lanes(scale_buf[...], dv)).astype(h_ref.dtype)

    def step(t, slot, first=False, last=False):
        if not first:
            for d in dirs:
                outputs(d, t - 1)
        vecs = [gate_vectors(g_ref[rows(d[0], t)[0]], d[0]) for d in dirs]
        if not last:
            for d in dirs:
                score_matmul(d, t + 1, 1 - slot)
        for d, gv in zip(dirs, vecs):
            gates(d, t, gv, d[4][slot])
        if not last:
            for d, gv in zip(dirs, vecs):
                _, rs = rows(d[0], t)
                update_state(d[2], d[3], k_ref[rs, :], v_ref[rs, :], gv[2], gv[3], gv[4])
            for d in dirs:
                state_matmul(d, t + 1)

    for d in dirs:
        d[2][...] = jnp.zeros_like(d[2])
        d[3][...] = jnp.zeros_like(d[3])
        _, _, b_tot, b_c, ig_c = gate_vectors(gc_ref[...], d[0])
        update_state(d[2], d[3], kc_ref[...], vc_ref[...], b_tot, b_c, ig_c)
    for d in dirs:
        score_matmul(d, 0, 0)
        state_matmul(d, 0)
    step(0, 0, first=True)

    def body(pair, carry):
        step(2 * pair + 1, 1)
        step(2 * pair + 2, 0)
        return carry

    assert nc % 2 == 0
    lax.fori_loop(0, (nc - 2) // 2, body, 0)
    step(nc - 1, 1, last=True)
    for d in dirs:
        outputs(d, nc - 1)


def _mlstm(proj, proj_ctx, gates, gates_ctx, batch):
    length = M_CHUNK
    assert CTX_LEN == length
    seq = proj.shape[0] // batch
    nc = seq // length
    out_shape = jax.ShapeDtypeStruct((batch * seq, M_V_W), BF16)
    out_spec = pl.BlockSpec((seq, M_DV), lambda b, h: (b, h))
    per_dir = [pltpu.VMEM((M_DQK, M_DV + LANES), F32),
               pltpu.VMEM((1, 1), F32),
               pltpu.VMEM((2, length, length), F32),
               pltpu.VMEM((length, M_DV + 2 * LANES), F32),
               pltpu.VMEM((length, length), BF16),
               pltpu.VMEM((length, M_DV), F32),
               pltpu.VMEM((length, LANES), F32)]
    return pl.pallas_call(
        _mlstm_kernel,
        grid=(batch, M_HEADS),
        in_specs=[pl.BlockSpec((seq, M_DQK), lambda b, h: (b, OFF_QM // M_DQK + h)),
                  pl.BlockSpec((seq, M_DQK), lambda b, h: (b, OFF_KM // M_DQK + h)),
                  pl.BlockSpec((seq, M_DV), lambda b, h: (b, OFF_VM // M_DV + h)),
                  pl.BlockSpec((None, None, nc, 4, length), lambda b, h: (b, h, 0, 0, 0)),
                  pl.BlockSpec((length, M_DQK), lambda b, h: (b, CTX_OFF_KM // M_DQK + h)),
                  pl.BlockSpec((length, M_DV), lambda b, h: (b, CTX_OFF_VM // M_DV + h)),
                  pl.BlockSpec((None, None, 4, length), lambda b, h: (b, h, 0, 0))],
        out_specs=[out_spec, out_spec],
        out_shape=[out_shape, out_shape],
        scratch_shapes=per_dir + per_dir,
        compiler_params=_params(2),
        name="mlstm",
    )(proj, proj, proj, gates, proj_ctx, proj_ctx, gates_ctx)


def _readout_kernel(hf_ref, hb_ref, om_ref, g_ref, o_ref):
    for h in range(M_HEADS):
        cols = slice(h * M_DV, (h + 1) * M_DV)
        x = hf_ref[:, cols].astype(F32) + hb_ref[:, cols].astype(F32)
        y = _rms(x) * g_ref[:, cols] * _sigmoid(om_ref[:, cols].astype(F32))
        o_ref[:, cols] = y.astype(o_ref.dtype)


def _readout(hf, hb, proj, m_norm_g, tm=256):
    m = hf.shape[0]
    row = pl.BlockSpec((tm, M_V_W), lambda i: (i, 0))
    return pl.pallas_call(
        _readout_kernel,
        grid=(m // tm,),
        in_specs=[row, row,
                  pl.BlockSpec((tm, M_V_W), lambda i: (i, OFF_OM // M_V_W)),
                  pl.BlockSpec((1, M_V_W), lambda i: (0, 0))],
        out_specs=row,
        out_shape=jax.ShapeDtypeStruct((m, M_V_W), BF16),
        compiler_params=_params(1),
        name="mlstm_readout",
    )(hf, hb, proj, m_norm_g.reshape(1, M_V_W))


def _rope(x, cos, sin_signed):
    lane = lax.broadcasted_iota(jnp.int32, x.shape, 1)
    partner = jnp.where((lane % 64) < 32, pltpu.roll(x, LANES - 32, 1), pltpu.roll(x, 32, 1))
    return x * cos + partner * sin_signed


def _attn_kernel(sink_ref, q_ref, k_ref, v_ref, kc_ref, vc_ref, cos_ref, sin_ref, o_ref,
                 kpad, vpad, s_buf, p_buf, den_buf):
    seq = k_ref.shape[0]
    nb = seq // A_BLOCK
    kvh = pl.program_id(1)
    scale = A_HEAD_DIM ** -0.5
    rows = A_GROUP * A_BLOCK
    n_win = 3 * A_BLOCK
    zeros = jnp.zeros((A_BLOCK, A_HEAD_DIM), BF16)
    kpad[0:A_BLOCK, :] = zeros
    kpad[A_BLOCK + seq:, :] = zeros
    vpad[0:A_BLOCK, :] = zeros
    vpad[A_BLOCK + seq:, :] = zeros
    kpad[A_BLOCK:A_BLOCK + seq, :] = _rope(k_ref[...].astype(F32), cos_ref[...], sin_ref[...]).astype(BF16)
    vpad[A_BLOCK:A_BLOCK + seq, :] = v_ref[...]
    qi = lax.broadcasted_iota(jnp.int32, (rows, n_win), 0) & (A_BLOCK - 1)
    kj = lax.broadcasted_iota(jnp.int32, (rows, n_win), 1)
    band = (kj >= qi) & (kj <= qi + 2 * WINDOW)
    kj_row = lax.broadcasted_iota(jnp.int32, (1, n_win), 1)
    sink = jnp.concatenate(
        [jnp.full((A_BLOCK, 1), sink_ref[kvh * A_GROUP + g], F32) for g in range(A_GROUP)], axis=0)
    nt = (((1,), (1,)), ((), ()))

    def scores(n):
        r0 = pl.multiple_of(n * A_BLOCK, A_BLOCK)
        cos = cos_ref[pl.ds(r0, A_BLOCK), :]
        sin = sin_ref[pl.ds(r0, A_BLOCK), :]
        q = jnp.concatenate(
            [_rope(q_ref[pl.ds(r0, A_BLOCK), g * A_HEAD_DIM:(g + 1) * A_HEAD_DIM].astype(F32), cos, sin)
             for g in range(A_GROUP)], axis=0).astype(BF16)
        kpos = r0 - A_BLOCK + kj_row
        valid = band & ((kpos >= 0) & (kpos < seq))
        sw = lax.dot_general(q, kpad[pl.ds(r0, n_win), :], nt, preferred_element_type=F32) * scale
        s_buf[:, 0:n_win] = jnp.where(valid, sw, -jnp.inf)
        s_buf[:, n_win:] = lax.dot_general(q, kc_ref[...], nt, preferred_element_type=F32) * scale

    def softmax():
        s = s_buf[...]
        mx = jnp.maximum(jnp.max(s, axis=1, keepdims=True), sink)
        p = jnp.exp(s - mx)
        den_buf[...] = jnp.sum(p, axis=1, keepdims=True) + jnp.exp(sink - mx)
        p_buf[...] = p.astype(p_buf.dtype)

    def output(n):
        r0 = pl.multiple_of(n * A_BLOCK, A_BLOCK)
        o = (jnp.dot(p_buf[:, 0:n_win], vpad[pl.ds(r0, n_win), :], preferred_element_type=F32)
             + jnp.dot(p_buf[:, n_win:], vc_ref[...], preferred_element_type=F32)) * (1.0 / den_buf[...])
        for g in range(A_GROUP):
            o_ref[pl.ds(r0, A_BLOCK), g * A_HEAD_DIM:(g + 1) * A_HEAD_DIM] = (
                o[g * A_BLOCK:(g + 1) * A_BLOCK, :].astype(o_ref.dtype))

    scores(0)
    softmax()
    scores(1)

    def body(t, carry):
        output(t - 1)
        softmax()
        scores(t + 1)
        return carry

    lax.fori_loop(1, nb - 1, body, 0)
    output(nb - 2)
    softmax()
    output(nb - 1)


def _attention(proj, proj_ctx, sink, cos, sin, batch):
    seq = proj.shape[0] // batch
    qw = A_GROUP * A_HEAD_DIM
    tab = pl.BlockSpec((seq, A_HEAD_DIM), lambda b, h: (0, 0))
    return pl.pallas_call(
        _attn_kernel,
        grid=(batch, A_KV_HEADS),
        in_specs=[pl.BlockSpec(memory_space=pltpu.SMEM),
                  pl.BlockSpec((seq, qw), lambda b, h: (b, OFF_QA // qw + h)),
                  pl.BlockSpec((seq, A_HEAD_DIM), lambda b, h: (b, OFF_KA // A_HEAD_DIM + h)),
                  pl.BlockSpec((seq, A_HEAD_DIM), lambda b, h: (b, OFF_VA // A_HEAD_DIM + h)),
                  pl.BlockSpec((CTX_LEN, A_HEAD_DIM), lambda b, h: (b, CTX_OFF_KA // A_HEAD_DIM + h)),
                  pl.BlockSpec((CTX_LEN, A_HEAD_DIM), lambda b, h: (b, CTX_OFF_VA // A_HEAD_DIM + h)),
                  tab, tab],
        out_specs=pl.BlockSpec((seq, qw), lambda b, h: (b, h)),
        out_shape=jax.ShapeDtypeStruct((batch * seq, A_Q_W), BF16),
        scratch_shapes=[pltpu.VMEM((seq + 2 * A_BLOCK, A_HEAD_DIM), BF16),
                        pltpu.VMEM((seq + 2 * A_BLOCK, A_HEAD_DIM), BF16),
                        pltpu.VMEM((A_GROUP * A_BLOCK, 3 * A_BLOCK + CTX_LEN), F32),
                        pltpu.VMEM((A_GROUP * A_BLOCK, 3 * A_BLOCK + CTX_LEN), BF16),
                        pltpu.VMEM((A_GROUP * A_BLOCK, 1), F32)],
        compiler_params=_params(2),
        name="window_attention",
    )(sink, proj, proj, proj, proj_ctx, proj_ctx, cos, sin)


def _rope_tables(seq):
    t = jnp.arange(seq)
    nf = A_HEAD_DIM // 4
    freqs = ROPE_BASE ** (-jnp.arange(nf, dtype=F32) / nf)
    ang_r = (t // GRID_W).astype(F32)[:, None] * freqs
    ang_c = (t % GRID_W).astype(F32)[:, None] * freqs
    cos = jnp.concatenate([jnp.cos(ang_r), jnp.cos(ang_r), jnp.cos(ang_c), jnp.cos(ang_c)], axis=-1)
    sin = jnp.concatenate([-jnp.sin(ang_r), jnp.sin(ang_r), -jnp.sin(ang_c), jnp.sin(ang_c)], axis=-1)
    return cos, sin


def _gate_rows(g, batch):
    return g[:, :N_GATES].reshape(batch, -1, 4, M_HEADS).transpose(0, 3, 2, 1)


def kernel(x, c, ctx, c_ctx, w_mod, b_mod, norm_g, w_in, m_gate_b, m_norm_g, attn_sink,
           w_out_m, w_out_a, w_o, w_ff1, w_ff2):
    batch, seq, d = x.shape
    n_ctx = ctx.shape[1]
    x2 = x.reshape(batch * seq, d)
    ctx2 = ctx.reshape(batch * n_ctx, d)

    w_in_t = w_in[0].T
    w_gate_t = jnp.pad(w_in_t[W_IN_GATE_LO:W_IN_GATE_HI], ((0, GATE_PAD - N_GATES), (0, 0))).astype(BF16)
    gate_bias = jnp.pad(m_gate_b[0].reshape(1, N_GATES), ((0, 0), (0, GATE_PAD - N_GATES)))

    c_all = jnp.zeros((MOD_ROWS, d), F32).at[:batch].set(c).at[batch].set(c_ctx)
    mod = _modulation(c_all, w_mod[0], b_mod[0]).reshape(MOD_ROWS, 6, 1, d)
    gn = norm_g[0].reshape(4, 1, d)

    u = _prenorm(x2, gn, mod, seq)
    uc = _prenorm(ctx2, gn, mod, n_ctx, fixed_row=batch)

    proj = _stream_matmul(u, w_in_t, n_out=PROJ_ALIGNED_W, out_dtype=BF16, transposed=True, name="in_proj")
    g_lat = _gate_proj(u, w_gate_t, gate_bias)
    ctx_tn = 512
    lat_tiles = (OFF_OM - OFF_KM) // ctx_tn
    proj_ctx = _stream_matmul(
        uc, w_in_t, n_out=CTX_PROJ_W, out_dtype=BF16, transposed=True, tn=ctx_tn, name="in_proj_ctx",
        tile_of=lambda j: jnp.where(j < lat_tiles, j + OFF_KM // ctx_tn, j - lat_tiles + OFF_KA // ctx_tn))
    g_ctx = _gate_proj(uc, w_gate_t, gate_bias, name="gate_proj_ctx")

    g_rows = _gate_rows(g_lat, batch)
    g_chunks = g_rows.reshape(batch, M_HEADS, 4, seq // M_CHUNK, M_CHUNK).transpose(0, 1, 3, 2, 4)
    hf, hb = _mlstm(proj, proj_ctx, g_chunks, _gate_rows(g_ctx, batch), batch)
    hm = _readout(hf, hb, proj, m_norm_g[0])

    cos, sin = _rope_tables(seq)
    at = _attention(proj, proj_ctx, attn_sink[0], cos, sin, batch)

    zm = _stream_matmul(hm, w_out_m, n_out=d, out_dtype=BF16, epilogue=_gate_epilogue,
                        extra=[proj], extra_off=[OFF_BGM], name="out_m")
    z = _stream_matmul(at, w_out_a, n_out=d, out_dtype=BF16, epilogue=_gate_add_epilogue,
                       extra=[proj, zm], extra_off=[OFF_BGA, 0], name="out_a_merge")
    mix = _stream_matmul(z, w_o, n_out=d, out_dtype=BF16, name="out_proj")
    h1, u2 = _post_mix(x2, mix, gn, mod, seq)

    hid, w_ff2_bf16 = _stream_matmul(u2, w_ff1, n_out=D_FF, out_dtype=BF16, epilogue=_relu2_epilogue,
                                     side=w_ff2, name="ff1")
    y = _matmul_ksplit(hid, w_ff2_bf16, tm=1024, tn=1024, tk=4096, name="ff2")
    out = _post_mlp(h1, y, gn, mod, seq)
    return out.reshape(batch, seq, d)
```

```python
import functools

import jax
import jax.numpy as jnp
from jax import lax
from jax.experimental import pallas as pl
from jax.experimental.pallas import tpu as pltpu

F32 = jnp.float32
BF16 = jnp.bfloat16

D_MODEL = 4096
GRID_W = 64
CTX_LEN = 256
M_HEADS = 8
M_DQK = 256
M_DV = 512
A_HEADS = 32
A_KV_HEADS = 8
A_HEAD_DIM = 128
A_GROUP = A_HEADS // A_KV_HEADS
WINDOW = 128
A_BLOCK = 128
ROPE_BASE = 10000.0
D_FF = 4 * D_MODEL
EPS = 1e-6

M_QK_W = M_HEADS * M_DQK
M_V_W = M_HEADS * M_DV
A_Q_W = A_HEADS * A_HEAD_DIM
A_KV_W = A_KV_HEADS * A_HEAD_DIM
N_GATES = 4 * M_HEADS

W_IN_GATE_LO = 2 * M_QK_W + 2 * M_V_W
W_IN_GATE_HI = W_IN_GATE_LO + N_GATES
OFF_QM = 0
OFF_KM = OFF_QM + M_QK_W
OFF_VM = OFF_KM + M_QK_W
OFF_OM = OFF_VM + M_V_W
OFF_QA = OFF_OM + M_V_W
OFF_KA = OFF_QA + A_Q_W
OFF_VA = OFF_KA + A_KV_W
OFF_BGM = OFF_VA + A_KV_W
OFF_BGA = OFF_BGM + D_MODEL
PROJ_ALIGNED_W = OFF_BGA + D_MODEL
CTX_OFF_KM = 0
CTX_OFF_VM = M_QK_W
CTX_OFF_KA = M_QK_W + M_V_W
CTX_OFF_VA = CTX_OFF_KA + A_KV_W
CTX_PROJ_W = CTX_OFF_VA + A_KV_W

M_CHUNK = 256
Q_SCALE = M_DQK ** -0.5
LANES = 128
GATE_PAD = LANES
MOD_ROWS = 8
VMEM_LIMIT = 60000 * 1024


def _params(n_axes):
    return pltpu.CompilerParams(dimension_semantics=("arbitrary",) * n_axes, vmem_limit_bytes=VMEM_LIMIT)


def _sigmoid(x):
    return 1.0 / (1.0 + jnp.exp(-x))


def _log_sigmoid(x):
    return jnp.minimum(x, 0.0) - jnp.log1p(jnp.exp(-jnp.abs(x)))


def _rms(x):
    return x * lax.rsqrt(jnp.mean(x * x, axis=-1, keepdims=True) + EPS)


def _mod_kernel(c_ref, w_ref, b_ref, o_ref):
    c = c_ref[...]
    sc = (c * _sigmoid(c)).astype(BF16)
    o_ref[...] = jnp.dot(sc, w_ref[...].astype(BF16), preferred_element_type=F32) + b_ref[...]


def _modulation(c_all, w_mod, b_mod, tn=1024):
    d, n = w_mod.shape
    return pl.pallas_call(
        _mod_kernel,
        grid=(n // tn,),
        in_specs=[pl.BlockSpec((MOD_ROWS, d), lambda j: (0, 0)),
                  pl.BlockSpec((d, tn), lambda j: (0, j)),
                  pl.BlockSpec((1, tn), lambda j: (0, j))],
        out_specs=pl.BlockSpec((MOD_ROWS, tn), lambda j: (0, j)),
        out_shape=jax.ShapeDtypeStruct((MOD_ROWS, n), F32),
        compiler_params=_params(1),
        name="modulation",
    )(c_all, w_mod, b_mod.reshape(1, n))


def _prenorm_kernel(x_ref, g_ref, sh_ref, sc_ref, o_ref):
    y = _rms(x_ref[...]) * g_ref[...]
    o_ref[...] = (y * (1.0 + sc_ref[...]) + sh_ref[...]).astype(o_ref.dtype)


def _mod_spec(slot, rows_per_mod_row, tm, fixed_row=None):
    if fixed_row is None:
        return pl.BlockSpec((None, None, 1, D_MODEL), lambda i: ((i * tm) // rows_per_mod_row, slot, 0, 0))
    return pl.BlockSpec((None, None, 1, D_MODEL), lambda i: (fixed_row, slot, 0, 0))


def _norm_spec(slot):
    return pl.BlockSpec((None, 1, D_MODEL), lambda i: (slot, 0, 0))


def _prenorm(x2, norm_g, mod, rows_per_mod_row, fixed_row=None, tm=512):
    m = x2.shape[0]
    return pl.pallas_call(
        _prenorm_kernel,
        grid=(m // tm,),
        in_specs=[pl.BlockSpec((tm, D_MODEL), lambda i: (i, 0)),
                  _norm_spec(0),
                  _mod_spec(0, rows_per_mod_row, tm, fixed_row),
                  _mod_spec(1, rows_per_mod_row, tm, fixed_row)],
        out_specs=pl.BlockSpec((tm, D_MODEL), lambda i: (i, 0)),
        out_shape=jax.ShapeDtypeStruct((m, D_MODEL), BF16),
        compiler_params=_params(1),
        name="prenorm",
    )(x2, norm_g, mod, mod)


def _post_mix_kernel(x_ref, mix_ref, gn1_ref, gn2_ref, g1_ref, sh2_ref, s2_ref, h1_ref, u2_ref):
    h1 = x_ref[...] + g1_ref[...] * (_rms(mix_ref[...].astype(F32)) * gn1_ref[...])
    h1_ref[...] = h1
    y2 = _rms(h1) * gn2_ref[...]
    u2_ref[...] = (y2 * (1.0 + s2_ref[...]) + sh2_ref[...]).astype(u2_ref.dtype)


def _post_mix(x2, mix, norm_g, mod, rows_per_mod_row, tm=512):
    m = x2.shape[0]
    row = pl.BlockSpec((tm, D_MODEL), lambda i: (i, 0))
    return pl.pallas_call(
        _post_mix_kernel,
        grid=(m // tm,),
        in_specs=[row, row, _norm_spec(1), _norm_spec(2),
                  _mod_spec(2, rows_per_mod_row, tm), _mod_spec(3, rows_per_mod_row, tm),
                  _mod_spec(4, rows_per_mod_row, tm)],
        out_specs=[row, row],
        out_shape=[jax.ShapeDtypeStruct((m, D_MODEL), F32), jax.ShapeDtypeStruct((m, D_MODEL), BF16)],
        compiler_params=_params(1),
        name="post_mix",
    )(x2, mix, norm_g, norm_g, mod, mod, mod)


def _post_mlp_kernel(h1_ref, y_ref, gn_ref, g2_ref, o_ref):
    o_ref[...] = h1_ref[...] + g2_ref[...] * (_rms(y_ref[...].astype(F32)) * gn_ref[...])


def _post_mlp(h1, y, norm_g, mod, rows_per_mod_row, tm=512):
    m = h1.shape[0]
    row = pl.BlockSpec((tm, D_MODEL), lambda i: (i, 0))
    return pl.pallas_call(
        _post_mlp_kernel,
        grid=(m // tm,),
        in_specs=[row, row, _norm_spec(3), _mod_spec(5, rows_per_mod_row, tm)],
        out_specs=row,
        out_shape=jax.ShapeDtypeStruct((m, D_MODEL), F32),
        compiler_params=_params(1),
        name="post_mlp",
    )(h1, y, norm_g, mod)


_NT = (((1,), (1,)), ((), ()))


def _stream_kernel(a_ref, slab_ref, *rest, nj, transposed, tile_of, epilogue, n_extra, has_next, has_side):
    j = pl.program_id(0)
    i = pl.program_id(1)
    rest = list(rest)
    next_ref = rest.pop(0) if has_next else None
    extra = [rest.pop(0) for _ in range(n_extra)]
    side_ref = rest.pop(0) if has_side else None
    o_ref = rest.pop(0)
    side_out_ref = rest.pop(0) if has_side else None
    w_scr = rest.pop(0)
    rows = slab_ref.shape[0]
    r0 = pl.multiple_of(i * rows, rows)

    def cast_slab():
        dst = w_scr.at[j % 2]
        if has_next:
            tile_rows = w_scr.shape[1]
            shifted = tile_of(jnp.minimum(j, nj - 1)) * tile_rows >= W_IN_GATE_LO
            body = rows - N_GATES
            start = pl.multiple_of(jnp.where(shifted, N_GATES, 0), N_GATES)
            dst[pl.ds(r0, body), :] = slab_ref[pl.ds(start, body), :].astype(BF16)
            tail = jnp.where(shifted, next_ref[...], slab_ref[body:, :])
            dst[pl.ds(r0 + body, N_GATES), :] = tail.astype(BF16)
        else:
            dst[pl.ds(r0, rows), :] = slab_ref[...].astype(BF16)

    @pl.when(j == 0)
    def _():
        cast_slab()

    @pl.when(j > 0)
    def _():
        cast_slab()
        w = w_scr[(j - 1) % 2]
        if transposed:
            acc = lax.dot_general(a_ref[...], w, _NT, preferred_element_type=F32)
        else:
            acc = jnp.dot(a_ref[...], w, preferred_element_type=F32)
        o_ref[...] = epilogue(acc, *[e[...] for e in extra]).astype(o_ref.dtype)
        if has_side:
            side_out_ref[...] = side_ref[...].astype(side_out_ref.dtype)


def _stream_matmul(a, w, *, n_out, out_dtype, epilogue=lambda acc: acc, extra=(), extra_off=(), transposed=False,
                   tile_of=None, side=None, tm=1024, tn=1024, name):
    m, k = a.shape
    nj, ni = n_out // tn, m // tm
    tile_of = (lambda j: j) if tile_of is None else tile_of

    def prev(j):
        return jnp.maximum(j - 1, 0)

    def cur(j):
        return jnp.minimum(j, nj - 1)

    def row(j, i):
        return jnp.where(j == 0, 0, i)

    in_specs = [pl.BlockSpec((tm, k), lambda j, i: (row(j, i), 0))]
    args = [a]
    if transposed:
        slab_rows = tn // ni
        in_specs.append(pl.BlockSpec((slab_rows, k), lambda j, i: (tile_of(cur(j)) * ni + i, 0)))
        in_specs.append(pl.BlockSpec(
            (N_GATES, k), lambda j, i: ((tile_of(cur(j)) * tn + (i + 1) * slab_rows) // N_GATES, 0)))
        args += [w, w]
        scratch = pltpu.VMEM((2, tn, k), BF16)
    else:
        in_specs.append(pl.BlockSpec((None, k // ni, tn), lambda j, i: (0, i, cur(j))))
        args.append(w)
        scratch = pltpu.VMEM((2, k, tn), BF16)
    for arr, off in zip(extra, extra_off):
        in_specs.append(pl.BlockSpec((tm, tn), lambda j, i, off=off: (row(j, i), off // tn + prev(j))))
        args.append(arr)
    out_specs = [pl.BlockSpec((tm, tn), lambda j, i: (row(j, i), prev(j)))]
    out_shape = [jax.ShapeDtypeStruct((m, n_out), out_dtype)]
    if side is not None:
        s_rows, s_cols = side.shape[1:]
        slab = s_rows // (nj * ni)

        def step(j, i):
            return jnp.maximum((j - 1) * ni + i, 0)

        in_specs.append(pl.BlockSpec((None, slab, s_cols), lambda j, i: (0, step(j, i), 0)))
        args.append(side)
        out_specs.append(pl.BlockSpec((slab, s_cols), lambda j, i: (step(j, i), 0)))
        out_shape.append(jax.ShapeDtypeStruct((s_rows, s_cols), BF16))
    out = pl.pallas_call(
        functools.partial(_stream_kernel, nj=nj, transposed=transposed, tile_of=tile_of, epilogue=epilogue,
                          n_extra=len(extra), has_next=transposed, has_side=side is not None),
        grid=(nj + 1, ni),
        in_specs=in_specs,
        out_specs=out_specs,
        out_shape=out_shape,
        scratch_shapes=[scratch],
        compiler_params=_params(2),
        name=name,
    )(*args)
    return out if side is not None else out[0]


def _gate_epilogue(acc, g):
    return _sigmoid(g.astype(F32)) * acc


def _gate_add_epilogue(acc, g, prev):
    return prev.astype(F32) + _sigmoid(g.astype(F32)) * acc


def _relu2_epilogue(acc):
    return jnp.square(jnp.maximum(acc, 0.0))


def _gate_table(g):
    length = g.shape[0]
    lf = _log_sigmoid(g)
    row = lax.broadcasted_iota(jnp.int32, g.shape, 0)
    col = lax.broadcasted_iota(jnp.int32, g.shape, 1)
    prefix = lf
    shift = 1
    while shift < length:
        prefix = prefix + jnp.where(row >= shift, pltpu.roll(prefix, shift, 0), 0.0)
        shift *= 2
    suffix = prefix[length - 1:length, :] - prefix + lf
    is_f_fwd = (col >= M_HEADS) & (col < 2 * M_HEADS)
    is_f_bwd = (col >= 3 * M_HEADS) & (col < 4 * M_HEADS)
    return jnp.where(is_f_fwd, prefix, jnp.where(is_f_bwd, suffix, g))


def _gate_proj_kernel(a_ref, w_ref, b_ref, o_ref):
    g = lax.dot_general(a_ref[...], w_ref[...], _NT, preferred_element_type=F32) + b_ref[...]
    for c in range(g.shape[0] // M_CHUNK):
        rows = slice(c * M_CHUNK, (c + 1) * M_CHUNK)
        o_ref[rows, :] = _gate_table(g[rows, :])


def _gate_proj(a, w_gate_t, bias, tm=1024, name="gate_proj"):
    m, k = a.shape
    n = w_gate_t.shape[0]
    return pl.pallas_call(
        _gate_proj_kernel,
        grid=(m // tm,),
        in_specs=[pl.BlockSpec((tm, k), lambda i: (i, 0)),
                  pl.BlockSpec((n, k), lambda i: (0, 0)),
                  pl.BlockSpec((1, n), lambda i: (0, 0))],
        out_specs=pl.BlockSpec((tm, n), lambda i: (i, 0)),
        out_shape=jax.ShapeDtypeStruct((m, n), F32),
        compiler_params=_params(1),
        name=name,
    )(a, w_gate_t, bias)


def _matmul_ksplit_kernel(a_ref, w_ref, o_ref, acc_ref):
    kk = pl.program_id(2)

    @pl.when(kk == 0)
    def _():
        acc_ref[...] = jnp.zeros_like(acc_ref)

    acc_ref[...] += jnp.dot(a_ref[...], w_ref[...], preferred_element_type=F32)

    @pl.when(kk == pl.num_programs(2) - 1)
    def _():
        o_ref[...] = acc_ref[...].astype(o_ref.dtype)


def _matmul_ksplit(a, w, *, tm, tn, tk, name):
    m, k = a.shape
    n = w.shape[1]
    return pl.pallas_call(
        _matmul_ksplit_kernel,
        grid=(n // tn, m // tm, k // tk),
        in_specs=[pl.BlockSpec((tm, tk), lambda j, i, kk: (i, kk)),
                  pl.BlockSpec((tk, tn), lambda j, i, kk: (kk, j))],
        out_specs=pl.BlockSpec((tm, tn), lambda j, i, kk: (i, j)),
        out_shape=jax.ShapeDtypeStruct((m, n), BF16),
        scratch_shapes=[pltpu.VMEM((tm, tn), F32)],
        compiler_params=_params(3),
        name=name,
    )(a, w)


def _tile_lanes(x, width):
    return jnp.concatenate([x] * (width // x.shape[1]), axis=1)


def _rows_to_replicated_columns(x_row, y_row):
    length = x_row.shape[1]

    def pieces(v):
        hi = v.astype(BF16)
        rest = v - hi.astype(F32)
        mid = rest.astype(BF16)
        return [hi, mid, (rest - mid.astype(F32)).astype(BF16)]

    n_rows = 16
    stacked = jnp.concatenate(
        pieces(x_row) + pieces(y_row) + [jnp.zeros((n_rows - 6, length), BF16)], axis=0)
    piece = lax.broadcasted_iota(jnp.int32, (n_rows, 2 * LANES), 0)
    lane = lax.broadcasted_iota(jnp.int32, (n_rows, 2 * LANES), 1)
    selector = jnp.where((lane < LANES) == (piece < 3), 1.0, 0.0)
    selector = jnp.where(piece < 6, selector, 0.0).astype(BF16)
    both = lax.dot_general(stacked, selector, (((0,), (0,)), ((), ())), preferred_element_type=F32)
    return both[:, :LANES], both[:, LANES:]


def _mlstm_kernel(q_ref, k_ref, v_ref, g_ref, kc_ref, vc_ref, gc_ref, hf_ref, hb_ref, *scratch):
    length = M_CHUNK
    dv = M_DV
    nc = q_ref.shape[0] // length
    per_dir = len(scratch) // 2
    dirs = [(False, hf_ref) + tuple(scratch[:per_dir]), (True, hb_ref) + tuple(scratch[per_dir:])]

    def rows(reverse, t):
        r = (nc - 1 - t) if reverse else t
        return r, pl.ds(pl.multiple_of(r * length, length), length)

    def gate_vectors(g4, reverse):
        ji, jf = (2, 3) if reverse else (0, 1)
        ig_r = g4[ji:ji + 1, :]
        b_r = g4[jf:jf + 1, :]
        b_tot = b_r[:, 0:1] if reverse else b_r[:, length - 1:length]
        b_c, ig_c = _rows_to_replicated_columns(b_r, ig_r)
        return ig_r, b_r, b_tot, b_c, ig_c

    def update_state(ct_ref, m_ref, k, v, b_tot, b_c, ig_c):
        m_prev = m_ref[...]
        w_c = b_tot - b_c + ig_c
        m_new = jnp.maximum(b_tot + m_prev, jnp.max(w_c, axis=0, keepdims=True)[:, 0:1])
        decay = jnp.exp(b_tot + m_prev - m_new)
        kw = k.astype(F32) * _tile_lanes(jnp.exp(w_c - m_new), k.shape[1])
        v_aug = jnp.concatenate([v, jnp.ones((length, LANES), BF16)], axis=1)
        ct_ref[...] = decay * ct_ref[...] + lax.dot_general(
            kw.astype(BF16), v_aug, (((0,), (0,)), ((), ())), preferred_element_type=F32)
        m_ref[...] = m_new

    def score_matmul(d, t, slot):
        reverse, _, _, _, qk_buf = d[:5]
        _, rs = rows(reverse, t)
        qk_buf[slot] = lax.dot_general(q_ref[rs, :], k_ref[rs, :], _NT, preferred_element_type=F32)

    def state_matmul(d, t):
        reverse, _, ct_ref, _, _, qc_buf = d[:6]
        _, rs = rows(reverse, t)
        ct = ct_ref[...]
        n_hi = ct[:, dv:].astype(BF16)
        n_mid = (ct[:, dv:] - n_hi.astype(F32)).astype(BF16)
        qc_buf[...] = jnp.dot(q_ref[rs, :], jnp.concatenate([ct[:, :dv].astype(BF16), n_hi, n_mid], axis=1),
                              preferred_element_type=F32)

    def gates(d, t, gate_vecs, qk):
        reverse, _, _, m_ref, _, qc_buf, s_buf, inter_buf, scale_buf = d
        ig_r, b_r, _, b_c, _ = gate_vecs
        t_idx = lax.broadcasted_iota(jnp.int32, (length, length), 0)
        s_idx = lax.broadcasted_iota(jnp.int32, (length, length), 1)
        seen = (s_idx >= t_idx) if reverse else (s_idx <= t_idx)
        a_c = b_c + m_ref[...]
        dm = jnp.where(seen, _tile_lanes(b_c, length) - b_r + ig_r, -jnp.inf)
        m_t = jnp.maximum(a_c, jnp.broadcast_to(jnp.max(dm, axis=1, keepdims=True), (length, LANES)))
        s = qk * Q_SCALE * jnp.exp(dm - _tile_lanes(m_t, length))
        inter = jnp.exp(a_c - m_t) * Q_SCALE
        qc = qc_buf[...]
        qn = qc[:, dv:dv + LANES] + qc[:, dv + LANES:]
        s_sum = jnp.broadcast_to(jnp.sum(s, axis=1, keepdims=True), (length, LANES))
        den = inter * qn + s_sum
        s_buf[...] = s.astype(s_buf.dtype)
        inter_buf[...] = _tile_lanes(inter, dv) * qc[:, :dv]
        scale_buf[...] = 1.0 / jnp.maximum(jnp.abs(den), jnp.exp(-m_t))

    def outputs(d, t):
        reverse, h_ref = d[:2]
        s_buf, inter_buf, scale_buf = d[6:]
        _, rs = rows(reverse, t)
        sv = jnp.dot(s_buf[...], v_ref[rs, :], preferred_element_type=F32)
        h_ref[rs, :] = ((inter_buf[...] + sv) * _tile_lanes(scale_buf[...], dv)).astype(h_ref.dtype)

    def step(t, slot, first=False, last=False):
        if not first:
            for d in dirs:
                outputs(d, t - 1)
        vecs = [gate_vectors(g_ref[rows(d[0], t)[0]], d[0]) for d in dirs]
        if not last:
            for d in dirs:
                score_matmul(d, t + 1, 1 - slot)
        for d, gv in zip(dirs, vecs):
            gates(d, t, gv, d[4][slot])
        if not last:
            for d, gv in zip(dirs, vecs):
                _, rs = rows(d[0], t)
                update_state(d[2], d[3], k_ref[rs, :], v_ref[rs, :], gv[2], gv[3], gv[4])
            for d in dirs:
                state_matmul(d, t + 1)

    for d in dirs:
        d[2][...] = jnp.zeros_like(d[2])
        d[3][...] = jnp.zeros_like(d[3])
        _, _, b_tot, b_c, ig_c = gate_vectors(gc_ref[...], d[0])
        update_state(d[2], d[3], kc_ref[...], vc_ref[...], b_tot, b_c, ig_c)
    for d in dirs:
        score_matmul(d, 0, 0)
        state_matmul(d, 0)
    step(0, 0, first=True)

    def body(pair, carry):
        step(2 * pair + 1, 1)
        step(2 * pair + 2, 0)
        return carry

    assert nc % 2 == 0
    lax.fori_loop(0, (nc - 2) // 2, body, 0)
    step(nc - 1, 1, last=True)
    for d in dirs:
        outputs(d, nc - 1)


def _mlstm(proj, proj_ctx, gates, gates_ctx, batch):
    length = M_CHUNK
    assert CTX_LEN == length
    seq = proj.shape[0] // batch
    nc = seq // length
    out_shape = jax.ShapeDtypeStruct((batch * seq, M_V_W), BF16)
    out_spec = pl.BlockSpec((seq, M_DV), lambda b, h: (b, h))
    per_dir = [pltpu.VMEM((M_DQK, M_DV + LANES), F32),
               pltpu.VMEM((1, 1), F32),
               pltpu.VMEM((2, length, length), F32),
               pltpu.VMEM((length, M_DV + 2 * LANES), F32),
               pltpu.VMEM((length, length), BF16),
               pltpu.VMEM((length, M_DV), F32),
               pltpu.VMEM((length, LANES), F32)]
    return pl.pallas_call(
        _mlstm_kernel,
        grid=(batch, M_HEADS),
        in_specs=[pl.BlockSpec((seq, M_DQK), lambda b, h: (b, OFF_QM // M_DQK + h)),
                  pl.BlockSpec((seq, M_DQK), lambda b, h: (b, OFF_KM // M_DQK + h)),
                  pl.BlockSpec((seq, M_DV), lambda b, h: (b, OFF_VM // M_DV + h)),
                  pl.BlockSpec((None, None, nc, 4, length), lambda b, h: (b, h, 0, 0, 0)),
                  pl.BlockSpec((length, M_DQK), lambda b, h: (b, CTX_OFF_KM // M_DQK + h)),
                  pl.BlockSpec((length, M_DV), lambda b, h: (b, CTX_OFF_VM // M_DV + h)),
                  pl.BlockSpec((None, None, 4, length), lambda b, h: (b, h, 0, 0))],
        out_specs=[out_spec, out_spec],
        out_shape=[out_shape, out_shape],
        scratch_shapes=per_dir + per_dir,
        compiler_params=_params(2),
        name="mlstm",
    )(proj, proj, proj, gates, proj_ctx, proj_ctx, gates_ctx)


def _readout_kernel(hf_ref, hb_ref, om_ref, g_ref, o_ref):
    for h in range(M_HEADS):
        cols = slice(h * M_DV, (h + 1) * M_DV)
        x = hf_ref[:, cols].astype(F32) + hb_ref[:, cols].astype(F32)
        y = _rms(x) * g_ref[:, cols] * _sigmoid(om_ref[:, cols].astype(F32))
        o_ref[:, cols] = y.astype(o_ref.dtype)


def _readout(hf, hb, proj, m_norm_g, tm=256):
    m = hf.shape[0]
    row = pl.BlockSpec((tm, M_V_W), lambda i: (i, 0))
    return pl.pallas_call(
        _readout_kernel,
        grid=(m // tm,),
        in_specs=[row, row,
                  pl.BlockSpec((tm, M_V_W), lambda i: (i, OFF_OM // M_V_W)),
                  pl.BlockSpec((1, M_V_W), lambda i: (0, 0))],
        out_specs=row,
        out_shape=jax.ShapeDtypeStruct((m, M_V_W), BF16),
        compiler_params=_params(1),
        name="mlstm_readout",
    )(hf, hb, proj, m_norm_g.reshape(1, M_V_W))


def _rope(x, cos, sin_signed):
    lane = lax.broadcasted_iota(jnp.int32, x.shape, 1)
    partner = jnp.where((lane % 64) < 32, pltpu.roll(x, LANES - 32, 1), pltpu.roll(x, 32, 1))
    return x * cos + partner * sin_signed


def _attn_kernel(sink_ref, q_ref, k_ref, v_ref, kc_ref, vc_ref, cos_ref, sin_ref, o_ref,
                 kpad, vpad, s_buf, p_buf, den_buf):
    seq = k_ref.shape[0]
    nb = seq // A_BLOCK
    kvh = pl.program_id(1)
    scale = A_HEAD_DIM ** -0.5
    rows = A_GROUP * A_BLOCK
    n_win = 3 * A_BLOCK
    zeros = jnp.zeros((A_BLOCK, A_HEAD_DIM), BF16)
    kpad[0:A_BLOCK, :] = zeros
    kpad[A_BLOCK + seq:, :] = zeros
    vpad[0:A_BLOCK, :] = zeros
    vpad[A_BLOCK + seq:, :] = zeros
    kpad[A_BLOCK:A_BLOCK + seq, :] = _rope(k_ref[...].astype(F32), cos_ref[...], sin_ref[...]).astype(BF16)
    vpad[A_BLOCK:A_BLOCK + seq, :] = v_ref[...]
    qi = lax.broadcasted_iota(jnp.int32, (rows, n_win), 0) & (A_BLOCK - 1)
    kj = lax.broadcasted_iota(jnp.int32, (rows, n_win), 1)
    band = (kj >= qi) & (kj <= qi + 2 * WINDOW)
    kj_row = lax.broadcasted_iota(jnp.int32, (1, n_win), 1)
    sink = jnp.concatenate(
        [jnp.full((A_BLOCK, 1), sink_ref[kvh * A_GROUP + g], F32) for g in range(A_GROUP)], axis=0)
    nt = (((1,), (1,)), ((), ()))

    def scores(n):
        r0 = pl.multiple_of(n * A_BLOCK, A_BLOCK)
        cos = cos_ref[pl.ds(r0, A_BLOCK), :]
        sin = sin_ref[pl.ds(r0, A_BLOCK), :]
        q = jnp.concatenate(
            [_rope(q_ref[pl.ds(r0, A_BLOCK), g * A_HEAD_DIM:(g + 1) * A_HEAD_DIM].astype(F32), cos, sin)
             for g in range(A_GROUP)], axis=0).astype(BF16)
        kpos = r0 - A_BLOCK + kj_row
        valid = band & ((kpos >= 0) & (kpos < seq))
        sw = lax.dot_general(q, kpad[pl.ds(r0, n_win), :], nt, preferred_element_type=F32) * scale
        s_buf[:, 0:n_win] = jnp.where(valid, sw, -jnp.inf)
        s_buf[:, n_win:] = lax.dot_general(q, kc_ref[...], nt, preferred_element_type=F32) * scale

    def softmax():
        s = s_buf[...]
        mx = jnp.maximum(jnp.max(s, axis=1, keepdims=True), sink)
        p = jnp.exp(s - mx)
        den_buf[...] = jnp.sum(p, axis=1, keepdims=True) + jnp.exp(sink - mx)
        p_buf[...] = p.astype(p_buf.dtype)

    def output(n):
        r0 = pl.multiple_of(n * A_BLOCK, A_BLOCK)
        o = (jnp.dot(p_buf[:, 0:n_win], vpad[pl.ds(r0, n_win), :], preferred_element_type=F32)
             + jnp.dot(p_buf[:, n_win:], vc_ref[...], preferred_element_type=F32)) * (1.0 / den_buf[...])
        for g in range(A_GROUP):
            o_ref[pl.ds(r0, A_BLOCK), g * A_HEAD_DIM:(g + 1) * A_HEAD_DIM] = (
                o[g * A_BLOCK:(g + 1) * A_BLOCK, :].astype(o_ref.dtype))

    scores(0)
    softmax()
    scores(1)

    def body(t, carry):
        output(t - 1)
        softmax()
        scores(t + 1)
        return carry

    lax.fori_loop(1, nb - 1, body, 0)
    output(nb - 2)
    softmax()
    output(nb - 1)


def _attention(proj, proj_ctx, sink, cos, sin, batch):
    seq = proj.shape[0] // batch
    qw = A_GROUP * A_HEAD_DIM
    tab = pl.BlockSpec((seq, A_HEAD_DIM), lambda b, h: (0, 0))
    return pl.pallas_call(
        _attn_kernel,
        grid=(batch, A_KV_HEADS),
        in_specs=[pl.BlockSpec(memory_space=pltpu.SMEM),
                  pl.BlockSpec((seq, qw), lambda b, h: (b, OFF_QA // qw + h)),
                  pl.BlockSpec((seq, A_HEAD_DIM), lambda b, h: (b, OFF_KA // A_HEAD_DIM + h)),
                  pl.BlockSpec((seq, A_HEAD_DIM), lambda b, h: (b, OFF_VA // A_HEAD_DIM + h)),
                  pl.BlockSpec((CTX_LEN, A_HEAD_DIM), lambda b, h: (b, CTX_OFF_KA // A_HEAD_DIM + h)),
                  pl.BlockSpec((CTX_LEN, A_HEAD_DIM), lambda b, h: (b, CTX_OFF_VA // A_HEAD_DIM + h)),
                  tab, tab],
        out_specs=pl.BlockSpec((seq, qw), lambda b, h: (b, h)),
        out_shape=jax.ShapeDtypeStruct((batch * seq, A_Q_W), BF16),
        scratch_shapes=[pltpu.VMEM((seq + 2 * A_BLOCK, A_HEAD_DIM), BF16),
                        pltpu.VMEM((seq + 2 * A_BLOCK, A_HEAD_DIM), BF16),
                        pltpu.VMEM((A_GROUP * A_BLOCK, 3 * A_BLOCK + CTX_LEN), F32),
                        pltpu.VMEM((A_GROUP * A_BLOCK, 3 * A_BLOCK + CTX_LEN), BF16),
                        pltpu.VMEM((A_GROUP * A_BLOCK, 1), F32)],
        compiler_params=_params(2),
        name="window_attention",
    )(sink, proj, proj, proj, proj_ctx, proj_ctx, cos, sin)


def _rope_tables(seq):
    t = jnp.arange(seq)
    nf = A_HEAD_DIM // 4
    freqs = ROPE_BASE ** (-jnp.arange(nf, dtype=F32) / nf)
    ang_r = (t // GRID_W).astype(F32)[:, None] * freqs
    ang_c = (t % GRID_W).astype(F32)[:, None] * freqs
    cos = jnp.concatenate([jnp.cos(ang_r), jnp.cos(ang_r), jnp.cos(ang_c), jnp.cos(ang_c)], axis=-1)
    sin = jnp.concatenate([-jnp.sin(ang_r), jnp.sin(ang_r), -jnp.sin(ang_c), jnp.sin(ang_c)], axis=-1)
    return cos, sin


def _gate_rows(g, batch):
    return g[:, :N_GATES].reshape(batch, -1, 4, M_HEADS).transpose(0, 3, 2, 1)


def kernel(x, c, ctx, c_ctx, w_mod, b_mod, norm_g, w_in, m_gate_b, m_norm_g, attn_sink,
           w_out_m, w_out_a, w_o, w_ff1, w_ff2):
    batch, seq, d = x.shape
    n_ctx = ctx.shape[1]
    x2 = x.reshape(batch * seq, d)
    ctx2 = ctx.reshape(batch * n_ctx, d)

    w_in_t = w_in[0].T
    w_gate_t = jnp.pad(w_in_t[W_IN_GATE_LO:W_IN_GATE_HI], ((0, GATE_PAD - N_GATES), (0, 0))).astype(BF16)
    gate_bias = jnp.pad(m_gate_b[0].reshape(1, N_GATES), ((0, 0), (0, GATE_PAD - N_GATES)))

    c_all = jnp.zeros((MOD_ROWS, d), F32).at[:batch].set(c).at[batch].set(c_ctx)
    mod = _modulation(c_all, w_mod[0], b_mod[0]).reshape(MOD_ROWS, 6, 1, d)
    gn = norm_g[0].reshape(4, 1, d)

    u = _prenorm(x2, gn, mod, seq)
    uc = _prenorm(ctx2, gn, mod, n_ctx, fixed_row=batch)

    proj = _stream_matmul(u, w_in_t, n_out=PROJ_ALIGNED_W, out_dtype=BF16, transposed=True, name="in_proj")
    g_lat = _gate_proj(u, w_gate_t, gate_bias)
    ctx_tn = 512
    lat_tiles = (OFF_OM - OFF_KM) // ctx_tn
    proj_ctx = _stream_matmul(
        uc, w_in_t, n_out=CTX_PROJ_W, out_dtype=BF16, transposed=True, tn=ctx_tn, name="in_proj_ctx",
        tile_of=lambda j: jnp.where(j < lat_tiles, j + OFF_KM // ctx_tn, j - lat_tiles + OFF_KA // ctx_tn))
    g_ctx = _gate_proj(uc, w_gate_t, gate_bias, name="gate_proj_ctx")

    g_rows = _gate_rows(g_lat, batch)
    g_chunks = g_rows.reshape(batch, M_HEADS, 4, seq // M_CHUNK, M_CHUNK).transpose(0, 1, 3, 2, 4)
    hf, hb = _mlstm(proj, proj_ctx, g_chunks, _gate_rows(g_ctx, batch), batch)
    hm = _readout(hf, hb, proj, m_norm_g[0])

    cos, sin = _rope_tables(seq)
    at = _attention(proj, proj_ctx, attn_sink[0], cos, sin, batch)

    zm = _stream_matmul(hm, w_out_m, n_out=d, out_dtype=BF16, epilogue=_gate_epilogue,
                        extra=[proj], extra_off=[OFF_BGM], name="out_m")
    z = _stream_matmul(at, w_out_a, n_out=d, out_dtype=BF16, epilogue=_gate_add_epilogue,
                       extra=[proj, zm], extra_off=[OFF_BGA, 0], name="out_a_merge")
    mix = _stream_matmul(z, w_o, n_out=d, out_dtype=BF16, name="out_proj")
    h1, u2 = _post_mix(x2, mix, gn, mod, seq)

    hid, w_ff2_bf16 = _stream_matmul(u2, w_ff1, n_out=D_FF, out_dtype=BF16, epilogue=_relu2_epilogue,
                                     side=w_ff2, name="ff1")
    y = _matmul_ksplit(hid, w_ff2_bf16, tm=1024, tn=1024, tk=4096, name="ff2")
    out = _post_mlp(h1, y, gn, mod, seq)
    return out.reshape(batch, seq, d)
```

```python
import functools

import jax
import jax.numpy as jnp
from jax import lax
from jax.experimental import pallas as pl
from jax.experimental.pallas import tpu as pltpu

F32 = jnp.float32
BF16 = jnp.bfloat16

D_MODEL = 4096
GRID_W = 64
CTX_LEN = 256
M_HEADS = 8
M_DQK = 256
M_DV = 512
A_HEADS = 32
A_KV_HEADS = 8
A_HEAD_DIM = 128
A_GROUP = A_HEADS // A_KV_HEADS
WINDOW = 128
A_BLOCK = 128
ROPE_BASE = 10000.0
D_FF = 4 * D_MODEL
EPS = 1e-6

M_QK_W = M_HEADS * M_DQK
M_V_W = M_HEADS * M_DV
A_Q_W = A_HEADS * A_HEAD_DIM
A_KV_W = A_KV_HEADS * A_HEAD_DIM
N_GATES = 4 * M_HEADS

W_IN_GATE_LO = 2 * M_QK_W + 2 * M_V_W
W_IN_GATE_HI = W_IN_GATE_LO + N_GATES
OFF_QM = 0
OFF_KM = OFF_QM + M_QK_W
OFF_VM = OFF_KM + M_QK_W
OFF_OM = OFF_VM + M_V_W
OFF_QA = OFF_OM + M_V_W
OFF_KA = OFF_QA + A_Q_W
OFF_VA = OFF_KA + A_KV_W
OFF_BGM = OFF_VA + A_KV_W
OFF_BGA = OFF_BGM + D_MODEL
PROJ_ALIGNED_W = OFF_BGA + D_MODEL
CTX_OFF_KM = 0
CTX_OFF_VM = M_QK_W
CTX_OFF_KA = M_QK_W + M_V_W
CTX_OFF_VA = CTX_OFF_KA + A_KV_W
CTX_PROJ_W = CTX_OFF_VA + A_KV_W

M_CHUNK = 256
M_HEADS_PER_STEP = 4
Q_SCALE = M_DQK ** -0.5
LANES = 128
GATE_PAD = LANES
MOD_ROWS = 8
VMEM_LIMIT = 60000 * 1024


def _params(n_axes):
    return pltpu.CompilerParams(dimension_semantics=("arbitrary",) * n_axes, vmem_limit_bytes=VMEM_LIMIT)


def _sigmoid(x):
    return 1.0 / (1.0 + jnp.exp(-x))


def _log_sigmoid(x):
    return jnp.minimum(x, 0.0) - jnp.log1p(jnp.exp(-jnp.abs(x)))


def _rms(x):
    return x * lax.rsqrt(jnp.mean(x * x, axis=-1, keepdims=True) + EPS)


def _mod_kernel(c_ref, w_ref, b_ref, o_ref):
    c = c_ref[...]
    sc = (c * _sigmoid(c)).astype(BF16)
    o_ref[...] = jnp.dot(sc, w_ref[...].astype(BF16), preferred_element_type=F32) + b_ref[...]


def _modulation(c_all, w_mod, b_mod, tn=1024):
    d, n = w_mod.shape
    return pl.pallas_call(
        _mod_kernel,
        grid=(n // tn,),
        in_specs=[pl.BlockSpec((MOD_ROWS, d), lambda j: (0, 0)),
                  pl.BlockSpec((d, tn), lambda j: (0, j)),
                  pl.BlockSpec((1, tn), lambda j: (0, j))],
        out_specs=pl.BlockSpec((MOD_ROWS, tn), lambda j: (0, j)),
        out_shape=jax.ShapeDtypeStruct((MOD_ROWS, n), F32),
        compiler_params=_params(1),
        name="modulation",
    )(c_all, w_mod, b_mod.reshape(1, n))


def _prenorm_kernel(x_ref, g_ref, sh_ref, sc_ref, o_ref):
    y = _rms(x_ref[...]) * g_ref[...]
    o_ref[...] = (y * (1.0 + sc_ref[...]) + sh_ref[...]).astype(o_ref.dtype)


def _mod_spec(slot, rows_per_mod_row, tm, fixed_row=None):
    if fixed_row is None:
        return pl.BlockSpec((None, None, 1, D_MODEL), lambda i: ((i * tm) // rows_per_mod_row, slot, 0, 0))
    return pl.BlockSpec((None, None, 1, D_MODEL), lambda i: (fixed_row, slot, 0, 0))


def _norm_spec(slot):
    return pl.BlockSpec((None, 1, D_MODEL), lambda i: (slot, 0, 0))


def _prenorm(x2, norm_g, mod, rows_per_mod_row, fixed_row=None, tm=512):
    m = x2.shape[0]
    return pl.pallas_call(
        _prenorm_kernel,
        grid=(m // tm,),
        in_specs=[pl.BlockSpec((tm, D_MODEL), lambda i: (i, 0)),
                  _norm_spec(0),
                  _mod_spec(0, rows_per_mod_row, tm, fixed_row),
                  _mod_spec(1, rows_per_mod_row, tm, fixed_row)],
        out_specs=pl.BlockSpec((tm, D_MODEL), lambda i: (i, 0)),
        out_shape=jax.ShapeDtypeStruct((m, D_MODEL), BF16),
        compiler_params=_params(1),
        name="prenorm",
    )(x2, norm_g, mod, mod)


def _post_mix_kernel(x_ref, mix_ref, gn1_ref, gn2_ref, g1_ref, sh2_ref, s2_ref, h1_ref, u2_ref):
    h1 = x_ref[...] + g1_ref[...] * (_rms(mix_ref[...].astype(F32)) * gn1_ref[...])
    h1_ref[...] = h1
    y2 = _rms(h1) * gn2_ref[...]
    u2_ref[...] = (y2 * (1.0 + s2_ref[...]) + sh2_ref[...]).astype(u2_ref.dtype)


def _post_mix(x2, mix, norm_g, mod, rows_per_mod_row, tm=512):
    m = x2.shape[0]
    row = pl.BlockSpec((tm, D_MODEL), lambda i: (i, 0))
    return pl.pallas_call(
        _post_mix_kernel,
        grid=(m // tm,),
        in_specs=[row, row, _norm_spec(1), _norm_spec(2),
                  _mod_spec(2, rows_per_mod_row, tm), _mod_spec(3, rows_per_mod_row, tm),
                  _mod_spec(4, rows_per_mod_row, tm)],
        out_specs=[row, row],
        out_shape=[jax.ShapeDtypeStruct((m, D_MODEL), F32), jax.ShapeDtypeStruct((m, D_MODEL), BF16)],
        compiler_params=_params(1),
        name="post_mix",
    )(x2, mix, norm_g, norm_g, mod, mod, mod)


def _post_mlp_kernel(h1_ref, y_ref, gn_ref, g2_ref, o_ref):
    o_ref[...] = h1_ref[...] + g2_ref[...] * (_rms(y_ref[...].astype(F32)) * gn_ref[...])


def _post_mlp(h1, y, norm_g, mod, rows_per_mod_row, tm=512):
    m = h1.shape[0]
    row = pl.BlockSpec((tm, D_MODEL), lambda i: (i, 0))
    return pl.pallas_call(
        _post_mlp_kernel,
        grid=(m // tm,),
        in_specs=[row, row, _norm_spec(3), _mod_spec(5, rows_per_mod_row, tm)],
        out_specs=row,
        out_shape=jax.ShapeDtypeStruct((m, D_MODEL), F32),
        compiler_params=_params(1),
        name="post_mlp",
    )(h1, y, norm_g, mod)


_NT = (((1,), (1,)), ((), ()))


def _stream_kernel(a_ref, slab_ref, *rest, nj, transposed, tile_of, epilogue, n_extra, has_next, has_side):
    j = pl.program_id(0)
    i = pl.program_id(1)
    rest = list(rest)
    next_ref = rest.pop(0) if has_next else None
    extra = [rest.pop(0) for _ in range(n_extra)]
    side_ref = rest.pop(0) if has_side else None
    o_ref = rest.pop(0)
    side_out_ref = rest.pop(0) if has_side else None
    w_scr = rest.pop(0)
    rows = slab_ref.shape[0]
    r0 = pl.multiple_of(i * rows, rows)

    def cast_slab():
        dst = w_scr.at[j % 2]
        if has_next:
            tile_rows = w_scr.shape[1]
            shifted = tile_of(jnp.minimum(j, nj - 1)) * tile_rows >= W_IN_GATE_LO
            body = rows - N_GATES
            start = pl.multiple_of(jnp.where(shifted, N_GATES, 0), N_GATES)
            dst[pl.ds(r0, body), :] = slab_ref[pl.ds(start, body), :].astype(BF16)
            tail = jnp.where(shifted, next_ref[...], slab_ref[body:, :])
            dst[pl.ds(r0 + body, N_GATES), :] = tail.astype(BF16)
        else:
            dst[pl.ds(r0, rows), :] = slab_ref[...].astype(BF16)

    @pl.when(j == 0)
    def _():
        cast_slab()

    @pl.when(j > 0)
    def _():
        cast_slab()
        w = w_scr[(j - 1) % 2]
        if transposed:
            acc = lax.dot_general(a_ref[...], w, _NT, preferred_element_type=F32)
        else:
            acc = jnp.dot(a_ref[...], w, preferred_element_type=F32)
        o_ref[...] = epilogue(acc, *[e[...] for e in extra]).astype(o_ref.dtype)
        if has_side:
            side_out_ref[...] = side_ref[...].astype(side_out_ref.dtype)


def _stream_matmul(a, w, *, n_out, out_dtype, epilogue=lambda acc: acc, extra=(), extra_off=(), transposed=False,
                   tile_of=None, side=None, tm=1024, tn=1024, name):
    m, k = a.shape
    nj, ni = n_out // tn, m // tm
    tile_of = (lambda j: j) if tile_of is None else tile_of

    def prev(j):
        return jnp.maximum(j - 1, 0)

    def cur(j):
        return jnp.minimum(j, nj - 1)

    def row(j, i):
        return jnp.where(j == 0, 0, i)

    in_specs = [pl.BlockSpec((tm, k), lambda j, i: (row(j, i), 0))]
    args = [a]
    if transposed:
        slab_rows = tn // ni
        in_specs.append(pl.BlockSpec((slab_rows, k), lambda j, i: (tile_of(cur(j)) * ni + i, 0)))
        in_specs.append(pl.BlockSpec(
            (N_GATES, k), lambda j, i: ((tile_of(cur(j)) * tn + (i + 1) * slab_rows) // N_GATES, 0)))
        args += [w, w]
        scratch = pltpu.VMEM((2, tn, k), BF16)
    else:
        in_specs.append(pl.BlockSpec((None, k // ni, tn), lambda j, i: (0, i, cur(j))))
        args.append(w)
        scratch = pltpu.VMEM((2, k, tn), BF16)
    for arr, off in zip(extra, extra_off):
        in_specs.append(pl.BlockSpec((tm, tn), lambda j, i, off=off: (row(j, i), off // tn + prev(j))))
        args.append(arr)
    out_specs = [pl.BlockSpec((tm, tn), lambda j, i: (row(j, i), prev(j)))]
    out_shape = [jax.ShapeDtypeStruct((m, n_out), out_dtype)]
    if side is not None:
        s_rows, s_cols = side.shape[1:]
        slab = s_rows // (nj * ni)

        def step(j, i):
            return jnp.maximum((j - 1) * ni + i, 0)

        in_specs.append(pl.BlockSpec((None, slab, s_cols), lambda j, i: (0, step(j, i), 0)))
        args.append(side)
        out_specs.append(pl.BlockSpec((slab, s_cols), lambda j, i: (step(j, i), 0)))
        out_shape.append(jax.ShapeDtypeStruct((s_rows, s_cols), BF16))
    out = pl.pallas_call(
        functools.partial(_stream_kernel, nj=nj, transposed=transposed, tile_of=tile_of, epilogue=epilogue,
                          n_extra=len(extra), has_next=transposed, has_side=side is not None),
        grid=(nj + 1, ni),
        in_specs=in_specs,
        out_specs=out_specs,
        out_shape=out_shape,
        scratch_shapes=[scratch],
        compiler_params=_params(2),
        name=name,
    )(*args)
    return out if side is not None else out[0]


def _gate_epilogue(acc, g):
    return _sigmoid(g.astype(F32)) * acc


def _gate_add_epilogue(acc, g, prev):
    return prev.astype(F32) + _sigmoid(g.astype(F32)) * acc


def _relu2_epilogue(acc):
    return jnp.square(jnp.maximum(acc, 0.0))


def _gate_table(g):
    length = g.shape[0]
    lf = _log_sigmoid(g)
    row = lax.broadcasted_iota(jnp.int32, g.shape, 0)
    col = lax.broadcasted_iota(jnp.int32, g.shape, 1)
    prefix = lf
    shift = 1
    while shift < length:
        prefix = prefix + jnp.where(row >= shift, pltpu.roll(prefix, shift, 0), 0.0)
        shift *= 2
    suffix = prefix[length - 1:length, :] - prefix + lf
    is_f_fwd = (col >= M_HEADS) & (col < 2 * M_HEADS)
    is_f_bwd = (col >= 3 * M_HEADS) & (col < 4 * M_HEADS)
    return jnp.where(is_f_fwd, prefix, jnp.where(is_f_bwd, suffix, g))


def _gate_proj_kernel(a_ref, w_ref, b_ref, o_ref):
    g = lax.dot_general(a_ref[...], w_ref[...], _NT, preferred_element_type=F32) + b_ref[...]
    for c in range(g.shape[0] // M_CHUNK):
        rows = slice(c * M_CHUNK, (c + 1) * M_CHUNK)
        o_ref[rows, :] = _gate_table(g[rows, :])


def _gate_proj(a, w_gate_t, bias, tm=1024, name="gate_proj"):
    m, k = a.shape
    n = w_gate_t.shape[0]
    return pl.pallas_call(
        _gate_proj_kernel,
        grid=(m // tm,),
        in_specs=[pl.BlockSpec((tm, k), lambda i: (i, 0)),
                  pl.BlockSpec((n, k), lambda i: (0, 0)),
                  pl.BlockSpec((1, n), lambda i: (0, 0))],
        out_specs=pl.BlockSpec((tm, n), lambda i: (i, 0)),
        out_shape=jax.ShapeDtypeStruct((m, n), F32),
        compiler_params=_params(1),
        name=name,
    )(a, w_gate_t, bias)


def _matmul_ksplit_kernel(a_ref, w_ref, o_ref, acc_ref):
    kk = pl.program_id(2)

    @pl.when(kk == 0)
    def _():
        acc_ref[...] = jnp.zeros_like(acc_ref)

    acc_ref[...] += jnp.dot(a_ref[...], w_ref[...], preferred_element_type=F32)

    @pl.when(kk == pl.num_programs(2) - 1)
    def _():
        o_ref[...] = acc_ref[...].astype(o_ref.dtype)


def _matmul_ksplit(a, w, *, tm, tn, tk, name):
    m, k = a.shape
    n = w.shape[1]
    return pl.pallas_call(
        _matmul_ksplit_kernel,
        grid=(n // tn, m // tm, k // tk),
        in_specs=[pl.BlockSpec((tm, tk), lambda j, i, kk: (i, kk)),
                  pl.BlockSpec((tk, tn), lambda j, i, kk: (kk, j))],
        out_specs=pl.BlockSpec((tm, tn), lambda j, i, kk: (i, j)),
        out_shape=jax.ShapeDtypeStruct((m, n), BF16),
        scratch_shapes=[pltpu.VMEM((tm, tn), F32)],
        compiler_params=_params(3),
        name=name,
    )(a, w)


def _tile_lanes(x, width):
    return jnp.concatenate([x] * (width // x.shape[1]), axis=1)


def _rows_to_replicated_columns(x_row, y_row):
    length = x_row.shape[1]

    def pieces(v):
        hi = v.astype(BF16)
        rest = v - hi.astype(F32)
        mid = rest.astype(BF16)
        return [hi, mid, (rest - mid.astype(F32)).astype(BF16)]

    n_rows = 16
    stacked = jnp.concatenate(
        pieces(x_row) + pieces(y_row) + [jnp.zeros((n_rows - 6, length), BF16)], axis=0)
    piece = lax.broadcasted_iota(jnp.int32, (n_rows, 2 * LANES), 0)
    lane = lax.broadcasted_iota(jnp.int32, (n_rows, 2 * LANES), 1)
    selector = jnp.where((lane < LANES) == (piece < 3), 1.0, 0.0)
    selector = jnp.where(piece < 6, selector, 0.0).astype(BF16)
    both = lax.dot_general(stacked, selector, (((0,), (0,)), ((), ())), preferred_element_type=F32)
    return both[:, :LANES], both[:, LANES:]


def _mlstm_chunk(q_ref, k_ref, v_ref, gr_ref, ct_ref, m_ref, h_ref, *, reverse):
    length = k_ref.shape[0]
    dv = v_ref.shape[1]
    k = k_ref[...]
    v_aug = jnp.concatenate([v_ref[...], jnp.ones((length, LANES), BF16)], axis=1)
    if h_ref is not None:
        q = q_ref[...]
        qk = lax.dot_general(q, k, _NT, preferred_element_type=F32)
        ct = ct_ref[...]
        n_hi = ct[:, dv:].astype(BF16)
        n_mid = (ct[:, dv:] - n_hi.astype(F32)).astype(BF16)
        qc_aug = jnp.dot(q, jnp.concatenate([ct[:, :dv].astype(BF16), n_hi, n_mid], axis=1),
                         preferred_element_type=F32)
        qn = qc_aug[:, dv:dv + LANES] + qc_aug[:, dv + LANES:]
    yield

    ji, jf = (2, 3) if reverse else (0, 1)
    gr = gr_ref[...]
    ig_r = gr[ji:ji + 1, :]
    b_r = gr[jf:jf + 1, :]
    b_c, ig_c = _rows_to_replicated_columns(b_r, ig_r)
    b_tot = b_r[:, 0:1] if reverse else b_r[:, length - 1:length]
    m_prev = m_ref[...]
    w_c = b_tot - b_c + ig_c
    m_new = jnp.maximum(b_tot + m_prev, jnp.max(w_c, axis=0, keepdims=True)[:, 0:1])
    decay = jnp.exp(b_tot + m_prev - m_new)
    kw = k.astype(F32) * _tile_lanes(jnp.exp(w_c - m_new), k.shape[1])
    if h_ref is not None:
        t_idx = lax.broadcasted_iota(jnp.int32, (length, length), 0)
        s_idx = lax.broadcasted_iota(jnp.int32, (length, length), 1)
        seen = (s_idx >= t_idx) if reverse else (s_idx <= t_idx)
        a_c = b_c + m_prev
        dm = jnp.where(seen, _tile_lanes(b_c, length) - b_r + ig_r, -jnp.inf)
        m_t = jnp.maximum(a_c, jnp.broadcast_to(jnp.max(dm, axis=1, keepdims=True), (length, LANES)))
        s = qk * Q_SCALE * jnp.exp(dm - _tile_lanes(m_t, length))
        inter = jnp.exp(a_c - m_t) * Q_SCALE
        s_sum = jnp.broadcast_to(jnp.sum(s, axis=1, keepdims=True), (length, LANES))
        den = inter * qn + s_sum
        scale = 1.0 / jnp.maximum(jnp.abs(den), jnp.exp(-m_t))
    yield

    if h_ref is not None:
        sv = jnp.dot(s.astype(BF16), v_aug[:, :dv], preferred_element_type=F32)
        num = _tile_lanes(inter, dv) * qc_aug[:, :dv] + sv
        h_ref[...] = (num * _tile_lanes(scale, dv)).astype(h_ref.dtype)
    ct_ref[...] = decay * ct_ref[...] + lax.dot_general(
        kw.astype(BF16), v_aug, (((0,), (0,)), ((), ())), preferred_element_type=F32)
    m_ref[...] = m_new
    yield


def _interleave(*chunks):
    for _ in range(3):
        for chunk in chunks:
            next(chunk)


def _mlstm_kernel(qf, kf, vf, grf, qb, kb, vb, grb, kc, vc, grc, hf, hb, *state):
    c = pl.program_id(2)
    fwd_state = [state[2 * i:2 * i + 2] for i in range(M_HEADS_PER_STEP)]
    bwd_state = [state[2 * (M_HEADS_PER_STEP + i):2 * (M_HEADS_PER_STEP + i) + 2] for i in range(M_HEADS_PER_STEP)]

    def head(ref, i, width):
        return ref.at[:, i * width:(i + 1) * width]

    @pl.when(c == 0)
    def _():
        for ref in state:
            ref[...] = jnp.zeros_like(ref)
        chunks = []
        for i in range(M_HEADS_PER_STEP):
            k, v, g = head(kc, i, M_DQK), head(vc, i, M_DV), grc.at[i]
            chunks.append(_mlstm_chunk(None, k, v, g, *fwd_state[i], None, reverse=False))
            chunks.append(_mlstm_chunk(None, k, v, g, *bwd_state[i], None, reverse=True))
        _interleave(*chunks)

    @pl.when(c > 0)
    def _():
        chunks = []
        for i in range(M_HEADS_PER_STEP):
            chunks.append(_mlstm_chunk(head(qf, i, M_DQK), head(kf, i, M_DQK), head(vf, i, M_DV), grf.at[i],
                                       *fwd_state[i], head(hf, i, M_DV), reverse=False))
            chunks.append(_mlstm_chunk(head(qb, i, M_DQK), head(kb, i, M_DQK), head(vb, i, M_DV), grb.at[i],
                                       *bwd_state[i], head(hb, i, M_DV), reverse=True))
        _interleave(*chunks)


def _mlstm(proj, proj_ctx, gr, gr_ctx, batch):
    length = M_CHUNK
    assert CTX_LEN == length
    seq = proj.shape[0] // batch
    nc = seq // length
    hps = M_HEADS_PER_STEP
    qk_w, v_w = hps * M_DQK, hps * M_DV

    def fwd(c):
        return jnp.maximum(c - 1, 0)

    def bwd(c):
        return nc - jnp.maximum(c, 1)

    def lat_specs(chunk_of):
        return [
            pl.BlockSpec((length, qk_w), lambda b, h, c: (b * nc + chunk_of(c), OFF_QM // qk_w + h)),
            pl.BlockSpec((length, qk_w), lambda b, h, c: (b * nc + chunk_of(c), OFF_KM // qk_w + h)),
            pl.BlockSpec((length, v_w), lambda b, h, c: (b * nc + chunk_of(c), OFF_VM // v_w + h)),
            pl.BlockSpec((None, hps, 4, length), lambda b, h, c: (b, h, 0, chunk_of(c))),
        ]

    ctx_specs = [
        pl.BlockSpec((length, qk_w), lambda b, h, c: (b, CTX_OFF_KM // qk_w + h)),
        pl.BlockSpec((length, v_w), lambda b, h, c: (b, CTX_OFF_VM // v_w + h)),
        pl.BlockSpec((None, hps, 4, length), lambda b, h, c: (b, h, 0, 0)),
    ]
    out_shape = jax.ShapeDtypeStruct((batch * seq, M_V_W), BF16)
    state = [pltpu.VMEM((M_DQK, M_DV + LANES), F32), pltpu.VMEM((1, 1), F32)]
    return pl.pallas_call(
        _mlstm_kernel,
        grid=(batch, M_HEADS // hps, nc + 1),
        in_specs=lat_specs(fwd) + lat_specs(bwd) + ctx_specs,
        out_specs=[pl.BlockSpec((length, v_w), lambda b, h, c: (b * nc + fwd(c), h)),
                   pl.BlockSpec((length, v_w), lambda b, h, c: (b * nc + bwd(c), h))],
        out_shape=[out_shape, out_shape],
        scratch_shapes=state * (2 * hps),
        compiler_params=_params(3),
        name="mlstm",
    )(proj, proj, proj, gr, proj, proj, proj, gr, proj_ctx, proj_ctx, gr_ctx)


def _readout_kernel(hf_ref, hb_ref, om_ref, g_ref, o_ref):
    for h in range(M_HEADS):
        cols = slice(h * M_DV, (h + 1) * M_DV)
        x = hf_ref[:, cols].astype(F32) + hb_ref[:, cols].astype(F32)
        y = _rms(x) * g_ref[:, cols] * _sigmoid(om_ref[:, cols].astype(F32))
        o_ref[:, cols] = y.astype(o_ref.dtype)


def _readout(hf, hb, proj, m_norm_g, tm=256):
    m = hf.shape[0]
    row = pl.BlockSpec((tm, M_V_W), lambda i: (i, 0))
    return pl.pallas_call(
        _readout_kernel,
        grid=(m // tm,),
        in_specs=[row, row,
                  pl.BlockSpec((tm, M_V_W), lambda i: (i, OFF_OM // M_V_W)),
                  pl.BlockSpec((1, M_V_W), lambda i: (0, 0))],
        out_specs=row,
        out_shape=jax.ShapeDtypeStruct((m, M_V_W), BF16),
        compiler_params=_params(1),
        name="mlstm_readout",
    )(hf, hb, proj, m_norm_g.reshape(1, M_V_W))


def _rope(x, cos, sin_signed):
    lane = lax.broadcasted_iota(jnp.int32, x.shape, 1)
    partner = jnp.where((lane % 64) < 32, pltpu.roll(x, LANES - 32, 1), pltpu.roll(x, 32, 1))
    return x * cos + partner * sin_signed


def _attn_kernel(sink_ref, q_ref, k_ref, v_ref, kc_ref, vc_ref, cos_ref, sin_ref, o_ref,
                 kpad, vpad, s_buf, p_buf, den_buf):
    seq = k_ref.shape[0]
    nb = seq // A_BLOCK
    kvh = pl.program_id(1)
    scale = A_HEAD_DIM ** -0.5
    rows = A_GROUP * A_BLOCK
    n_win = 3 * A_BLOCK
    zeros = jnp.zeros((A_BLOCK, A_HEAD_DIM), BF16)
    kpad[0:A_BLOCK, :] = zeros
    kpad[A_BLOCK + seq:, :] = zeros
    vpad[0:A_BLOCK, :] = zeros
    vpad[A_BLOCK + seq:, :] = zeros
    kpad[A_BLOCK:A_BLOCK + seq, :] = _rope(k_ref[...].astype(F32), cos_ref[...], sin_ref[...]).astype(BF16)
    vpad[A_BLOCK:A_BLOCK + seq, :] = v_ref[...]
    qi = lax.broadcasted_iota(jnp.int32, (rows, n_win), 0) & (A_BLOCK - 1)
    kj = lax.broadcasted_iota(jnp.int32, (rows, n_win), 1)
    band = (kj >= qi) & (kj <= qi + 2 * WINDOW)
    kj_row = lax.broadcasted_iota(jnp.int32, (1, n_win), 1)
    sink = jnp.concatenate(
        [jnp.full((A_BLOCK, 1), sink_ref[kvh * A_GROUP + g], F32) for g in range(A_GROUP)], axis=0)
    nt = (((1,), (1,)), ((), ()))

    def scores(n):
        r0 = pl.multiple_of(n * A_BLOCK, A_BLOCK)
        cos = cos_ref[pl.ds(r0, A_BLOCK), :] * scale
        sin = sin_ref[pl.ds(r0, A_BLOCK), :] * scale
        q = jnp.concatenate(
            [_rope(q_ref[pl.ds(r0, A_BLOCK), g * A_HEAD_DIM:(g + 1) * A_HEAD_DIM].astype(F32), cos, sin)
             for g in range(A_GROUP)], axis=0).astype(BF16)
        kpos = r0 - A_BLOCK + kj_row
        valid = band & ((kpos >= 0) & (kpos < seq))
        sw = lax.dot_general(q, kpad[pl.ds(r0, n_win), :], nt, preferred_element_type=F32)
        s_buf[:, 0:n_win] = jnp.where(valid, sw, -jnp.inf)
        s_buf[:, n_win:] = lax.dot_general(q, kc_ref[...], nt, preferred_element_type=F32)

    def softmax():
        s = s_buf[...]
        mx = jnp.maximum(jnp.max(s, axis=1, keepdims=True), sink)
        p = jnp.exp(s - mx)
        den_buf[...] = jnp.sum(p, axis=1, keepdims=True) + jnp.exp(sink - mx)
        p_buf[...] = p.astype(p_buf.dtype)

    def output(n):
        r0 = pl.multiple_of(n * A_BLOCK, A_BLOCK)
        o = (jnp.dot(p_buf[:, 0:n_win], vpad[pl.ds(r0, n_win), :], preferred_element_type=F32)
             + jnp.dot(p_buf[:, n_win:], vc_ref[...], preferred_element_type=F32)) * (1.0 / den_buf[...])
        for g in range(A_GROUP):
            o_ref[pl.ds(r0, A_BLOCK), g * A_HEAD_DIM:(g + 1) * A_HEAD_DIM] = (
                o[g * A_BLOCK:(g + 1) * A_BLOCK, :].astype(o_ref.dtype))

    scores(0)
    softmax()
    scores(1)

    def body(t, carry):
        output(t - 1)
        softmax()
        scores(t + 1)
        return carry

    lax.fori_loop(1, nb - 1, body, 0)
    output(nb - 2)
    softmax()
    output(nb - 1)


def _attention(proj, proj_ctx, sink, cos, sin, batch):
    seq = proj.shape[0] // batch
    qw = A_GROUP * A_HEAD_DIM
    tab = pl.BlockSpec((seq, A_HEAD_DIM), lambda b, h: (0, 0))
    return pl.pallas_call(
        _attn_kernel,
        grid=(batch, A_KV_HEADS),
        in_specs=[pl.BlockSpec(memory_space=pltpu.SMEM),
                  pl.BlockSpec((seq, qw), lambda b, h: (b, OFF_QA // qw + h)),
                  pl.BlockSpec((seq, A_HEAD_DIM), lambda b, h: (b, OFF_KA // A_HEAD_DIM + h)),
                  pl.BlockSpec((seq, A_HEAD_DIM), lambda b, h: (b, OFF_VA // A_HEAD_DIM + h)),
                  pl.BlockSpec((CTX_LEN, A_HEAD_DIM), lambda b, h: (b, CTX_OFF_KA // A_HEAD_DIM + h)),
                  pl.BlockSpec((CTX_LEN, A_HEAD_DIM), lambda b, h: (b, CTX_OFF_VA // A_HEAD_DIM + h)),
                  tab, tab],
        out_specs=pl.BlockSpec((seq, qw), lambda b, h: (b, h)),
        out_shape=jax.ShapeDtypeStruct((batch * seq, A_Q_W), BF16),
        scratch_shapes=[pltpu.VMEM((seq + 2 * A_BLOCK, A_HEAD_DIM), BF16),
                        pltpu.VMEM((seq + 2 * A_BLOCK, A_HEAD_DIM), BF16),
                        pltpu.VMEM((A_GROUP * A_BLOCK, 3 * A_BLOCK + CTX_LEN), F32),
                        pltpu.VMEM((A_GROUP * A_BLOCK, 3 * A_BLOCK + CTX_LEN), BF16),
                        pltpu.VMEM((A_GROUP * A_BLOCK, 1), F32)],
        compiler_params=_params(2),
        name="window_attention",
    )(sink, proj, proj, proj, proj_ctx, proj_ctx, cos, sin)


def _rope_tables(seq):
    t = jnp.arange(seq)
    nf = A_HEAD_DIM // 4
    freqs = ROPE_BASE ** (-jnp.arange(nf, dtype=F32) / nf)
    ang_r = (t // GRID_W).astype(F32)[:, None] * freqs
    ang_c = (t % GRID_W).astype(F32)[:, None] * freqs
    cos = jnp.concatenate([jnp.cos(ang_r), jnp.cos(ang_r), jnp.cos(ang_c), jnp.cos(ang_c)], axis=-1)
    sin = jnp.concatenate([-jnp.sin(ang_r), jnp.sin(ang_r), -jnp.sin(ang_c), jnp.sin(ang_c)], axis=-1)
    return cos, sin


def _gate_rows(g, batch):
    return g[:, :N_GATES].reshape(batch, -1, 4, M_HEADS).transpose(0, 3, 2, 1)


def kernel(x, c, ctx, c_ctx, w_mod, b_mod, norm_g, w_in, m_gate_b, m_norm_g, attn_sink,
           w_out_m, w_out_a, w_o, w_ff1, w_ff2):
    batch, seq, d = x.shape
    n_ctx = ctx.shape[1]
    x2 = x.reshape(batch * seq, d)
    ctx2 = ctx.reshape(batch * n_ctx, d)

    w_in_t = w_in[0].T
    w_gate_t = jnp.pad(w_in_t[W_IN_GATE_LO:W_IN_GATE_HI], ((0, GATE_PAD - N_GATES), (0, 0))).astype(BF16)
    gate_bias = jnp.pad(m_gate_b[0].reshape(1, N_GATES), ((0, 0), (0, GATE_PAD - N_GATES)))

    c_all = jnp.zeros((MOD_ROWS, d), F32).at[:batch].set(c).at[batch].set(c_ctx)
    mod = _modulation(c_all, w_mod[0], b_mod[0]).reshape(MOD_ROWS, 6, 1, d)
    gn = norm_g[0].reshape(4, 1, d)

    u = _prenorm(x2, gn, mod, seq)
    uc = _prenorm(ctx2, gn, mod, n_ctx, fixed_row=batch)

    proj = _stream_matmul(u, w_in_t, n_out=PROJ_ALIGNED_W, out_dtype=BF16, transposed=True, name="in_proj")
    g_lat = _gate_proj(u, w_gate_t, gate_bias)
    ctx_tn = 512
    lat_tiles = (OFF_OM - OFF_KM) // ctx_tn
    proj_ctx = _stream_matmul(
        uc, w_in_t, n_out=CTX_PROJ_W, out_dtype=BF16, transposed=True, tn=ctx_tn, name="in_proj_ctx",
        tile_of=lambda j: jnp.where(j < lat_tiles, j + OFF_KM // ctx_tn, j - lat_tiles + OFF_KA // ctx_tn))
    g_ctx = _gate_proj(uc, w_gate_t, gate_bias, name="gate_proj_ctx")

    hf, hb = _mlstm(proj, proj_ctx, _gate_rows(g_lat, batch), _gate_rows(g_ctx, batch), batch)
    hm = _readout(hf, hb, proj, m_norm_g[0])

    cos, sin = _rope_tables(seq)
    at = _attention(proj, proj_ctx, attn_sink[0], cos, sin, batch)

    zm = _stream_matmul(hm, w_out_m, n_out=d, out_dtype=BF16, epilogue=_gate_epilogue,
                        extra=[proj], extra_off=[OFF_BGM], name="out_m")
    z = _stream_matmul(at, w_out_a, n_out=d, out_dtype=BF16, epilogue=_gate_add_epilogue,
                       extra=[proj, zm], extra_off=[OFF_BGA, 0], name="out_a_merge")
    mix = _stream_matmul(z, w_o, n_out=d, out_dtype=BF16, name="out_proj")
    h1, u2 = _post_mix(x2, mix, gn, mod, seq)

    hid, w_ff2_bf16 = _stream_matmul(u2, w_ff1, n_out=D_FF, out_dtype=BF16, epilogue=_relu2_epilogue,
                                     side=w_ff2, name="ff1")
    y = _matmul_ksplit(hid, w_ff2_bf16, tm=1024, tn=1024, tk=4096, name="ff2")
    out = _post_mlp(h1, y, gn, mod, seq)
    return out.reshape(batch, seq, d)
```

```python
import functools

import jax
import jax.numpy as jnp
from jax import lax
from jax.experimental import pallas as pl
from jax.experimental.pallas import tpu as pltpu

F32 = jnp.float32
BF16 = jnp.bfloat16

D_MODEL = 4096
GRID_W = 64
CTX_LEN = 256
M_HEADS = 8
M_DQK = 256
M_DV = 512
A_HEADS = 32
A_KV_HEADS = 8
A_HEAD_DIM = 128
A_GROUP = A_HEADS // A_KV_HEADS
WINDOW = 128
A_BLOCK = 128
ROPE_BASE = 10000.0
D_FF = 4 * D_MODEL
EPS = 1e-6

M_QK_W = M_HEADS * M_DQK
M_V_W = M_HEADS * M_DV
A_Q_W = A_HEADS * A_HEAD_DIM
A_KV_W = A_KV_HEADS * A_HEAD_DIM
N_GATES = 4 * M_HEADS

W_IN_GATE_LO = 2 * M_QK_W + 2 * M_V_W
W_IN_GATE_HI = W_IN_GATE_LO + N_GATES
OFF_QM = 0
OFF_KM = OFF_QM + M_QK_W
OFF_VM = OFF_KM + M_QK_W
OFF_OM = OFF_VM + M_V_W
OFF_QA = OFF_OM + M_V_W
OFF_KA = OFF_QA + A_Q_W
OFF_VA = OFF_KA + A_KV_W
OFF_BGM = OFF_VA + A_KV_W
OFF_BGA = OFF_BGM + D_MODEL
PROJ_ALIGNED_W = OFF_BGA + D_MODEL
CTX_OFF_KM = 0
CTX_OFF_VM = M_QK_W
CTX_OFF_KA = M_QK_W + M_V_W
CTX_OFF_VA = CTX_OFF_KA + A_KV_W
CTX_PROJ_W = CTX_OFF_VA + A_KV_W

M_CHUNK = 256
M_HEADS_PER_STEP = 4
Q_SCALE = M_DQK ** -0.5
LANES = 128
GATE_PAD = LANES
MOD_ROWS = 8
VMEM_LIMIT = 60000 * 1024


def _params(n_axes):
    return pltpu.CompilerParams(dimension_semantics=("arbitrary",) * n_axes, vmem_limit_bytes=VMEM_LIMIT)


def _sigmoid(x):
    return 1.0 / (1.0 + jnp.exp(-x))


def _log_sigmoid(x):
    return jnp.minimum(x, 0.0) - jnp.log1p(jnp.exp(-jnp.abs(x)))


def _rms(x):
    return x * lax.rsqrt(jnp.mean(x * x, axis=-1, keepdims=True) + EPS)


def _mod_kernel(c_ref, w_ref, b_ref, o_ref):
    c = c_ref[...]
    sc = (c * _sigmoid(c)).astype(BF16)
    o_ref[...] = jnp.dot(sc, w_ref[...].astype(BF16), preferred_element_type=F32) + b_ref[...]


def _modulation(c_all, w_mod, b_mod, tn=1024):
    d, n = w_mod.shape
    return pl.pallas_call(
        _mod_kernel,
        grid=(n // tn,),
        in_specs=[pl.BlockSpec((MOD_ROWS, d), lambda j: (0, 0)),
                  pl.BlockSpec((d, tn), lambda j: (0, j)),
                  pl.BlockSpec((1, tn), lambda j: (0, j))],
        out_specs=pl.BlockSpec((MOD_ROWS, tn), lambda j: (0, j)),
        out_shape=jax.ShapeDtypeStruct((MOD_ROWS, n), F32),
        compiler_params=_params(1),
        name="modulation",
    )(c_all, w_mod, b_mod.reshape(1, n))


def _prenorm_kernel(x_ref, g_ref, sh_ref, sc_ref, o_ref):
    y = _rms(x_ref[...]) * g_ref[...]
    o_ref[...] = (y * (1.0 + sc_ref[...]) + sh_ref[...]).astype(o_ref.dtype)


def _mod_spec(slot, rows_per_mod_row, tm, fixed_row=None):
    if fixed_row is None:
        return pl.BlockSpec((None, None, 1, D_MODEL), lambda i: ((i * tm) // rows_per_mod_row, slot, 0, 0))
    return pl.BlockSpec((None, None, 1, D_MODEL), lambda i: (fixed_row, slot, 0, 0))


def _norm_spec(slot):
    return pl.BlockSpec((None, 1, D_MODEL), lambda i: (slot, 0, 0))


def _prenorm(x2, norm_g, mod, rows_per_mod_row, fixed_row=None, tm=512):
    m = x2.shape[0]
    return pl.pallas_call(
        _prenorm_kernel,
        grid=(m // tm,),
        in_specs=[pl.BlockSpec((tm, D_MODEL), lambda i: (i, 0)),
                  _norm_spec(0),
                  _mod_spec(0, rows_per_mod_row, tm, fixed_row),
                  _mod_spec(1, rows_per_mod_row, tm, fixed_row)],
        out_specs=pl.BlockSpec((tm, D_MODEL), lambda i: (i, 0)),
        out_shape=jax.ShapeDtypeStruct((m, D_MODEL), BF16),
        compiler_params=_params(1),
        name="prenorm",
    )(x2, norm_g, mod, mod)


def _post_mix_kernel(x_ref, mix_ref, gn1_ref, gn2_ref, g1_ref, sh2_ref, s2_ref, h1_ref, u2_ref):
    h1 = x_ref[...] + g1_ref[...] * (_rms(mix_ref[...].astype(F32)) * gn1_ref[...])
    h1_ref[...] = h1
    y2 = _rms(h1) * gn2_ref[...]
    u2_ref[...] = (y2 * (1.0 + s2_ref[...]) + sh2_ref[...]).astype(u2_ref.dtype)


def _post_mix(x2, mix, norm_g, mod, rows_per_mod_row, tm=512):
    m = x2.shape[0]
    row = pl.BlockSpec((tm, D_MODEL), lambda i: (i, 0))
    return pl.pallas_call(
        _post_mix_kernel,
        grid=(m // tm,),
        in_specs=[row, row, _norm_spec(1), _norm_spec(2),
                  _mod_spec(2, rows_per_mod_row, tm), _mod_spec(3, rows_per_mod_row, tm),
                  _mod_spec(4, rows_per_mod_row, tm)],
        out_specs=[row, row],
        out_shape=[jax.ShapeDtypeStruct((m, D_MODEL), F32), jax.ShapeDtypeStruct((m, D_MODEL), BF16)],
        compiler_params=_params(1),
        name="post_mix",
    )(x2, mix, norm_g, norm_g, mod, mod, mod)


def _post_mlp_kernel(h1_ref, y_ref, gn_ref, g2_ref, o_ref):
    o_ref[...] = h1_ref[...] + g2_ref[...] * (_rms(y_ref[...].astype(F32)) * gn_ref[...])


def _post_mlp(h1, y, norm_g, mod, rows_per_mod_row, tm=512):
    m = h1.shape[0]
    row = pl.BlockSpec((tm, D_MODEL), lambda i: (i, 0))
    return pl.pallas_call(
        _post_mlp_kernel,
        grid=(m // tm,),
        in_specs=[row, row, _norm_spec(3), _mod_spec(5, rows_per_mod_row, tm)],
        out_specs=row,
        out_shape=jax.ShapeDtypeStruct((m, D_MODEL), F32),
        compiler_params=_params(1),
        name="post_mlp",
    )(h1, y, norm_g, mod)


_NT = (((1,), (1,)), ((), ()))


def _stream_kernel(a_ref, slab_ref, *rest, nj, transposed, tile_of, epilogue, n_extra, has_next, has_side):
    j = pl.program_id(0)
    i = pl.program_id(1)
    rest = list(rest)
    next_ref = rest.pop(0) if has_next else None
    extra = [rest.pop(0) for _ in range(n_extra)]
    side_ref = rest.pop(0) if has_side else None
    o_ref = rest.pop(0)
    side_out_ref = rest.pop(0) if has_side else None
    w_scr = rest.pop(0)
    rows = slab_ref.shape[0]
    r0 = pl.multiple_of(i * rows, rows)

    def cast_slab():
        dst = w_scr.at[j % 2]
        if has_next:
            tile_rows = w_scr.shape[1]
            shifted = tile_of(jnp.minimum(j, nj - 1)) * tile_rows >= W_IN_GATE_LO
            body = rows - N_GATES
            start = pl.multiple_of(jnp.where(shifted, N_GATES, 0), N_GATES)
            dst[pl.ds(r0, body), :] = slab_ref[pl.ds(start, body), :].astype(BF16)
            tail = jnp.where(shifted, next_ref[...], slab_ref[body:, :])
            dst[pl.ds(r0 + body, N_GATES), :] = tail.astype(BF16)
        else:
            dst[pl.ds(r0, rows), :] = slab_ref[...].astype(BF16)

    @pl.when(j == 0)
    def _():
        cast_slab()

    @pl.when(j > 0)
    def _():
        cast_slab()
        w = w_scr[(j - 1) % 2]
        if transposed:
            acc = lax.dot_general(a_ref[...], w, _NT, preferred_element_type=F32)
        else:
            acc = jnp.dot(a_ref[...], w, preferred_element_type=F32)
        o_ref[...] = epilogue(acc, *[e[...] for e in extra]).astype(o_ref.dtype)
        if has_side:
            side_out_ref[...] = side_ref[...].astype(side_out_ref.dtype)


def _stream_matmul(a, w, *, n_out, out_dtype, epilogue=lambda acc: acc, extra=(), extra_off=(), transposed=False,
                   tile_of=None, side=None, tm=1024, tn=1024, name):
    m, k = a.shape
    nj, ni = n_out // tn, m // tm
    tile_of = (lambda j: j) if tile_of is None else tile_of

    def prev(j):
        return jnp.maximum(j - 1, 0)

    def cur(j):
        return jnp.minimum(j, nj - 1)

    def row(j, i):
        return jnp.where(j == 0, 0, i)

    in_specs = [pl.BlockSpec((tm, k), lambda j, i: (row(j, i), 0))]
    args = [a]
    if transposed:
        slab_rows = tn // ni
        in_specs.append(pl.BlockSpec((slab_rows, k), lambda j, i: (tile_of(cur(j)) * ni + i, 0)))
        in_specs.append(pl.BlockSpec(
            (N_GATES, k), lambda j, i: ((tile_of(cur(j)) * tn + (i + 1) * slab_rows) // N_GATES, 0)))
        args += [w, w]
        scratch = pltpu.VMEM((2, tn, k), BF16)
    else:
        in_specs.append(pl.BlockSpec((None, k // ni, tn), lambda j, i: (0, i, cur(j))))
        args.append(w)
        scratch = pltpu.VMEM((2, k, tn), BF16)
    for arr, off in zip(extra, extra_off):
        in_specs.append(pl.BlockSpec((tm, tn), lambda j, i, off=off: (row(j, i), off // tn + prev(j))))
        args.append(arr)
    out_specs = [pl.BlockSpec((tm, tn), lambda j, i: (row(j, i), prev(j)))]
    out_shape = [jax.ShapeDtypeStruct((m, n_out), out_dtype)]
    if side is not None:
        s_rows, s_cols = side.shape[1:]
        slab = s_rows // (nj * ni)

        def step(j, i):
            return jnp.maximum((j - 1) * ni + i, 0)

        in_specs.append(pl.BlockSpec((None, slab, s_cols), lambda j, i: (0, step(j, i), 0)))
        args.append(side)
        out_specs.append(pl.BlockSpec((slab, s_cols), lambda j, i: (step(j, i), 0)))
        out_shape.append(jax.ShapeDtypeStruct((s_rows, s_cols), BF16))
    out = pl.pallas_call(
        functools.partial(_stream_kernel, nj=nj, transposed=transposed, tile_of=tile_of, epilogue=epilogue,
                          n_extra=len(extra), has_next=transposed, has_side=side is not None),
        grid=(nj + 1, ni),
        in_specs=in_specs,
        out_specs=out_specs,
        out_shape=out_shape,
        scratch_shapes=[scratch],
        compiler_params=_params(2),
        name=name,
    )(*args)
    return out if side is not None else out[0]


def _gate_epilogue(acc, g):
    return _sigmoid(g.astype(F32)) * acc


def _gate_add_epilogue(acc, g, prev):
    return prev.astype(F32) + _sigmoid(g.astype(F32)) * acc


def _relu2_epilogue(acc):
    return jnp.square(jnp.maximum(acc, 0.0))


def _gate_table(g):
    length = g.shape[0]
    lf = _log_sigmoid(g)
    row = lax.broadcasted_iota(jnp.int32, g.shape, 0)
    col = lax.broadcasted_iota(jnp.int32, g.shape, 1)
    prefix = lf
    shift = 1
    while shift < length:
        prefix = prefix + jnp.where(row >= shift, pltpu.roll(prefix, shift, 0), 0.0)
        shift *= 2
    suffix = prefix[length - 1:length, :] - prefix + lf
    is_f_fwd = (col >= M_HEADS) & (col < 2 * M_HEADS)
    is_f_bwd = (col >= 3 * M_HEADS) & (col < 4 * M_HEADS)
    return jnp.where(is_f_fwd, prefix, jnp.where(is_f_bwd, suffix, g))


def _gate_proj_kernel(a_ref, w_ref, b_ref, o_ref):
    g = lax.dot_general(a_ref[...], w_ref[...], _NT, preferred_element_type=F32) + b_ref[...]
    for c in range(g.shape[0] // M_CHUNK):
        rows = slice(c * M_CHUNK, (c + 1) * M_CHUNK)
        o_ref[rows, :] = _gate_table(g[rows, :])


def _gate_proj(a, w_gate_t, bias, tm=1024, name="gate_proj"):
    m, k = a.shape
    n = w_gate_t.shape[0]
    return pl.pallas_call(
        _gate_proj_kernel,
        grid=(m // tm,),
        in_specs=[pl.BlockSpec((tm, k), lambda i: (i, 0)),
                  pl.BlockSpec((n, k), lambda i: (0, 0)),
                  pl.BlockSpec((1, n), lambda i: (0, 0))],
        out_specs=pl.BlockSpec((tm, n), lambda i: (i, 0)),
        out_shape=jax.ShapeDtypeStruct((m, n), F32),
        compiler_params=_params(1),
        name=name,
    )(a, w_gate_t, bias)


def _matmul_ksplit_kernel(a_ref, w_ref, o_ref, acc_ref):
    kk = pl.program_id(2)

    @pl.when(kk == 0)
    def _():
        acc_ref[...] = jnp.zeros_like(acc_ref)

    acc_ref[...] += jnp.dot(a_ref[...], w_ref[...], preferred_element_type=F32)

    @pl.when(kk == pl.num_programs(2) - 1)
    def _():
        o_ref[...] = acc_ref[...].astype(o_ref.dtype)


def _matmul_ksplit(a, w, *, tm, tn, tk, name):
    m, k = a.shape
    n = w.shape[1]
    return pl.pallas_call(
        _matmul_ksplit_kernel,
        grid=(n // tn, m // tm, k // tk),
        in_specs=[pl.BlockSpec((tm, tk), lambda j, i, kk: (i, kk)),
                  pl.BlockSpec((tk, tn), lambda j, i, kk: (kk, j))],
        out_specs=pl.BlockSpec((tm, tn), lambda j, i, kk: (i, j)),
        out_shape=jax.ShapeDtypeStruct((m, n), BF16),
        scratch_shapes=[pltpu.VMEM((tm, tn), F32)],
        compiler_params=_params(3),
        name=name,
    )(a, w)


def _tile_lanes(x, width):
    return jnp.concatenate([x] * (width // x.shape[1]), axis=1)


def _rows_to_replicated_columns(x_row, y_row):
    length = x_row.shape[1]

    def pieces(v):
        hi = v.astype(BF16)
        rest = v - hi.astype(F32)
        mid = rest.astype(BF16)
        return [hi, mid, (rest - mid.astype(F32)).astype(BF16)]

    n_rows = 16
    stacked = jnp.concatenate(
        pieces(x_row) + pieces(y_row) + [jnp.zeros((n_rows - 6, length), BF16)], axis=0)
    piece = lax.broadcasted_iota(jnp.int32, (n_rows, 2 * LANES), 0)
    lane = lax.broadcasted_iota(jnp.int32, (n_rows, 2 * LANES), 1)
    selector = jnp.where((lane < LANES) == (piece < 3), 1.0, 0.0)
    selector = jnp.where(piece < 6, selector, 0.0).astype(BF16)
    both = lax.dot_general(stacked, selector, (((0,), (0,)), ((), ())), preferred_element_type=F32)
    return both[:, :LANES], both[:, LANES:]


def _mlstm_chunk(q_ref, k_ref, v_ref, gr_ref, ct_ref, m_ref, h_ref, *, reverse, readout=None):
    length = k_ref.shape[0]
    dv = v_ref.shape[1]
    k = k_ref[...]
    v_aug = jnp.concatenate([v_ref[...], jnp.ones((length, LANES), BF16)], axis=1)
    if h_ref is not None:
        q = q_ref[...]
        qk = lax.dot_general(q, k, _NT, preferred_element_type=F32)
        ct = ct_ref[...]
        n_hi = ct[:, dv:].astype(BF16)
        n_mid = (ct[:, dv:] - n_hi.astype(F32)).astype(BF16)
        qc_aug = jnp.dot(q, jnp.concatenate([ct[:, :dv].astype(BF16), n_hi, n_mid], axis=1),
                         preferred_element_type=F32)
        qn = qc_aug[:, dv:dv + LANES] + qc_aug[:, dv + LANES:]
    yield

    ji, jf = (2, 3) if reverse else (0, 1)
    gr = gr_ref[...]
    ig_r = gr[ji:ji + 1, :]
    b_r = gr[jf:jf + 1, :]
    b_c, ig_c = _rows_to_replicated_columns(b_r, ig_r)
    b_tot = b_r[:, 0:1] if reverse else b_r[:, length - 1:length]
    m_prev = m_ref[...]
    w_c = b_tot - b_c + ig_c
    m_new = jnp.maximum(b_tot + m_prev, jnp.max(w_c, axis=0, keepdims=True)[:, 0:1])
    decay = jnp.exp(b_tot + m_prev - m_new)
    kw = k.astype(F32) * _tile_lanes(jnp.exp(w_c - m_new), k.shape[1])
    if h_ref is not None:
        t_idx = lax.broadcasted_iota(jnp.int32, (length, length), 0)
        s_idx = lax.broadcasted_iota(jnp.int32, (length, length), 1)
        seen = (s_idx >= t_idx) if reverse else (s_idx <= t_idx)
        a_c = b_c + m_prev
        dm = jnp.where(seen, _tile_lanes(b_c, length) - b_r + ig_r, -jnp.inf)
        m_t = jnp.maximum(a_c, jnp.broadcast_to(jnp.max(dm, axis=1, keepdims=True), (length, LANES)))
        s = qk * Q_SCALE * jnp.exp(dm - _tile_lanes(m_t, length))
        inter = jnp.exp(a_c - m_t) * Q_SCALE
        s_sum = jnp.broadcast_to(jnp.sum(s, axis=1, keepdims=True), (length, LANES))
        den = inter * qn + s_sum
        scale = 1.0 / jnp.maximum(jnp.abs(den), jnp.exp(-m_t))
    yield

    if h_ref is not None:
        sv = jnp.dot(s.astype(BF16), v_aug[:, :dv], preferred_element_type=F32)
        num = _tile_lanes(inter, dv) * qc_aug[:, :dv] + sv
        h = num * _tile_lanes(scale, dv)
        if readout is not None:
            other_ref, om_ref, gain_ref = readout
            h = _rms(h + other_ref[...].astype(F32)) * gain_ref[...] * _sigmoid(om_ref[...].astype(F32))
        h_ref[...] = h.astype(h_ref.dtype)
    ct_ref[...] = decay * ct_ref[...] + lax.dot_general(
        kw.astype(BF16), v_aug, (((0,), (0,)), ((), ())), preferred_element_type=F32)
    m_ref[...] = m_new
    yield


def _interleave(*chunks):
    for _ in range(3):
        for chunk in chunks:
            next(chunk)


def _mlstm_kernel(q_ref, k_ref, v_ref, gr_ref, kc_ref, vc_ref, grc_ref, *rest, reverse, fused):
    c = pl.program_id(2)
    rest = list(rest)
    other_ref, om_ref, gain_ref = (rest.pop(0), rest.pop(0), rest.pop(0)) if fused else (None, None, None)
    h_ref = rest.pop(0)
    states = [rest[2 * i:2 * i + 2] for i in range(M_HEADS_PER_STEP)]

    def head(ref, i, width):
        return ref.at[:, i * width:(i + 1) * width]

    @pl.when(c == 0)
    def _():
        for ref in rest:
            ref[...] = jnp.zeros_like(ref)
        _interleave(*[_mlstm_chunk(None, head(kc_ref, i, M_DQK), head(vc_ref, i, M_DV), grc_ref.at[i],
                                   *states[i], None, reverse=reverse) for i in range(M_HEADS_PER_STEP)])

    @pl.when(c > 0)
    def _():
        chunks = []
        for i in range(M_HEADS_PER_STEP):
            readout = (head(other_ref, i, M_DV), head(om_ref, i, M_DV), head(gain_ref, i, M_DV)) if fused else None
            chunks.append(_mlstm_chunk(head(q_ref, i, M_DQK), head(k_ref, i, M_DQK), head(v_ref, i, M_DV),
                                       gr_ref.at[i], *states[i], head(h_ref, i, M_DV),
                                       reverse=reverse, readout=readout))
        _interleave(*chunks)


def _mlstm(proj, proj_ctx, gr, gr_ctx, batch, *, reverse, readout=None):
    length = M_CHUNK
    assert CTX_LEN == length
    seq = proj.shape[0] // batch
    nc = seq // length
    hps = M_HEADS_PER_STEP
    qk_w, v_w = hps * M_DQK, hps * M_DV

    def chunk_of(c):
        return (nc - jnp.maximum(c, 1)) if reverse else jnp.maximum(c - 1, 0)

    def rows(b, c):
        return b * nc + chunk_of(c)

    in_specs = [
        pl.BlockSpec((length, qk_w), lambda b, h, c: (rows(b, c), OFF_QM // qk_w + h)),
        pl.BlockSpec((length, qk_w), lambda b, h, c: (rows(b, c), OFF_KM // qk_w + h)),
        pl.BlockSpec((length, v_w), lambda b, h, c: (rows(b, c), OFF_VM // v_w + h)),
        pl.BlockSpec((None, hps, 4, length), lambda b, h, c: (b, h, 0, chunk_of(c))),
        pl.BlockSpec((length, qk_w), lambda b, h, c: (b, CTX_OFF_KM // qk_w + h)),
        pl.BlockSpec((length, v_w), lambda b, h, c: (b, CTX_OFF_VM // v_w + h)),
        pl.BlockSpec((None, hps, 4, length), lambda b, h, c: (b, h, 0, 0)),
    ]
    args = [proj, proj, proj, gr, proj_ctx, proj_ctx, gr_ctx]
    if readout is not None:
        other, gain = readout
        in_specs += [pl.BlockSpec((length, v_w), lambda b, h, c: (rows(b, c), h)),
                     pl.BlockSpec((length, v_w), lambda b, h, c: (rows(b, c), OFF_OM // v_w + h)),
                     pl.BlockSpec((1, v_w), lambda b, h, c: (0, h))]
        args += [other, proj, gain.reshape(1, M_V_W)]
    state = [pltpu.VMEM((M_DQK, M_DV + LANES), F32), pltpu.VMEM((1, 1), F32)]
    return pl.pallas_call(
        functools.partial(_mlstm_kernel, reverse=reverse, fused=readout is not None),
        grid=(batch, M_HEADS // hps, nc + 1),
        in_specs=in_specs,
        out_specs=pl.BlockSpec((length, v_w), lambda b, h, c: (rows(b, c), h)),
        out_shape=jax.ShapeDtypeStruct((batch * seq, M_V_W), BF16),
        scratch_shapes=state * hps,
        compiler_params=_params(3),
        name="mlstm_bwd" if reverse else "mlstm_fwd",
    )(*args)


def _rope(x, cos, sin_signed):
    lane = lax.broadcasted_iota(jnp.int32, x.shape, 1)
    partner = jnp.where((lane % 64) < 32, pltpu.roll(x, LANES - 32, 1), pltpu.roll(x, 32, 1))
    return x * cos + partner * sin_signed


def _attn_kernel(sink_ref, q_ref, k_ref, v_ref, kc_ref, vc_ref, cos_ref, sin_ref, o_ref,
                 kpad, vpad, s_buf, p_buf, den_buf):
    seq = k_ref.shape[0]
    nb = seq // A_BLOCK
    kvh = pl.program_id(1)
    scale = A_HEAD_DIM ** -0.5
    rows = A_GROUP * A_BLOCK
    n_win = 3 * A_BLOCK
    zeros = jnp.zeros((A_BLOCK, A_HEAD_DIM), BF16)
    kpad[0:A_BLOCK, :] = zeros
    kpad[A_BLOCK + seq:, :] = zeros
    vpad[0:A_BLOCK, :] = zeros
    vpad[A_BLOCK + seq:, :] = zeros
    kpad[A_BLOCK:A_BLOCK + seq, :] = _rope(k_ref[...].astype(F32), cos_ref[...], sin_ref[...]).astype(BF16)
    vpad[A_BLOCK:A_BLOCK + seq, :] = v_ref[...]
    qi = lax.broadcasted_iota(jnp.int32, (rows, n_win), 0) & (A_BLOCK - 1)
    kj = lax.broadcasted_iota(jnp.int32, (rows, n_win), 1)
    band = (kj >= qi) & (kj <= qi + 2 * WINDOW)
    kj_row = lax.broadcasted_iota(jnp.int32, (1, n_win), 1)
    sink = jnp.concatenate(
        [jnp.full((A_BLOCK, 1), sink_ref[kvh * A_GROUP + g], F32) for g in range(A_GROUP)], axis=0)
    nt = (((1,), (1,)), ((), ()))

    def scores(n):
        r0 = pl.multiple_of(n * A_BLOCK, A_BLOCK)
        cos = cos_ref[pl.ds(r0, A_BLOCK), :] * scale
        sin = sin_ref[pl.ds(r0, A_BLOCK), :] * scale
        q = jnp.concatenate(
            [_rope(q_ref[pl.ds(r0, A_BLOCK), g * A_HEAD_DIM:(g + 1) * A_HEAD_DIM].astype(F32), cos, sin)
             for g in range(A_GROUP)], axis=0).astype(BF16)
        kpos = r0 - A_BLOCK + kj_row
        valid = band & ((kpos >= 0) & (kpos < seq))
        sw = lax.dot_general(q, kpad[pl.ds(r0, n_win), :], nt, preferred_element_type=F32)
        s_buf[:, 0:n_win] = jnp.where(valid, sw, -jnp.inf)
        s_buf[:, n_win:] = lax.dot_general(q, kc_ref[...], nt, preferred_element_type=F32)

    def softmax():
        s = s_buf[...]
        mx = jnp.maximum(jnp.max(s, axis=1, keepdims=True), sink)
        p = jnp.exp(s - mx)
        den_buf[...] = jnp.sum(p, axis=1, keepdims=True) + jnp.exp(sink - mx)
        p_buf[...] = p.astype(p_buf.dtype)

    def output(n):
        r0 = pl.multiple_of(n * A_BLOCK, A_BLOCK)
        o = (jnp.dot(p_buf[:, 0:n_win], vpad[pl.ds(r0, n_win), :], preferred_element_type=F32)
             + jnp.dot(p_buf[:, n_win:], vc_ref[...], preferred_element_type=F32)) * (1.0 / den_buf[...])
        for g in range(A_GROUP):
            o_ref[pl.ds(r0, A_BLOCK), g * A_HEAD_DIM:(g + 1) * A_HEAD_DIM] = (
                o[g * A_BLOCK:(g + 1) * A_BLOCK, :].astype(o_ref.dtype))

    scores(0)
    softmax()
    scores(1)

    def body(t, carry):
        output(t - 1)
        softmax()
        scores(t + 1)
        return carry

    lax.fori_loop(1, nb - 1, body, 0)
    output(nb - 2)
    softmax()
    output(nb - 1)


def _attention(proj, proj_ctx, sink, cos, sin, batch):
    seq = proj.shape[0] // batch
    qw = A_GROUP * A_HEAD_DIM
    tab = pl.BlockSpec((seq, A_HEAD_DIM), lambda b, h: (0, 0))
    return pl.pallas_call(
        _attn_kernel,
        grid=(batch, A_KV_HEADS),
        in_specs=[pl.BlockSpec(memory_space=pltpu.SMEM),
                  pl.BlockSpec((seq, qw), lambda b, h: (b, OFF_QA // qw + h)),
                  pl.BlockSpec((seq, A_HEAD_DIM), lambda b, h: (b, OFF_KA // A_HEAD_DIM + h)),
                  pl.BlockSpec((seq, A_HEAD_DIM), lambda b, h: (b, OFF_VA // A_HEAD_DIM + h)),
                  pl.BlockSpec((CTX_LEN, A_HEAD_DIM), lambda b, h: (b, CTX_OFF_KA // A_HEAD_DIM + h)),
                  pl.BlockSpec((CTX_LEN, A_HEAD_DIM), lambda b, h: (b, CTX_OFF_VA // A_HEAD_DIM + h)),
                  tab, tab],
        out_specs=pl.BlockSpec((seq, qw), lambda b, h: (b, h)),
        out_shape=jax.ShapeDtypeStruct((batch * seq, A_Q_W), BF16),
        scratch_shapes=[pltpu.VMEM((seq + 2 * A_BLOCK, A_HEAD_DIM), BF16),
                        pltpu.VMEM((seq + 2 * A_BLOCK, A_HEAD_DIM), BF16),
                        pltpu.VMEM((A_GROUP * A_BLOCK, 3 * A_BLOCK + CTX_LEN), F32),
                        pltpu.VMEM((A_GROUP * A_BLOCK, 3 * A_BLOCK + CTX_LEN), BF16),
                        pltpu.VMEM((A_GROUP * A_BLOCK, 1), F32)],
        compiler_params=_params(2),
        name="window_attention",
    )(sink, proj, proj, proj, proj_ctx, proj_ctx, cos, sin)


def _rope_tables(seq):
    t = jnp.arange(seq)
    nf = A_HEAD_DIM // 4
    freqs = ROPE_BASE ** (-jnp.arange(nf, dtype=F32) / nf)
    ang_r = (t // GRID_W).astype(F32)[:, None] * freqs
    ang_c = (t % GRID_W).astype(F32)[:, None] * freqs
    cos = jnp.concatenate([jnp.cos(ang_r), jnp.cos(ang_r), jnp.cos(ang_c), jnp.cos(ang_c)], axis=-1)
    sin = jnp.concatenate([-jnp.sin(ang_r), jnp.sin(ang_r), -jnp.sin(ang_c), jnp.sin(ang_c)], axis=-1)
    return cos, sin


def _gate_rows(g, batch):
    return g[:, :N_GATES].reshape(batch, -1, 4, M_HEADS).transpose(0, 3, 2, 1)


def kernel(x, c, ctx, c_ctx, w_mod, b_mod, norm_g, w_in, m_gate_b, m_norm_g, attn_sink,
           w_out_m, w_out_a, w_o, w_ff1, w_ff2):
    batch, seq, d = x.shape
    n_ctx = ctx.shape[1]
    x2 = x.reshape(batch * seq, d)
    ctx2 = ctx.reshape(batch * n_ctx, d)

    w_in_t = w_in[0].T
    w_gate_t = jnp.pad(w_in_t[W_IN_GATE_LO:W_IN_GATE_HI], ((0, GATE_PAD - N_GATES), (0, 0))).astype(BF16)
    gate_bias = jnp.pad(m_gate_b[0].reshape(1, N_GATES), ((0, 0), (0, GATE_PAD - N_GATES)))

    c_all = jnp.zeros((MOD_ROWS, d), F32).at[:batch].set(c).at[batch].set(c_ctx)
    mod = _modulation(c_all, w_mod[0], b_mod[0]).reshape(MOD_ROWS, 6, 1, d)
    gn = norm_g[0].reshape(4, 1, d)

    u = _prenorm(x2, gn, mod, seq)
    uc = _prenorm(ctx2, gn, mod, n_ctx, fixed_row=batch)

    proj = _stream_matmul(u, w_in_t, n_out=PROJ_ALIGNED_W, out_dtype=BF16, transposed=True, name="in_proj")
    g_lat = _gate_proj(u, w_gate_t, gate_bias)
    ctx_tn = 512
    lat_tiles = (OFF_OM - OFF_KM) // ctx_tn
    proj_ctx = _stream_matmul(
        uc, w_in_t, n_out=CTX_PROJ_W, out_dtype=BF16, transposed=True, tn=ctx_tn, name="in_proj_ctx",
        tile_of=lambda j: jnp.where(j < lat_tiles, j + OFF_KM // ctx_tn, j - lat_tiles + OFF_KA // ctx_tn))
    g_ctx = _gate_proj(uc, w_gate_t, gate_bias, name="gate_proj_ctx")

    g_rows, g_rows_ctx = _gate_rows(g_lat, batch), _gate_rows(g_ctx, batch)
    hb = _mlstm(proj, proj_ctx, g_rows, g_rows_ctx, batch, reverse=True)
    hm = _mlstm(proj, proj_ctx, g_rows, g_rows_ctx, batch, reverse=False, readout=(hb, m_norm_g[0]))

    cos, sin = _rope_tables(seq)
    at = _attention(proj, proj_ctx, attn_sink[0], cos, sin, batch)

    zm = _stream_matmul(hm, w_out_m, n_out=d, out_dtype=BF16, epilogue=_gate_epilogue,
                        extra=[proj], extra_off=[OFF_BGM], name="out_m")
    z = _stream_matmul(at, w_out_a, n_out=d, out_dtype=BF16, epilogue=_gate_add_epilogue,
                       extra=[proj, zm], extra_off=[OFF_BGA, 0], name="out_a_merge")
    mix = _stream_matmul(z, w_o, n_out=d, out_dtype=BF16, name="out_proj")
    h1, u2 = _post_mix(x2, mix, gn, mod, seq)

    hid, w_ff2_bf16 = _stream_matmul(u2, w_ff1, n_out=D_FF, out_dtype=BF16, epilogue=_relu2_epilogue,
                                     side=w_ff2, name="ff1")
    y = _matmul_ksplit(hid, w_ff2_bf16, tm=1024, tn=1024, tk=4096, name="ff2")
    out = _post_mlp(h1, y, gn, mod, seq)
    return out.reshape(batch, seq, d)
```

```python
import functools

import jax
import jax.numpy as jnp
from jax import lax
from jax.experimental import pallas as pl
from jax.experimental.pallas import tpu as pltpu

F32 = jnp.float32
BF16 = jnp.bfloat16

D_MODEL = 4096
GRID_W = 64
CTX_LEN = 256
M_HEADS = 8
M_DQK = 256
M_DV = 512
A_HEADS = 32
A_KV_HEADS = 8
A_HEAD_DIM = 128
A_GROUP = A_HEADS // A_KV_HEADS
WINDOW = 128
A_BLOCK = 128
ROPE_BASE = 10000.0
D_FF = 4 * D_MODEL
EPS = 1e-6

M_QK_W = M_HEADS * M_DQK
M_V_W = M_HEADS * M_DV
A_Q_W = A_HEADS * A_HEAD_DIM
A_KV_W = A_KV_HEADS * A_HEAD_DIM
N_GATES = 4 * M_HEADS

W_IN_GATE_LO = 2 * M_QK_W + 2 * M_V_W
W_IN_GATE_HI = W_IN_GATE_LO + N_GATES
OFF_QM = 0
OFF_KM = OFF_QM + M_QK_W
OFF_VM = OFF_KM + M_QK_W
OFF_OM = OFF_VM + M_V_W
OFF_QA = OFF_OM + M_V_W
OFF_KA = OFF_QA + A_Q_W
OFF_VA = OFF_KA + A_KV_W
OFF_BGM = OFF_VA + A_KV_W
OFF_BGA = OFF_BGM + D_MODEL
PROJ_ALIGNED_W = OFF_BGA + D_MODEL
CTX_OFF_KM = 0
CTX_OFF_VM = M_QK_W
CTX_OFF_KA = M_QK_W + M_V_W
CTX_OFF_VA = CTX_OFF_KA + A_KV_W
CTX_PROJ_W = CTX_OFF_VA + A_KV_W

M_CHUNK = 256
M_HEADS_PER_STEP = 8
M_CTX_V_PARTS = 2
Q_SCALE = M_DQK ** -0.5
LANES = 128
GATE_PAD = LANES
MOD_ROWS = 8
VMEM_LIMIT = 60000 * 1024


def _params(n_axes):
    return pltpu.CompilerParams(dimension_semantics=("arbitrary",) * n_axes, vmem_limit_bytes=VMEM_LIMIT)


def _sigmoid(x):
    return 1.0 / (1.0 + jnp.exp(-x))


def _log_sigmoid(x):
    return jnp.minimum(x, 0.0) - jnp.log1p(jnp.exp(-jnp.abs(x)))


def _rms(x):
    return x * lax.rsqrt(jnp.mean(x * x, axis=-1, keepdims=True) + EPS)


def _mod_kernel(c_ref, w_ref, b_ref, o_ref):
    c = c_ref[...]
    sc = (c * _sigmoid(c)).astype(BF16)
    o_ref[...] = jnp.dot(sc, w_ref[...].astype(BF16), preferred_element_type=F32) + b_ref[...]


def _modulation(c_all, w_mod, b_mod, tn=1024):
    d, n = w_mod.shape
    return pl.pallas_call(
        _mod_kernel,
        grid=(n // tn,),
        in_specs=[pl.BlockSpec((MOD_ROWS, d), lambda j: (0, 0)),
                  pl.BlockSpec((d, tn), lambda j: (0, j)),
                  pl.BlockSpec((1, tn), lambda j: (0, j))],
        out_specs=pl.BlockSpec((MOD_ROWS, tn), lambda j: (0, j)),
        out_shape=jax.ShapeDtypeStruct((MOD_ROWS, n), F32),
        compiler_params=_params(1),
        name="modulation",
    )(c_all, w_mod, b_mod.reshape(1, n))


def _prenorm_kernel(x_ref, g_ref, sh_ref, sc_ref, o_ref):
    y = _rms(x_ref[...]) * g_ref[...]
    o_ref[...] = (y * (1.0 + sc_ref[...]) + sh_ref[...]).astype(o_ref.dtype)


def _mod_spec(slot, rows_per_mod_row, tm, fixed_row=None):
    if fixed_row is None:
        return pl.BlockSpec((None, None, 1, D_MODEL), lambda i: ((i * tm) // rows_per_mod_row, slot, 0, 0))
    return pl.BlockSpec((None, None, 1, D_MODEL), lambda i: (fixed_row, slot, 0, 0))


def _norm_spec(slot):
    return pl.BlockSpec((None, 1, D_MODEL), lambda i: (slot, 0, 0))


def _prenorm(x2, norm_g, mod, rows_per_mod_row, fixed_row=None, tm=512):
    m = x2.shape[0]
    return pl.pallas_call(
        _prenorm_kernel,
        grid=(m // tm,),
        in_specs=[pl.BlockSpec((tm, D_MODEL), lambda i: (i, 0)),
                  _norm_spec(0),
                  _mod_spec(0, rows_per_mod_row, tm, fixed_row),
                  _mod_spec(1, rows_per_mod_row, tm, fixed_row)],
        out_specs=pl.BlockSpec((tm, D_MODEL), lambda i: (i, 0)),
        out_shape=jax.ShapeDtypeStruct((m, D_MODEL), BF16),
        compiler_params=_params(1),
        name="prenorm",
    )(x2, norm_g, mod, mod)


def _post_mix_kernel(x_ref, mix_ref, gn1_ref, gn2_ref, g1_ref, sh2_ref, s2_ref, h1_ref, u2_ref):
    h1 = x_ref[...] + g1_ref[...] * (_rms(mix_ref[...].astype(F32)) * gn1_ref[...])
    h1_ref[...] = h1
    y2 = _rms(h1) * gn2_ref[...]
    u2_ref[...] = (y2 * (1.0 + s2_ref[...]) + sh2_ref[...]).astype(u2_ref.dtype)


def _post_mix(x2, mix, norm_g, mod, rows_per_mod_row, tm=512):
    m = x2.shape[0]
    row = pl.BlockSpec((tm, D_MODEL), lambda i: (i, 0))
    return pl.pallas_call(
        _post_mix_kernel,
        grid=(m // tm,),
        in_specs=[row, row, _norm_spec(1), _norm_spec(2),
                  _mod_spec(2, rows_per_mod_row, tm), _mod_spec(3, rows_per_mod_row, tm),
                  _mod_spec(4, rows_per_mod_row, tm)],
        out_specs=[row, row],
        out_shape=[jax.ShapeDtypeStruct((m, D_MODEL), F32), jax.ShapeDtypeStruct((m, D_MODEL), BF16)],
        compiler_params=_params(1),
        name="post_mix",
    )(x2, mix, norm_g, norm_g, mod, mod, mod)


def _post_mlp_kernel(h1_ref, y_ref, gn_ref, g2_ref, o_ref):
    o_ref[...] = h1_ref[...] + g2_ref[...] * (_rms(y_ref[...].astype(F32)) * gn_ref[...])


def _post_mlp(h1, y, norm_g, mod, rows_per_mod_row, tm=512):
    m = h1.shape[0]
    row = pl.BlockSpec((tm, D_MODEL), lambda i: (i, 0))
    return pl.pallas_call(
        _post_mlp_kernel,
        grid=(m // tm,),
        in_specs=[row, row, _norm_spec(3), _mod_spec(5, rows_per_mod_row, tm)],
        out_specs=row,
        out_shape=jax.ShapeDtypeStruct((m, D_MODEL), F32),
        compiler_params=_params(1),
        name="post_mlp",
    )(h1, y, norm_g, mod)


_NT = (((1,), (1,)), ((), ()))


def _stream_kernel(a_ref, slab_ref, *rest, nj, transposed, tile_of, epilogue, n_extra, has_next, has_side):
    j = pl.program_id(0)
    i = pl.program_id(1)
    rest = list(rest)
    next_ref = rest.pop(0) if has_next else None
    extra = [rest.pop(0) for _ in range(n_extra)]
    side_ref = rest.pop(0) if has_side else None
    o_ref = rest.pop(0)
    side_out_ref = rest.pop(0) if has_side else None
    w_scr = rest.pop(0)
    rows = slab_ref.shape[0]
    r0 = pl.multiple_of(i * rows, rows)

    def cast_slab():
        dst = w_scr.at[j % 2]
        if has_next:
            tile_rows = w_scr.shape[1]
            shifted = tile_of(jnp.minimum(j, nj - 1)) * tile_rows >= W_IN_GATE_LO
            body = rows - N_GATES
            start = pl.multiple_of(jnp.where(shifted, N_GATES, 0), N_GATES)
            dst[pl.ds(r0, body), :] = slab_ref[pl.ds(start, body), :].astype(BF16)
            tail = jnp.where(shifted, next_ref[...], slab_ref[body:, :])
            dst[pl.ds(r0 + body, N_GATES), :] = tail.astype(BF16)
        else:
            dst[pl.ds(r0, rows), :] = slab_ref[...].astype(BF16)

    @pl.when(j == 0)
    def _():
        cast_slab()

    @pl.when(j > 0)
    def _():
        cast_slab()
        w = w_scr[(j - 1) % 2]
        if transposed:
            acc = lax.dot_general(a_ref[...], w, _NT, preferred_element_type=F32)
        else:
            acc = jnp.dot(a_ref[...], w, preferred_element_type=F32)
        o_ref[...] = epilogue(acc, *[e[...] for e in extra]).astype(o_ref.dtype)
        if has_side:
            side_out_ref[...] = side_ref[...].astype(side_out_ref.dtype)


def _stream_matmul(a, w, *, n_out, out_dtype, epilogue=lambda acc: acc, extra=(), extra_off=(), transposed=False,
                   tile_of=None, side=None, tm=1024, tn=1024, name):
    m, k = a.shape
    nj, ni = n_out // tn, m // tm
    tile_of = (lambda j: j) if tile_of is None else tile_of

    def prev(j):
        return jnp.maximum(j - 1, 0)

    def cur(j):
        return jnp.minimum(j, nj - 1)

    def row(j, i):
        return jnp.where(j == 0, 0, i)

    in_specs = [pl.BlockSpec((tm, k), lambda j, i: (row(j, i), 0))]
    args = [a]
    if transposed:
        slab_rows = tn // ni
        in_specs.append(pl.BlockSpec((slab_rows, k), lambda j, i: (tile_of(cur(j)) * ni + i, 0)))
        in_specs.append(pl.BlockSpec(
            (N_GATES, k), lambda j, i: ((tile_of(cur(j)) * tn + (i + 1) * slab_rows) // N_GATES, 0)))
        args += [w, w]
        scratch = pltpu.VMEM((2, tn, k), BF16)
    else:
        in_specs.append(pl.BlockSpec((None, k // ni, tn), lambda j, i: (0, i, cur(j))))
        args.append(w)
        scratch = pltpu.VMEM((2, k, tn), BF16)
    for arr, off in zip(extra, extra_off):
        in_specs.append(pl.BlockSpec((tm, tn), lambda j, i, off=off: (row(j, i), off // tn + prev(j))))
        args.append(arr)
    out_specs = [pl.BlockSpec((tm, tn), lambda j, i: (row(j, i), prev(j)))]
    out_shape = [jax.ShapeDtypeStruct((m, n_out), out_dtype)]
    if side is not None:
        s_rows, s_cols = side.shape[1:]
        slab = s_rows // (nj * ni)

        def step(j, i):
            return jnp.maximum((j - 1) * ni + i, 0)

        in_specs.append(pl.BlockSpec((None, slab, s_cols), lambda j, i: (0, step(j, i), 0)))
        args.append(side)
        out_specs.append(pl.BlockSpec((slab, s_cols), lambda j, i: (step(j, i), 0)))
        out_shape.append(jax.ShapeDtypeStruct((s_rows, s_cols), BF16))
    out = pl.pallas_call(
        functools.partial(_stream_kernel, nj=nj, transposed=transposed, tile_of=tile_of, epilogue=epilogue,
                          n_extra=len(extra), has_next=transposed, has_side=side is not None),
        grid=(nj + 1, ni),
        in_specs=in_specs,
        out_specs=out_specs,
        out_shape=out_shape,
        scratch_shapes=[scratch],
        compiler_params=_params(2),
        name=name,
    )(*args)
    return out if side is not None else out[0]


def _gate_epilogue(acc, g):
    return _sigmoid(g.astype(F32)) * acc


def _gate_add_epilogue(acc, g, prev):
    return prev.astype(F32) + _sigmoid(g.astype(F32)) * acc


def _relu2_epilogue(acc):
    return jnp.square(jnp.maximum(acc, 0.0))


def _gate_table(g):
    length = g.shape[0]
    lf = _log_sigmoid(g)
    row = lax.broadcasted_iota(jnp.int32, g.shape, 0)
    col = lax.broadcasted_iota(jnp.int32, g.shape, 1)
    prefix = lf
    shift = 1
    while shift < length:
        prefix = prefix + jnp.where(row >= shift, pltpu.roll(prefix, shift, 0), 0.0)
        shift *= 2
    suffix = prefix[length - 1:length, :] - prefix + lf
    is_f_fwd = (col >= M_HEADS) & (col < 2 * M_HEADS)
    is_f_bwd = (col >= 3 * M_HEADS) & (col < 4 * M_HEADS)
    return jnp.where(is_f_fwd, prefix, jnp.where(is_f_bwd, suffix, g))


def _gate_proj_kernel(a_ref, w_ref, b_ref, o_ref):
    g = lax.dot_general(a_ref[...], w_ref[...], _NT, preferred_element_type=F32) + b_ref[...]
    for c in range(g.shape[0] // M_CHUNK):
        rows = slice(c * M_CHUNK, (c + 1) * M_CHUNK)
        o_ref[rows, :] = _gate_table(g[rows, :])


def _gate_proj(a, w_gate_t, bias, tm=1024, name="gate_proj"):
    m, k = a.shape
    n = w_gate_t.shape[0]
    return pl.pallas_call(
        _gate_proj_kernel,
        grid=(m // tm,),
        in_specs=[pl.BlockSpec((tm, k), lambda i: (i, 0)),
                  pl.BlockSpec((n, k), lambda i: (0, 0)),
                  pl.BlockSpec((1, n), lambda i: (0, 0))],
        out_specs=pl.BlockSpec((tm, n), lambda i: (i, 0)),
        out_shape=jax.ShapeDtypeStruct((m, n), F32),
        compiler_params=_params(1),
        name=name,
    )(a, w_gate_t, bias)


def _matmul_ksplit_kernel(a_ref, w_ref, o_ref, acc_ref):
    kk = pl.program_id(2)

    @pl.when(kk == 0)
    def _():
        acc_ref[...] = jnp.zeros_like(acc_ref)

    acc_ref[...] += jnp.dot(a_ref[...], w_ref[...], preferred_element_type=F32)

    @pl.when(kk == pl.num_programs(2) - 1)
    def _():
        o_ref[...] = acc_ref[...].astype(o_ref.dtype)


def _matmul_ksplit(a, w, *, tm, tn, tk, name):
    m, k = a.shape
    n = w.shape[1]
    return pl.pallas_call(
        _matmul_ksplit_kernel,
        grid=(n // tn, m // tm, k // tk),
        in_specs=[pl.BlockSpec((tm, tk), lambda j, i, kk: (i, kk)),
                  pl.BlockSpec((tk, tn), lambda j, i, kk: (kk, j))],
        out_specs=pl.BlockSpec((tm, tn), lambda j, i, kk: (i, j)),
        out_shape=jax.ShapeDtypeStruct((m, n), BF16),
        scratch_shapes=[pltpu.VMEM((tm, tn), F32)],
        compiler_params=_params(3),
        name=name,
    )(a, w)


def _tile_lanes(x, width):
    return jnp.concatenate([x] * (width // x.shape[1]), axis=1)


def _rows_to_replicated_columns(x_row, y_row):
    length = x_row.shape[1]

    def pieces(v):
        hi = v.astype(BF16)
        rest = v - hi.astype(F32)
        mid = rest.astype(BF16)
        return [hi, mid, (rest - mid.astype(F32)).astype(BF16)]

    n_rows = 16
    stacked = jnp.concatenate(
        pieces(x_row) + pieces(y_row) + [jnp.zeros((n_rows - 6, length), BF16)], axis=0)
    piece = lax.broadcasted_iota(jnp.int32, (n_rows, 2 * LANES), 0)
    lane = lax.broadcasted_iota(jnp.int32, (n_rows, 2 * LANES), 1)
    selector = jnp.where((lane < LANES) == (piece < 3), 1.0, 0.0)
    selector = jnp.where(piece < 6, selector, 0.0).astype(BF16)
    both = lax.dot_general(stacked, selector, (((0,), (0,)), ((), ())), preferred_element_type=F32)
    return both[:, :LANES], both[:, LANES:]


def _mlstm_chunk(q_ref, k_ref, v_ref, gr_ref, ct_ref, m_ref, h_ref, *, reverse, readout=None):
    length = k_ref.shape[0]
    dv = v_ref.shape[1]
    k = k_ref[...]
    v_aug = jnp.concatenate([v_ref[...], jnp.ones((length, LANES), BF16)], axis=1)
    if h_ref is not None:
        q = q_ref[...]
        qk = lax.dot_general(q, k, _NT, preferred_element_type=F32)
        ct = ct_ref[...]
        n_hi = ct[:, dv:].astype(BF16)
        n_mid = (ct[:, dv:] - n_hi.astype(F32)).astype(BF16)
        qc_aug = jnp.dot(q, jnp.concatenate([ct[:, :dv].astype(BF16), n_hi, n_mid], axis=1),
                         preferred_element_type=F32)
        qn = qc_aug[:, dv:dv + LANES] + qc_aug[:, dv + LANES:]
    yield

    ji, jf = (2, 3) if reverse else (0, 1)
    gr = gr_ref[...]
    ig_r = gr[ji:ji + 1, :]
    b_r = gr[jf:jf + 1, :]
    b_c, ig_c = _rows_to_replicated_columns(b_r, ig_r)
    b_tot = b_r[:, 0:1] if reverse else b_r[:, length - 1:length]
    m_prev = m_ref[...]
    w_c = b_tot - b_c + ig_c
    m_new = jnp.maximum(b_tot + m_prev, jnp.max(w_c, axis=0, keepdims=True)[:, 0:1])
    decay = jnp.exp(b_tot + m_prev - m_new)
    kw = k.astype(F32) * _tile_lanes(jnp.exp(w_c - m_new), k.shape[1])
    if h_ref is not None:
        t_idx = lax.broadcasted_iota(jnp.int32, (length, length), 0)
        s_idx = lax.broadcasted_iota(jnp.int32, (length, length), 1)
        seen = (s_idx >= t_idx) if reverse else (s_idx <= t_idx)
        a_c = b_c + m_prev
        dm = jnp.where(seen, _tile_lanes(b_c, length) - b_r + ig_r, -jnp.inf)
        m_t = jnp.maximum(a_c, jnp.broadcast_to(jnp.max(dm, axis=1, keepdims=True), (length, LANES)))
        s = qk * Q_SCALE * jnp.exp(dm - _tile_lanes(m_t, length))
        inter = jnp.exp(a_c - m_t) * Q_SCALE
        s_sum = jnp.broadcast_to(jnp.sum(s, axis=1, keepdims=True), (length, LANES))
        den = inter * qn + s_sum
        scale = 1.0 / jnp.maximum(jnp.abs(den), jnp.exp(-m_t))
    yield

    if h_ref is not None:
        sv = jnp.dot(s.astype(BF16), v_aug[:, :dv], preferred_element_type=F32)
        num = _tile_lanes(inter, dv) * qc_aug[:, :dv] + sv
        h = num * _tile_lanes(scale, dv)
        if readout is not None:
            other_ref, om_ref, gain_ref = readout
            h = _rms(h + other_ref[...].astype(F32)) * gain_ref[...] * _sigmoid(om_ref[...].astype(F32))
        h_ref[...] = h.astype(h_ref.dtype)
    ct_ref[...] = decay * ct_ref[...] + lax.dot_general(
        kw.astype(BF16), v_aug, (((0,), (0,)), ((), ())), preferred_element_type=F32)
    m_ref[...] = m_new
    yield


def _interleave(*chunks):
    for _ in range(3):
        for chunk in chunks:
            next(chunk)


def _mlstm_kernel(q_ref, k_ref, v_ref, gr_ref, kc_ref, grc_ref, *rest, reverse, fused):
    c = pl.program_id(2)
    rest = list(rest)
    vc_refs = [rest.pop(0) for _ in range(M_CTX_V_PARTS)]
    other_ref, om_ref, gain_ref = (rest.pop(0), rest.pop(0), rest.pop(0)) if fused else (None, None, None)
    h_ref = rest.pop(0)
    states = [rest[2 * i:2 * i + 2] for i in range(M_HEADS_PER_STEP)]

    def head(ref, i, width):
        return ref.at[:, i * width:(i + 1) * width]

    @pl.when(c == 0)
    def _():
        for ref in rest:
            ref[...] = jnp.zeros_like(ref)
        per = M_HEADS_PER_STEP // M_CTX_V_PARTS
        _interleave(*[_mlstm_chunk(None, head(kc_ref, i, M_DQK), head(vc_refs[i // per], i % per, M_DV), grc_ref.at[i],
                                   *states[i], None, reverse=reverse) for i in range(M_HEADS_PER_STEP)])

    @pl.when(c > 0)
    def _():
        chunks = []
        for i in range(M_HEADS_PER_STEP):
            readout = (head(other_ref, i, M_DV), head(om_ref, i, M_DV), head(gain_ref, i, M_DV)) if fused else None
            chunks.append(_mlstm_chunk(head(q_ref, i, M_DQK), head(k_ref, i, M_DQK), head(v_ref, i, M_DV),
                                       gr_ref.at[i], *states[i], head(h_ref, i, M_DV),
                                       reverse=reverse, readout=readout))
        _interleave(*chunks)


def _mlstm(proj, proj_ctx, gr, gr_ctx, batch, *, reverse, readout=None):
    length = M_CHUNK
    assert CTX_LEN == length
    seq = proj.shape[0] // batch
    nc = seq // length
    hps = M_HEADS_PER_STEP
    qk_w, v_w = hps * M_DQK, hps * M_DV

    def chunk_of(c):
        return (nc - jnp.maximum(c, 1)) if reverse else jnp.maximum(c - 1, 0)

    def rows(b, c):
        return b * nc + chunk_of(c)

    in_specs = [
        pl.BlockSpec((length, qk_w), lambda b, h, c: (rows(b, c), OFF_QM // qk_w + h)),
        pl.BlockSpec((length, qk_w), lambda b, h, c: (rows(b, c), OFF_KM // qk_w + h)),
        pl.BlockSpec((length, v_w), lambda b, h, c: (rows(b, c), OFF_VM // v_w + h)),
        pl.BlockSpec((None, hps, 4, length), lambda b, h, c: (b, h, 0, chunk_of(c))),
        pl.BlockSpec((length, qk_w), lambda b, h, c: (b, CTX_OFF_KM // qk_w + h)),
        pl.BlockSpec((None, hps, 4, length), lambda b, h, c: (b, h, 0, 0)),
    ]
    vc_w = v_w // M_CTX_V_PARTS
    assert CTX_OFF_VM % vc_w == 0
    for part in range(M_CTX_V_PARTS):
        in_specs.append(pl.BlockSpec(
            (length, vc_w), lambda b, h, c, part=part: (b, CTX_OFF_VM // vc_w + h * M_CTX_V_PARTS + part)))
    args = [proj, proj, proj, gr, proj_ctx, gr_ctx] + [proj_ctx] * M_CTX_V_PARTS
    if readout is not None:
        other, gain = readout
        in_specs += [pl.BlockSpec((length, v_w), lambda b, h, c: (rows(b, c), h)),
                     pl.BlockSpec((length, v_w), lambda b, h, c: (rows(b, c), OFF_OM // v_w + h)),
                     pl.BlockSpec((1, v_w), lambda b, h, c: (0, h))]
        args += [other, proj, gain.reshape(1, M_V_W)]
    state = [pltpu.VMEM((M_DQK, M_DV + LANES), F32), pltpu.VMEM((1, 1), F32)]
    return pl.pallas_call(
        functools.partial(_mlstm_kernel, reverse=reverse, fused=readout is not None),
        grid=(batch, M_HEADS // hps, nc + 1),
        in_specs=in_specs,
        out_specs=pl.BlockSpec((length, v_w), lambda b, h, c: (rows(b, c), h)),
        out_shape=jax.ShapeDtypeStruct((batch * seq, M_V_W), BF16),
        scratch_shapes=state * hps,
        compiler_params=_params(3),
        name="mlstm_bwd" if reverse else "mlstm_fwd",
    )(*args)


def _rope(x, cos, sin_signed):
    lane = lax.broadcasted_iota(jnp.int32, x.shape, 1)
    partner = jnp.where((lane % 64) < 32, pltpu.roll(x, LANES - 32, 1), pltpu.roll(x, 32, 1))
    return x * cos + partner * sin_signed


def _attn_kernel(sink_ref, q_ref, k_ref, v_ref, kc_ref, vc_ref, cos_ref, sin_ref, o_ref,
                 kpad, vpad, s_buf, p_buf, den_buf):
    seq = k_ref.shape[0]
    nb = seq // A_BLOCK
    kvh = pl.program_id(1)
    scale = A_HEAD_DIM ** -0.5
    rows = A_GROUP * A_BLOCK
    n_win = 3 * A_BLOCK
    zeros = jnp.zeros((A_BLOCK, A_HEAD_DIM), BF16)
    kpad[0:A_BLOCK, :] = zeros
    kpad[A_BLOCK + seq:, :] = zeros
    vpad[0:A_BLOCK, :] = zeros
    vpad[A_BLOCK + seq:, :] = zeros
    kpad[A_BLOCK:A_BLOCK + seq, :] = _rope(k_ref[...].astype(F32), cos_ref[...], sin_ref[...]).astype(BF16)
    vpad[A_BLOCK:A_BLOCK + seq, :] = v_ref[...]
    qi = lax.broadcasted_iota(jnp.int32, (rows, n_win), 0) & (A_BLOCK - 1)
    kj = lax.broadcasted_iota(jnp.int32, (rows, n_win), 1)
    band = (kj >= qi) & (kj <= qi + 2 * WINDOW)
    kj_row = lax.broadcasted_iota(jnp.int32, (1, n_win), 1)
    sink = jnp.concatenate(
        [jnp.full((A_BLOCK, 1), sink_ref[kvh * A_GROUP + g], F32) for g in range(A_GROUP)], axis=0)
    nt = (((1,), (1,)), ((), ()))

    def scores(n):
        r0 = pl.multiple_of(n * A_BLOCK, A_BLOCK)
        cos = cos_ref[pl.ds(r0, A_BLOCK), :] * scale
        sin = sin_ref[pl.ds(r0, A_BLOCK), :] * scale
        q = jnp.concatenate(
            [_rope(q_ref[pl.ds(r0, A_BLOCK), g * A_HEAD_DIM:(g + 1) * A_HEAD_DIM].astype(F32), cos, sin)
             for g in range(A_GROUP)], axis=0).astype(BF16)
        kpos = r0 - A_BLOCK + kj_row
        valid = band & ((kpos >= 0) & (kpos < seq))
        sw = lax.dot_general(q, kpad[pl.ds(r0, n_win), :], nt, preferred_element_type=F32)
        s_buf[:, 0:n_win] = jnp.where(valid, sw, -jnp.inf)
        s_buf[:, n_win:] = lax.dot_general(q, kc_ref[...], nt, preferred_element_type=F32)

    def softmax():
        s = s_buf[...]
        mx = jnp.maximum(jnp.max(s, axis=1, keepdims=True), sink)
        p = jnp.exp(s - mx)
        den_buf[...] = jnp.sum(p, axis=1, keepdims=True) + jnp.exp(sink - mx)
        p_buf[...] = p.astype(p_buf.dtype)

    def output(n):
        r0 = pl.multiple_of(n * A_BLOCK, A_BLOCK)
        o = (jnp.dot(p_buf[:, 0:n_win], vpad[pl.ds(r0, n_win), :], preferred_element_type=F32)
             + jnp.dot(p_buf[:, n_win:], vc_ref[...], preferred_element_type=F32)) * (1.0 / den_buf[...])
        for g in range(A_GROUP):
            o_ref[pl.ds(r0, A_BLOCK), g * A_HEAD_DIM:(g + 1) * A_HEAD_DIM] = (
                o[g * A_BLOCK:(g + 1) * A_BLOCK, :].astype(o_ref.dtype))

    scores(0)
    softmax()
    scores(1)

    def body(t, carry):
        output(t - 1)
        softmax()
        scores(t + 1)
        return carry

    lax.fori_loop(1, nb - 1, body, 0)
    output(nb - 2)
    softmax()
    output(nb - 1)


def _attention(proj, proj_ctx, sink, cos, sin, batch):
    seq = proj.shape[0] // batch
    qw = A_GROUP * A_HEAD_DIM
    tab = pl.BlockSpec((seq, A_HEAD_DIM), lambda b, h: (0, 0))
    return pl.pallas_call(
        _attn_kernel,
        grid=(batch, A_KV_HEADS),
        in_specs=[pl.BlockSpec(memory_space=pltpu.SMEM),
                  pl.BlockSpec((seq, qw), lambda b, h: (b, OFF_QA // qw + h)),
                  pl.BlockSpec((seq, A_HEAD_DIM), lambda b, h: (b, OFF_KA // A_HEAD_DIM + h)),
                  pl.BlockSpec((seq, A_HEAD_DIM), lambda b, h: (b, OFF_VA // A_HEAD_DIM + h)),
                  pl.BlockSpec((CTX_LEN, A_HEAD_DIM), lambda b, h: (b, CTX_OFF_KA // A_HEAD_DIM + h)),
                  pl.BlockSpec((CTX_LEN, A_HEAD_DIM), lambda b, h: (b, CTX_OFF_VA // A_HEAD_DIM + h)),
                  tab, tab],
        out_specs=pl.BlockSpec((seq, qw), lambda b, h: (b, h)),
        out_shape=jax.ShapeDtypeStruct((batch * seq, A_Q_W), BF16),
        scratch_shapes=[pltpu.VMEM((seq + 2 * A_BLOCK, A_HEAD_DIM), BF16),
                        pltpu.VMEM((seq + 2 * A_BLOCK, A_HEAD_DIM), BF16),
                        pltpu.VMEM((A_GROUP * A_BLOCK, 3 * A_BLOCK + CTX_LEN), F32),
                        pltpu.VMEM((A_GROUP * A_BLOCK, 3 * A_BLOCK + CTX_LEN), BF16),
                        pltpu.VMEM((A_GROUP * A_BLOCK, 1), F32)],
        compiler_params=_params(2),
        name="window_attention",
    )(sink, proj, proj, proj, proj_ctx, proj_ctx, cos, sin)


def _rope_tables(seq):
    t = jnp.arange(seq)
    nf = A_HEAD_DIM // 4
    freqs = ROPE_BASE ** (-jnp.arange(nf, dtype=F32) / nf)
    ang_r = (t // GRID_W).astype(F32)[:, None] * freqs
    ang_c = (t % GRID_W).astype(F32)[:, None] * freqs
    cos = jnp.concatenate([jnp.cos(ang_r), jnp.cos(ang_r), jnp.cos(ang_c), jnp.cos(ang_c)], axis=-1)
    sin = jnp.concatenate([-jnp.sin(ang_r), jnp.sin(ang_r), -jnp.sin(ang_c), jnp.sin(ang_c)], axis=-1)
    return cos, sin


def _gate_rows(g, batch):
    return g[:, :N_GATES].reshape(batch, -1, 4, M_HEADS).transpose(0, 3, 2, 1)


def kernel(x, c, ctx, c_ctx, w_mod, b_mod, norm_g, w_in, m_gate_b, m_norm_g, attn_sink,
           w_out_m, w_out_a, w_o, w_ff1, w_ff2):
    batch, seq, d = x.shape
    n_ctx = ctx.shape[1]
    x2 = x.reshape(batch * seq, d)
    ctx2 = ctx.reshape(batch * n_ctx, d)

    w_in_t = w_in[0].T
    w_gate_t = jnp.pad(w_in_t[W_IN_GATE_LO:W_IN_GATE_HI], ((0, GATE_PAD - N_GATES), (0, 0))).astype(BF16)
    gate_bias = jnp.pad(m_gate_b[0].reshape(1, N_GATES), ((0, 0), (0, GATE_PAD - N_GATES)))

    c_all = jnp.zeros((MOD_ROWS, d), F32).at[:batch].set(c).at[batch].set(c_ctx)
    mod = _modulation(c_all, w_mod[0], b_mod[0]).reshape(MOD_ROWS, 6, 1, d)
    gn = norm_g[0].reshape(4, 1, d)

    u = _prenorm(x2, gn, mod, seq)
    uc = _prenorm(ctx2, gn, mod, n_ctx, fixed_row=batch)

    proj = _stream_matmul(u, w_in_t, n_out=PROJ_ALIGNED_W, out_dtype=BF16, transposed=True, name="in_proj")
    g_lat = _gate_proj(u, w_gate_t, gate_bias)
    ctx_tn = 512
    lat_tiles = (OFF_OM - OFF_KM) // ctx_tn
    proj_ctx = _stream_matmul(
        uc, w_in_t, n_out=CTX_PROJ_W, out_dtype=BF16, transposed=True, tn=ctx_tn, name="in_proj_ctx",
        tile_of=lambda j: jnp.where(j < lat_tiles, j + OFF_KM // ctx_tn, j - lat_tiles + OFF_KA // ctx_tn))
    g_ctx = _gate_proj(uc, w_gate_t, gate_bias, name="gate_proj_ctx")

    g_rows, g_rows_ctx = _gate_rows(g_lat, batch), _gate_rows(g_ctx, batch)
    hb = _mlstm(proj, proj_ctx, g_rows, g_rows_ctx, batch, reverse=True)
    hm = _mlstm(proj, proj_ctx, g_rows, g_rows_ctx, batch, reverse=False, readout=(hb, m_norm_g[0]))

    cos, sin = _rope_tables(seq)
    at = _attention(proj, proj_ctx, attn_sink[0], cos, sin, batch)

    zm = _stream_matmul(hm, w_out_m, n_out=d, out_dtype=BF16, epilogue=_gate_epilogue,
                        extra=[proj], extra_off=[OFF_BGM], name="out_m")
    z = _stream_matmul(at, w_out_a, n_out=d, out_dtype=BF16, epilogue=_gate_add_epilogue,
                       extra=[proj, zm], extra_off=[OFF_BGA, 0], name="out_a_merge")
    mix = _stream_matmul(z, w_o, n_out=d, out_dtype=BF16, name="out_proj")
    h1, u2 = _post_mix(x2, mix, gn, mod, seq)

    hid, w_ff2_bf16 = _stream_matmul(u2, w_ff1, n_out=D_FF, out_dtype=BF16, epilogue=_relu2_epilogue,
                                     side=w_ff2, name="ff1")
    y = _matmul_ksplit(hid, w_ff2_bf16, tm=1024, tn=1024, tk=4096, name="ff2")
    out = _post_mlp(h1, y, gn, mod, seq)
    return out.reshape(batch, seq, d)
```

```python
import functools

import jax
import jax.numpy as jnp
from jax import lax
from jax.experimental import pallas as pl
from jax.experimental.pallas import tpu as pltpu

F32 = jnp.float32
BF16 = jnp.bfloat16

D_MODEL = 4096
GRID_W = 64
CTX_LEN = 256
M_HEADS = 8
M_DQK = 256
M_DV = 512
A_HEADS = 32
A_KV_HEADS = 8
A_HEAD_DIM = 128
A_GROUP = A_HEADS // A_KV_HEADS
WINDOW = 128
A_BLOCK = 128
ROPE_BASE = 10000.0
D_FF = 4 * D_MODEL
EPS = 1e-6

M_QK_W = M_HEADS * M_DQK
M_V_W = M_HEADS * M_DV
A_Q_W = A_HEADS * A_HEAD_DIM
A_KV_W = A_KV_HEADS * A_HEAD_DIM
N_GATES = 4 * M_HEADS

W_IN_GATE_LO = 2 * M_QK_W + 2 * M_V_W
W_IN_GATE_HI = W_IN_GATE_LO + N_GATES
OFF_QM = 0
OFF_KM = OFF_QM + M_QK_W
OFF_VM = OFF_KM + M_QK_W
OFF_OM = OFF_VM + M_V_W
OFF_QA = OFF_OM + M_V_W
OFF_KA = OFF_QA + A_Q_W
OFF_VA = OFF_KA + A_KV_W
OFF_BGM = OFF_VA + A_KV_W
OFF_BGA = OFF_BGM + D_MODEL
PROJ_ALIGNED_W = OFF_BGA + D_MODEL
CTX_OFF_KM = 0
CTX_OFF_VM = M_QK_W
CTX_OFF_KA = M_QK_W + M_V_W
CTX_OFF_VA = CTX_OFF_KA + A_KV_W
CTX_PROJ_W = CTX_OFF_VA + A_KV_W

M_CHUNK = 256
M_HEADS_PER_STEP = 8
M_CTX_V_PARTS = 2
Q_SCALE = M_DQK ** -0.5
LANES = 128
GATE_PAD = LANES
MOD_ROWS = 8
VMEM_LIMIT = 60000 * 1024


def _params(n_axes):
    return pltpu.CompilerParams(dimension_semantics=("arbitrary",) * n_axes, vmem_limit_bytes=VMEM_LIMIT)


def _sigmoid(x):
    return 1.0 / (1.0 + jnp.exp(-x))


def _log_sigmoid(x):
    return jnp.minimum(x, 0.0) - jnp.log1p(jnp.exp(-jnp.abs(x)))


def _rms(x):
    return x * lax.rsqrt(jnp.mean(x * x, axis=-1, keepdims=True) + EPS)


def _mod_kernel(c_ref, w_ref, b_ref, o_ref):
    c = c_ref[...]
    sc = (c * _sigmoid(c)).astype(BF16)
    o_ref[...] = jnp.dot(sc, w_ref[...].astype(BF16), preferred_element_type=F32) + b_ref[...]


def _modulation(c_all, w_mod, b_mod, tn=1024):
    d, n = w_mod.shape
    return pl.pallas_call(
        _mod_kernel,
        grid=(n // tn,),
        in_specs=[pl.BlockSpec((MOD_ROWS, d), lambda j: (0, 0)),
                  pl.BlockSpec((d, tn), lambda j: (0, j)),
                  pl.BlockSpec((1, tn), lambda j: (0, j))],
        out_specs=pl.BlockSpec((MOD_ROWS, tn), lambda j: (0, j)),
        out_shape=jax.ShapeDtypeStruct((MOD_ROWS, n), F32),
        compiler_params=_params(1),
        name="modulation",
    )(c_all, w_mod, b_mod.reshape(1, n))


def _prenorm_kernel(x_ref, g_ref, sh_ref, sc_ref, o_ref):
    y = _rms(x_ref[...]) * g_ref[...]
    o_ref[...] = (y * (1.0 + sc_ref[...]) + sh_ref[...]).astype(o_ref.dtype)


def _mod_spec(slot, rows_per_mod_row, tm, fixed_row=None):
    if fixed_row is None:
        return pl.BlockSpec((None, None, 1, D_MODEL), lambda i: ((i * tm) // rows_per_mod_row, slot, 0, 0))
    return pl.BlockSpec((None, None, 1, D_MODEL), lambda i: (fixed_row, slot, 0, 0))


def _norm_spec(slot):
    return pl.BlockSpec((None, 1, D_MODEL), lambda i: (slot, 0, 0))


def _prenorm(x2, norm_g, mod, rows_per_mod_row, fixed_row=None, tm=512):
    m = x2.shape[0]
    return pl.pallas_call(
        _prenorm_kernel,
        grid=(m // tm,),
        in_specs=[pl.BlockSpec((tm, D_MODEL), lambda i: (i, 0)),
                  _norm_spec(0),
                  _mod_spec(0, rows_per_mod_row, tm, fixed_row),
                  _mod_spec(1, rows_per_mod_row, tm, fixed_row)],
        out_specs=pl.BlockSpec((tm, D_MODEL), lambda i: (i, 0)),
        out_shape=jax.ShapeDtypeStruct((m, D_MODEL), BF16),
        compiler_params=_params(1),
        name="prenorm",
    )(x2, norm_g, mod, mod)


def _post_mix_kernel(x_ref, mix_ref, gn1_ref, gn2_ref, g1_ref, sh2_ref, s2_ref, h1_ref, u2_ref):
    h1 = x_ref[...] + g1_ref[...] * (_rms(mix_ref[...].astype(F32)) * gn1_ref[...])
    h1_ref[...] = h1
    y2 = _rms(h1) * gn2_ref[...]
    u2_ref[...] = (y2 * (1.0 + s2_ref[...]) + sh2_ref[...]).astype(u2_ref.dtype)


def _post_mix(x2, mix, norm_g, mod, rows_per_mod_row, tm=512):
    m = x2.shape[0]
    row = pl.BlockSpec((tm, D_MODEL), lambda i: (i, 0))
    return pl.pallas_call(
        _post_mix_kernel,
        grid=(m // tm,),
        in_specs=[row, row, _norm_spec(1), _norm_spec(2),
                  _mod_spec(2, rows_per_mod_row, tm), _mod_spec(3, rows_per_mod_row, tm),
                  _mod_spec(4, rows_per_mod_row, tm)],
        out_specs=[row, row],
        out_shape=[jax.ShapeDtypeStruct((m, D_MODEL), F32), jax.ShapeDtypeStruct((m, D_MODEL), BF16)],
        compiler_params=_params(1),
        name="post_mix",
    )(x2, mix, norm_g, norm_g, mod, mod, mod)


def _post_mlp_kernel(h1_ref, y_ref, gn_ref, g2_ref, o_ref):
    o_ref[...] = h1_ref[...] + g2_ref[...] * (_rms(y_ref[...].astype(F32)) * gn_ref[...])


def _post_mlp(h1, y, norm_g, mod, rows_per_mod_row, tm=512):
    m = h1.shape[0]
    row = pl.BlockSpec((tm, D_MODEL), lambda i: (i, 0))
    return pl.pallas_call(
        _post_mlp_kernel,
        grid=(m // tm,),
        in_specs=[row, row, _norm_spec(3), _mod_spec(5, rows_per_mod_row, tm)],
        out_specs=row,
        out_shape=jax.ShapeDtypeStruct((m, D_MODEL), F32),
        compiler_params=_params(1),
        name="post_mlp",
    )(h1, y, norm_g, mod)


_NT = (((1,), (1,)), ((), ()))


def _stream_kernel(a_ref, slab_ref, *rest, nj, transposed, tile_of, epilogue, n_extra, has_next, has_side):
    j = pl.program_id(0)
    i = pl.program_id(1)
    rest = list(rest)
    next_ref = rest.pop(0) if has_next else None
    extra = [rest.pop(0) for _ in range(n_extra)]
    side_ref = rest.pop(0) if has_side else None
    o_ref = rest.pop(0)
    side_out_ref = rest.pop(0) if has_side else None
    w_scr = rest.pop(0)
    rows = slab_ref.shape[0]
    r0 = pl.multiple_of(i * rows, rows)

    def cast_slab():
        dst = w_scr.at[j % 2]
        if has_next:
            tile_rows = w_scr.shape[1]
            shifted = tile_of(jnp.minimum(j, nj - 1)) * tile_rows >= W_IN_GATE_LO
            body = rows - N_GATES
            start = pl.multiple_of(jnp.where(shifted, N_GATES, 0), N_GATES)
            dst[pl.ds(r0, body), :] = slab_ref[pl.ds(start, body), :].astype(BF16)
            tail = jnp.where(shifted, next_ref[...], slab_ref[body:, :])
            dst[pl.ds(r0 + body, N_GATES), :] = tail.astype(BF16)
        else:
            dst[pl.ds(r0, rows), :] = slab_ref[...].astype(BF16)

    @pl.when(j == 0)
    def _():
        cast_slab()

    @pl.when(j > 0)
    def _():
        cast_slab()
        w = w_scr[(j - 1) % 2]
        if transposed:
            acc = lax.dot_general(a_ref[...], w, _NT, preferred_element_type=F32)
        else:
            acc = jnp.dot(a_ref[...], w, preferred_element_type=F32)
        o_ref[...] = epilogue(acc, *[e[...] for e in extra]).astype(o_ref.dtype)
        if has_side:
            side_out_ref[...] = side_ref[...].astype(side_out_ref.dtype)


def _stream_matmul(a, w, *, n_out, out_dtype, epilogue=lambda acc: acc, extra=(), extra_off=(), transposed=False,
                   tile_of=None, side=None, tm=1024, tn=1024, name):
    m, k = a.shape
    nj, ni = n_out // tn, m // tm
    tile_of = (lambda j: j) if tile_of is None else tile_of

    def prev(j):
        return jnp.maximum(j - 1, 0)

    def cur(j):
        return jnp.minimum(j, nj - 1)

    def row(j, i):
        return jnp.where(j == 0, 0, i)

    in_specs = [pl.BlockSpec((tm, k), lambda j, i: (row(j, i), 0))]
    args = [a]
    if transposed:
        slab_rows = tn // ni
        in_specs.append(pl.BlockSpec((slab_rows, k), lambda j, i: (tile_of(cur(j)) * ni + i, 0)))
        in_specs.append(pl.BlockSpec(
            (N_GATES, k), lambda j, i: ((tile_of(cur(j)) * tn + (i + 1) * slab_rows) // N_GATES, 0)))
        args += [w, w]
        scratch = pltpu.VMEM((2, tn, k), BF16)
    else:
        in_specs.append(pl.BlockSpec((None, k // ni, tn), lambda j, i: (0, i, cur(j))))
        args.append(w)
        scratch = pltpu.VMEM((2, k, tn), BF16)
    for arr, off in zip(extra, extra_off):
        in_specs.append(pl.BlockSpec((tm, tn), lambda j, i, off=off: (row(j, i), off // tn + prev(j))))
        args.append(arr)
    out_specs = [pl.BlockSpec((tm, tn), lambda j, i: (row(j, i), prev(j)))]
    out_shape = [jax.ShapeDtypeStruct((m, n_out), out_dtype)]
    if side is not None:
        s_rows, s_cols = side.shape[1:]
        slab = s_rows // (nj * ni)

        def step(j, i):
            return jnp.maximum((j - 1) * ni + i, 0)

        in_specs.append(pl.BlockSpec((None, slab, s_cols), lambda j, i: (0, step(j, i), 0)))
        args.append(side)
        out_specs.append(pl.BlockSpec((slab, s_cols), lambda j, i: (step(j, i), 0)))
        out_shape.append(jax.ShapeDtypeStruct((s_rows, s_cols), BF16))
    out = pl.pallas_call(
        functools.partial(_stream_kernel, nj=nj, transposed=transposed, tile_of=tile_of, epilogue=epilogue,
                          n_extra=len(extra), has_next=transposed, has_side=side is not None),
        grid=(nj + 1, ni),
        in_specs=in_specs,
        out_specs=out_specs,
        out_shape=out_shape,
        scratch_shapes=[scratch],
        compiler_params=_params(2),
        name=name,
    )(*args)
    return out if side is not None else out[0]


def _gate_epilogue(acc, g):
    return _sigmoid(g.astype(F32)) * acc


def _gate_add_epilogue(acc, g, prev):
    return prev.astype(F32) + _sigmoid(g.astype(F32)) * acc


def _relu2_epilogue(acc):
    return jnp.square(jnp.maximum(acc, 0.0))


def _gate_table(g):
    length = g.shape[0]
    lf = _log_sigmoid(g)
    row = lax.broadcasted_iota(jnp.int32, g.shape, 0)
    col = lax.broadcasted_iota(jnp.int32, g.shape, 1)
    prefix = lf
    shift = 1
    while shift < length:
        prefix = prefix + jnp.where(row >= shift, pltpu.roll(prefix, shift, 0), 0.0)
        shift *= 2
    suffix = prefix[length - 1:length, :] - prefix + lf
    is_f_fwd = (col >= M_HEADS) & (col < 2 * M_HEADS)
    is_f_bwd = (col >= 3 * M_HEADS) & (col < 4 * M_HEADS)
    return jnp.where(is_f_fwd, prefix, jnp.where(is_f_bwd, suffix, g))


def _gate_proj_kernel(a_ref, w_ref, b_ref, o_ref):
    g = lax.dot_general(a_ref[...], w_ref[...], _NT, preferred_element_type=F32) + b_ref[...]
    for c in range(g.shape[0] // M_CHUNK):
        rows = slice(c * M_CHUNK, (c + 1) * M_CHUNK)
        o_ref[rows, :] = _gate_table(g[rows, :])


def _gate_proj(a, w_gate_t, bias, tm=1024, name="gate_proj"):
    m, k = a.shape
    n = w_gate_t.shape[0]
    return pl.pallas_call(
        _gate_proj_kernel,
        grid=(m // tm,),
        in_specs=[pl.BlockSpec((tm, k), lambda i: (i, 0)),
                  pl.BlockSpec((n, k), lambda i: (0, 0)),
                  pl.BlockSpec((1, n), lambda i: (0, 0))],
        out_specs=pl.BlockSpec((tm, n), lambda i: (i, 0)),
        out_shape=jax.ShapeDtypeStruct((m, n), F32),
        compiler_params=_params(1),
        name=name,
    )(a, w_gate_t, bias)


def _matmul_ksplit_kernel(a_ref, w_ref, o_ref, acc_ref):
    kk = pl.program_id(2)

    @pl.when(kk == 0)
    def _():
        acc_ref[...] = jnp.zeros_like(acc_ref)

    acc_ref[...] += jnp.dot(a_ref[...], w_ref[...], preferred_element_type=F32)

    @pl.when(kk == pl.num_programs(2) - 1)
    def _():
        o_ref[...] = acc_ref[...].astype(o_ref.dtype)


def _matmul_ksplit(a, w, *, tm, tn, tk, name):
    m, k = a.shape
    n = w.shape[1]
    return pl.pallas_call(
        _matmul_ksplit_kernel,
        grid=(n // tn, m // tm, k // tk),
        in_specs=[pl.BlockSpec((tm, tk), lambda j, i, kk: (i, kk)),
                  pl.BlockSpec((tk, tn), lambda j, i, kk: (kk, j))],
        out_specs=pl.BlockSpec((tm, tn), lambda j, i, kk: (i, j)),
        out_shape=jax.ShapeDtypeStruct((m, n), BF16),
        scratch_shapes=[pltpu.VMEM((tm, tn), F32)],
        compiler_params=_params(3),
        name=name,
    )(a, w)


def _matmul_post_mlp_kernel(a_ref, w_ref, h1_ref, gn_ref, g2_ref, o_ref):
    kk = pl.program_id(1)

    @pl.when(kk == 0)
    def _():
        o_ref[...] = jnp.zeros_like(o_ref)

    for c in range(0, o_ref.shape[1], 1024):
        o_ref[:, c:c + 1024] += jnp.dot(a_ref[...], w_ref[:, c:c + 1024], preferred_element_type=F32)

    @pl.when(kk == pl.num_programs(1) - 1)
    def _():
        for r in range(0, o_ref.shape[0], 128):
            o_ref[r:r + 128, :] = h1_ref[r:r + 128, :] + g2_ref[...] * (_rms(o_ref[r:r + 128, :]) * gn_ref[...])


def _matmul_post_mlp(a, w, h1, norm_g, mod, rows_per_mod_row, *, tm, tk, name):
    m, k = a.shape
    row = pl.BlockSpec((tm, D_MODEL), lambda i, kk: (i, 0))
    return pl.pallas_call(
        _matmul_post_mlp_kernel,
        grid=(m // tm, k // tk),
        in_specs=[pl.BlockSpec((tm, tk), lambda i, kk: (i, kk)),
                  pl.BlockSpec((tk, D_MODEL), lambda i, kk: (kk, 0)),
                  row,
                  pl.BlockSpec((None, 1, D_MODEL), lambda i, kk: (3, 0, 0)),
                  pl.BlockSpec((None, None, 1, D_MODEL), lambda i, kk: ((i * tm) // rows_per_mod_row, 5, 0, 0))],
        out_specs=row,
        out_shape=jax.ShapeDtypeStruct((m, D_MODEL), F32),
        compiler_params=_params(2),
        name=name,
    )(a, w, h1, norm_g, mod)


def _tile_lanes(x, width):
    return jnp.concatenate([x] * (width // x.shape[1]), axis=1)


def _rows_to_replicated_columns(x_row, y_row):
    length = x_row.shape[1]

    def pieces(v):
        hi = v.astype(BF16)
        rest = v - hi.astype(F32)
        mid = rest.astype(BF16)
        return [hi, mid, (rest - mid.astype(F32)).astype(BF16)]

    n_rows = 16
    stacked = jnp.concatenate(
        pieces(x_row) + pieces(y_row) + [jnp.zeros((n_rows - 6, length), BF16)], axis=0)
    piece = lax.broadcasted_iota(jnp.int32, (n_rows, 2 * LANES), 0)
    lane = lax.broadcasted_iota(jnp.int32, (n_rows, 2 * LANES), 1)
    selector = jnp.where((lane < LANES) == (piece < 3), 1.0, 0.0)
    selector = jnp.where(piece < 6, selector, 0.0).astype(BF16)
    both = lax.dot_general(stacked, selector, (((0,), (0,)), ((), ())), preferred_element_type=F32)
    return both[:, :LANES], both[:, LANES:]


def _mlstm_chunk(q_ref, k_ref, v_ref, gr_ref, ct_ref, m_ref, h_ref, *, reverse, readout=None):
    length = k_ref.shape[0]
    dv = v_ref.shape[1]
    k = k_ref[...]
    v_aug = jnp.concatenate([v_ref[...], jnp.ones((length, LANES), BF16)], axis=1)
    if h_ref is not None:
        q = q_ref[...]
        qk = lax.dot_general(q, k, _NT, preferred_element_type=F32)
        ct = ct_ref[...]
        n_hi = ct[:, dv:].astype(BF16)
        n_mid = (ct[:, dv:] - n_hi.astype(F32)).astype(BF16)
        qc_aug = jnp.dot(q, jnp.concatenate([ct[:, :dv].astype(BF16), n_hi, n_mid], axis=1),
                         preferred_element_type=F32)
        qn = qc_aug[:, dv:dv + LANES] + qc_aug[:, dv + LANES:]
    yield

    ji, jf = (2, 3) if reverse else (0, 1)
    gr = gr_ref[...]
    ig_r = gr[ji:ji + 1, :]
    b_r = gr[jf:jf + 1, :]
    b_c, ig_c = _rows_to_replicated_columns(b_r, ig_r)
    b_tot = b_r[:, 0:1] if reverse else b_r[:, length - 1:length]
    m_prev = m_ref[...]
    w_c = b_tot - b_c + ig_c
    m_new = jnp.maximum(b_tot + m_prev, jnp.max(w_c, axis=0, keepdims=True)[:, 0:1])
    decay = jnp.exp(b_tot + m_prev - m_new)
    kw = k.astype(F32) * _tile_lanes(jnp.exp(w_c - m_new), k.shape[1])
    if h_ref is not None:
        t_idx = lax.broadcasted_iota(jnp.int32, (length, length), 0)
        s_idx = lax.broadcasted_iota(jnp.int32, (length, length), 1)
        seen = (s_idx >= t_idx) if reverse else (s_idx <= t_idx)
        a_c = b_c + m_prev
        dm = jnp.where(seen, _tile_lanes(b_c, length) - b_r + ig_r, -jnp.inf)
        m_t = jnp.maximum(a_c, jnp.broadcast_to(jnp.max(dm, axis=1, keepdims=True), (length, LANES)))
        s = qk * Q_SCALE * jnp.exp(dm - _tile_lanes(m_t, length))
        inter = jnp.exp(a_c - m_t) * Q_SCALE
        s_sum = jnp.broadcast_to(jnp.sum(s, axis=1, keepdims=True), (length, LANES))
        den = inter * qn + s_sum
        scale = 1.0 / jnp.maximum(jnp.abs(den), jnp.exp(-m_t))
    yield

    if h_ref is not None:
        sv = jnp.dot(s.astype(BF16), v_aug[:, :dv], preferred_element_type=F32)
        num = _tile_lanes(inter, dv) * qc_aug[:, :dv] + sv
        h = num * _tile_lanes(scale, dv)
        if readout is not None:
            other_ref, om_ref, gain_ref = readout
            h = _rms(h + other_ref[...].astype(F32)) * gain_ref[...] * _sigmoid(om_ref[...].astype(F32))
        h_ref[...] = h.astype(h_ref.dtype)
    ct_ref[...] = decay * ct_ref[...] + lax.dot_general(
        kw.astype(BF16), v_aug, (((0,), (0,)), ((), ())), preferred_element_type=F32)
    m_ref[...] = m_new
    yield


def _interleave(*chunks):
    for _ in range(3):
        for chunk in chunks:
            next(chunk)


def _mlstm_kernel(q_ref, k_ref, v_ref, gr_ref, kc_ref, grc_ref, *rest, reverse, fused):
    c = pl.program_id(2)
    rest = list(rest)
    vc_refs = [rest.pop(0) for _ in range(M_CTX_V_PARTS)]
    other_ref, om_ref, gain_ref = (rest.pop(0), rest.pop(0), rest.pop(0)) if fused else (None, None, None)
    h_ref = rest.pop(0)
    states = [rest[2 * i:2 * i + 2] for i in range(M_HEADS_PER_STEP)]

    def head(ref, i, width):
        return ref.at[:, i * width:(i + 1) * width]

    @pl.when(c == 0)
    def _():
        for ref in rest:
            ref[...] = jnp.zeros_like(ref)
        per = M_HEADS_PER_STEP // M_CTX_V_PARTS
        _interleave(*[_mlstm_chunk(None, head(kc_ref, i, M_DQK), head(vc_refs[i // per], i % per, M_DV), grc_ref.at[i],
                                   *states[i], None, reverse=reverse) for i in range(M_HEADS_PER_STEP)])

    @pl.when(c > 0)
    def _():
        chunks = []
        for i in range(M_HEADS_PER_STEP):
            readout = (head(other_ref, i, M_DV), head(om_ref, i, M_DV), head(gain_ref, i, M_DV)) if fused else None
            chunks.append(_mlstm_chunk(head(q_ref, i, M_DQK), head(k_ref, i, M_DQK), head(v_ref, i, M_DV),
                                       gr_ref.at[i], *states[i], head(h_ref, i, M_DV),
                                       reverse=reverse, readout=readout))
        _interleave(*chunks)


def _mlstm(proj, proj_ctx, gr, gr_ctx, batch, *, reverse, readout=None):
    length = M_CHUNK
    assert CTX_LEN == length
    seq = proj.shape[0] // batch
    nc = seq // length
    hps = M_HEADS_PER_STEP
    qk_w, v_w = hps * M_DQK, hps * M_DV

    def chunk_of(c):
        return (nc - jnp.maximum(c, 1)) if reverse else jnp.maximum(c - 1, 0)

    def rows(b, c):
        return b * nc + chunk_of(c)

    in_specs = [
        pl.BlockSpec((length, qk_w), lambda b, h, c: (rows(b, c), OFF_QM // qk_w + h)),
        pl.BlockSpec((length, qk_w), lambda b, h, c: (rows(b, c), OFF_KM // qk_w + h)),
        pl.BlockSpec((length, v_w), lambda b, h, c: (rows(b, c), OFF_VM // v_w + h)),
        pl.BlockSpec((None, hps, 4, length), lambda b, h, c: (b, h, 0, chunk_of(c))),
        pl.BlockSpec((length, qk_w), lambda b, h, c: (b, CTX_OFF_KM // qk_w + h)),
        pl.BlockSpec((None, hps, 4, length), lambda b, h, c: (b, h, 0, 0)),
    ]
    vc_w = v_w // M_CTX_V_PARTS
    assert CTX_OFF_VM % vc_w == 0
    for part in range(M_CTX_V_PARTS):
        in_specs.append(pl.BlockSpec(
            (length, vc_w), lambda b, h, c, part=part: (b, CTX_OFF_VM // vc_w + h * M_CTX_V_PARTS + part)))
    args = [proj, proj, proj, gr, proj_ctx, gr_ctx] + [proj_ctx] * M_CTX_V_PARTS
    if readout is not None:
        other, gain = readout
        in_specs += [pl.BlockSpec((length, v_w), lambda b, h, c: (rows(b, c), h)),
                     pl.BlockSpec((length, v_w), lambda b, h, c: (rows(b, c), OFF_OM // v_w + h)),
                     pl.BlockSpec((1, v_w), lambda b, h, c: (0, h))]
        args += [other, proj, gain.reshape(1, M_V_W)]
    state = [pltpu.VMEM((M_DQK, M_DV + LANES), F32), pltpu.VMEM((1, 1), F32)]
    return pl.pallas_call(
        functools.partial(_mlstm_kernel, reverse=reverse, fused=readout is not None),
        grid=(batch, M_HEADS // hps, nc + 1),
        in_specs=in_specs,
        out_specs=pl.BlockSpec((length, v_w), lambda b, h, c: (rows(b, c), h)),
        out_shape=jax.ShapeDtypeStruct((batch * seq, M_V_W), BF16),
        scratch_shapes=state * hps,
        compiler_params=_params(3),
        name="mlstm_bwd" if reverse else "mlstm_fwd",
    )(*args)


def _rope(x, cos, sin_signed):
    lane = lax.broadcasted_iota(jnp.int32, x.shape, 1)
    partner = jnp.where((lane % 64) < 32, pltpu.roll(x, LANES - 32, 1), pltpu.roll(x, 32, 1))
    return x * cos + partner * sin_signed


def _attn_kernel(sink_ref, q_ref, k_ref, v_ref, kc_ref, vc_ref, cos_ref, sin_ref, o_ref,
                 kpad, vpad, s_buf, p_buf, den_buf):
    seq = k_ref.shape[0]
    nb = seq // A_BLOCK
    kvh = pl.program_id(1)
    scale = A_HEAD_DIM ** -0.5
    rows = A_GROUP * A_BLOCK
    n_win = 3 * A_BLOCK
    zeros = jnp.zeros((A_BLOCK, A_HEAD_DIM), BF16)
    kpad[0:A_BLOCK, :] = zeros
    kpad[A_BLOCK + seq:, :] = zeros
    vpad[0:A_BLOCK, :] = zeros
    vpad[A_BLOCK + seq:, :] = zeros
    kpad[A_BLOCK:A_BLOCK + seq, :] = _rope(k_ref[...].astype(F32), cos_ref[...], sin_ref[...]).astype(BF16)
    vpad[A_BLOCK:A_BLOCK + seq, :] = v_ref[...]
    qi = lax.broadcasted_iota(jnp.int32, (rows, n_win), 0) & (A_BLOCK - 1)
    kj = lax.broadcasted_iota(jnp.int32, (rows, n_win), 1)
    band = (kj >= qi) & (kj <= qi + 2 * WINDOW)
    kj_row = lax.broadcasted_iota(jnp.int32, (1, n_win), 1)
    sink = jnp.concatenate(
        [jnp.full((A_BLOCK, 1), sink_ref[kvh * A_GROUP + g], F32) for g in range(A_GROUP)], axis=0)
    nt = (((1,), (1,)), ((), ()))

    def scores(n):
        r0 = pl.multiple_of(n * A_BLOCK, A_BLOCK)
        cos = cos_ref[pl.ds(r0, A_BLOCK), :] * scale
        sin = sin_ref[pl.ds(r0, A_BLOCK), :] * scale
        q = jnp.concatenate(
            [_rope(q_ref[pl.ds(r0, A_BLOCK), g * A_HEAD_DIM:(g + 1) * A_HEAD_DIM].astype(F32), cos, sin)
             for g in range(A_GROUP)], axis=0).astype(BF16)
        kpos = r0 - A_BLOCK + kj_row
        valid = band & ((kpos >= 0) & (kpos < seq))
        sw = lax.dot_general(q, kpad[pl.ds(r0, n_win), :], nt, preferred_element_type=F32)
        s_buf[:, 0:n_win] = jnp.where(valid, sw, -jnp.inf)
        s_buf[:, n_win:] = lax.dot_general(q, kc_ref[...], nt, preferred_element_type=F32)

    def softmax():
        s = s_buf[...]
        mx = jnp.maximum(jnp.max(s, axis=1, keepdims=True), sink)
        p = jnp.exp(s - mx)
        den_buf[...] = jnp.sum(p, axis=1, keepdims=True) + jnp.exp(sink - mx)
        p_buf[...] = p.astype(p_buf.dtype)

    def output(n):
        r0 = pl.multiple_of(n * A_BLOCK, A_BLOCK)
        o = (jnp.dot(p_buf[:, 0:n_win], vpad[pl.ds(r0, n_win), :], preferred_element_type=F32)
             + jnp.dot(p_buf[:, n_win:], vc_ref[...], preferred_element_type=F32)) * (1.0 / den_buf[...])
        for g in range(A_GROUP):
            o_ref[pl.ds(r0, A_BLOCK), g * A_HEAD_DIM:(g + 1) * A_HEAD_DIM] = (
                o[g * A_BLOCK:(g + 1) * A_BLOCK, :].astype(o_ref.dtype))

    scores(0)
    softmax()
    scores(1)

    def body(t, carry):
        output(t - 1)
        softmax()
        scores(t + 1)
        return carry

    lax.fori_loop(1, nb - 1, body, 0)
    output(nb - 2)
    softmax()
    output(nb - 1)


def _attention(proj, proj_ctx, sink, cos, sin, batch):
    seq = proj.shape[0] // batch
    qw = A_GROUP * A_HEAD_DIM
    tab = pl.BlockSpec((seq, A_HEAD_DIM), lambda b, h: (0, 0))
    return pl.pallas_call(
        _attn_kernel,
        grid=(batch, A_KV_HEADS),
        in_specs=[pl.BlockSpec(memory_space=pltpu.SMEM),
                  pl.BlockSpec((seq, qw), lambda b, h: (b, OFF_QA // qw + h)),
                  pl.BlockSpec((seq, A_HEAD_DIM), lambda b, h: (b, OFF_KA // A_HEAD_DIM + h)),
                  pl.BlockSpec((seq, A_HEAD_DIM), lambda b, h: (b, OFF_VA // A_HEAD_DIM + h)),
                  pl.BlockSpec((CTX_LEN, A_HEAD_DIM), lambda b, h: (b, CTX_OFF_KA // A_HEAD_DIM + h)),
                  pl.BlockSpec((CTX_LEN, A_HEAD_DIM), lambda b, h: (b, CTX_OFF_VA // A_HEAD_DIM + h)),
                  tab, tab],
        out_specs=pl.BlockSpec((seq, qw), lambda b, h: (b, h)),
        out_shape=jax.ShapeDtypeStruct((batch * seq, A_Q_W), BF16),
        scratch_shapes=[pltpu.VMEM((seq + 2 * A_BLOCK, A_HEAD_DIM), BF16),
                        pltpu.VMEM((seq + 2 * A_BLOCK, A_HEAD_DIM), BF16),
                        pltpu.VMEM((A_GROUP * A_BLOCK, 3 * A_BLOCK + CTX_LEN), F32),
                        pltpu.VMEM((A_GROUP * A_BLOCK, 3 * A_BLOCK + CTX_LEN), BF16),
                        pltpu.VMEM((A_GROUP * A_BLOCK, 1), F32)],
        compiler_params=_params(2),
        name="window_attention",
    )(sink, proj, proj, proj, proj_ctx, proj_ctx, cos, sin)


def _rope_tables(seq):
    t = jnp.arange(seq)
    nf = A_HEAD_DIM // 4
    freqs = ROPE_BASE ** (-jnp.arange(nf, dtype=F32) / nf)
    ang_r = (t // GRID_W).astype(F32)[:, None] * freqs
    ang_c = (t % GRID_W).astype(F32)[:, None] * freqs
    cos = jnp.concatenate([jnp.cos(ang_r), jnp.cos(ang_r), jnp.cos(ang_c), jnp.cos(ang_c)], axis=-1)
    sin = jnp.concatenate([-jnp.sin(ang_r), jnp.sin(ang_r), -jnp.sin(ang_c), jnp.sin(ang_c)], axis=-1)
    return cos, sin


def _gate_rows(g, batch):
    return g[:, :N_GATES].reshape(batch, -1, 4, M_HEADS).transpose(0, 3, 2, 1)


def kernel(x, c, ctx, c_ctx, w_mod, b_mod, norm_g, w_in, m_gate_b, m_norm_g, attn_sink,
           w_out_m, w_out_a, w_o, w_ff1, w_ff2):
    batch, seq, d = x.shape
    n_ctx = ctx.shape[1]
    x2 = x.reshape(batch * seq, d)
    ctx2 = ctx.reshape(batch * n_ctx, d)

    w_in_t = w_in[0].T
    w_gate_t = jnp.pad(w_in_t[W_IN_GATE_LO:W_IN_GATE_HI], ((0, GATE_PAD - N_GATES), (0, 0))).astype(BF16)
    gate_bias = jnp.pad(m_gate_b[0].reshape(1, N_GATES), ((0, 0), (0, GATE_PAD - N_GATES)))

    c_all = jnp.zeros((MOD_ROWS, d), F32).at[:batch].set(c).at[batch].set(c_ctx)
    mod = _modulation(c_all, w_mod[0], b_mod[0]).reshape(MOD_ROWS, 6, 1, d)
    gn = norm_g[0].reshape(4, 1, d)

    u = _prenorm(x2, gn, mod, seq)
    uc = _prenorm(ctx2, gn, mod, n_ctx, fixed_row=batch)

    proj = _stream_matmul(u, w_in_t, n_out=PROJ_ALIGNED_W, out_dtype=BF16, transposed=True, name="in_proj")
    g_lat = _gate_proj(u, w_gate_t, gate_bias)
    ctx_tn = 512
    lat_tiles = (OFF_OM - OFF_KM) // ctx_tn
    proj_ctx = _stream_matmul(
        uc, w_in_t, n_out=CTX_PROJ_W, out_dtype=BF16, transposed=True, tn=ctx_tn, name="in_proj_ctx",
        tile_of=lambda j: jnp.where(j < lat_tiles, j + OFF_KM // ctx_tn, j - lat_tiles + OFF_KA // ctx_tn))
    g_ctx = _gate_proj(uc, w_gate_t, gate_bias, name="gate_proj_ctx")

    g_rows, g_rows_ctx = _gate_rows(g_lat, batch), _gate_rows(g_ctx, batch)
    hb = _mlstm(proj, proj_ctx, g_rows, g_rows_ctx, batch, reverse=True)
    hm = _mlstm(proj, proj_ctx, g_rows, g_rows_ctx, batch, reverse=False, readout=(hb, m_norm_g[0]))

    cos, sin = _rope_tables(seq)
    at = _attention(proj, proj_ctx, attn_sink[0], cos, sin, batch)

    zm = _stream_matmul(hm, w_out_m, n_out=d, out_dtype=BF16, epilogue=_gate_epilogue,
                        extra=[proj], extra_off=[OFF_BGM], name="out_m")
    z = _stream_matmul(at, w_out_a, n_out=d, out_dtype=BF16, epilogue=_gate_add_epilogue,
                       extra=[proj, zm], extra_off=[OFF_BGA, 0], name="out_a_merge")
    mix = _stream_matmul(z, w_o, n_out=d, out_dtype=BF16, name="out_proj")
    h1, u2 = _post_mix(x2, mix, gn, mod, seq)

    hid, w_ff2_bf16 = _stream_matmul(u2, w_ff1, n_out=D_FF, out_dtype=BF16, epilogue=_relu2_epilogue,
                                     side=w_ff2, name="ff1")
    out = _matmul_post_mlp(hid, w_ff2_bf16, h1, gn, mod, seq, tm=512, tk=1024, name="ff2")
    return out.reshape(batch, seq, d)
```
